```python
import jax
import jax.numpy as jnp
from jax import lax
import numpy as np

D_MODEL = 1024
BATCH = 8
SEQ = 4096
DEPTH = 4

GRID_W = 64
CTX_LEN = 256
HEAD_DIM = 64
N_Q_HEADS = D_MODEL // HEAD_DIM
N_KV_HEADS = N_Q_HEADS // 4
GROUP = N_Q_HEADS // N_KV_HEADS
QKV_DIM = (N_Q_HEADS + 2 * N_KV_HEADS) * HEAD_DIM
WINDOW = 128
BLOCK_Q = 128
ROPE_THETA = 10000.0
CONV_WIDTH = 3
N_EXPERTS = 32
TOP_K = 4
D_FF = D_MODEL
SWIGLU_ALPHA = 1.702
SWIGLU_LIMIT = 7.0
EXPERT_BLOCK = 128
N_MIXERS = 3
N_WIN_LAYERS = (DEPTH + 2) // N_MIXERS
N_CONV_LAYERS = (DEPTH + 1) // N_MIXERS
N_FULL_LAYERS = DEPTH // N_MIXERS
DEEPNORM_ALPHA = (2.0 * DEPTH) ** 0.25
DEEPNORM_BETA = (8.0 * DEPTH) ** -0.25
LN_EPS = 1e-5
RMS_EPS = 1e-6
NEG_INF = -1e30

kernel_name = 'hybrid_interleaved_dit_moe'


def layer_norm(x, g, b):
    xf = x.astype(jnp.float32)
    mu = jnp.mean(xf, axis=-1, keepdims=True)
    var = jnp.mean(jnp.square(xf - mu), axis=-1, keepdims=True)
    return ((xf - mu) * lax.rsqrt(var + LN_EPS) * g.astype(jnp.float32) + b.astype(jnp.float32)).astype(x.dtype)


def rms_norm(x, g):
    xf = x.astype(jnp.float32)
    y = xf * lax.rsqrt(jnp.mean(jnp.square(xf), axis=-1, keepdims=True) + RMS_EPS)
    return (y * g.astype(jnp.float32)).astype(x.dtype)


def axial_rope_tables(rows):
    row = jnp.repeat(jnp.arange(rows, dtype=jnp.float32), GRID_W)
    col = jnp.tile(jnp.arange(GRID_W, dtype=jnp.float32), rows)
    n_freq = HEAD_DIM // 4
    inv_freq = ROPE_THETA ** (-jnp.arange(n_freq, dtype=jnp.float32) / n_freq)
    ang = jnp.stack([row[:, None] * inv_freq, col[:, None] * inv_freq], axis=1)
    return jnp.cos(ang), jnp.sin(ang)


def apply_rope(x, cos, sin):
    B, L, H, Dh = x.shape
    xr = x.astype(jnp.float32).reshape(B, L, H, 2, 2, Dh // 4)
    x1, x2 = xr[..., 0, :], xr[..., 1, :]
    cs, sn = cos[None, :, None], sin[None, :, None]
    out = jnp.stack([x1 * cs - x2 * sn, x2 * cs + x1 * sn], axis=-2)
    return out.reshape(B, L, H, Dh).astype(x.dtype)


def project_qkv(h, w, b=None):
    B, L, _ = h.shape
    qkv = h @ w
    if b is not None:
        qkv = qkv + b
    q, k, v = jnp.split(qkv, [N_Q_HEADS * HEAD_DIM, (N_Q_HEADS + N_KV_HEADS) * HEAD_DIM], axis=-1)
    return (q.reshape(B, L, N_Q_HEADS, HEAD_DIM),
            k.reshape(B, L, N_KV_HEADS, HEAD_DIM),
            v.reshape(B, L, N_KV_HEADS, HEAD_DIM))


def attend(q, k, v, mask=None, sink=None):
    s = jnp.einsum('bqhgd,bkhd->bhgqk', q, k).astype(jnp.float32) * (HEAD_DIM ** -0.5)
    if mask is not None:
        s = jnp.where(mask, s, NEG_INF)
    if sink is not None:
        B, H, G, Q, _ = s.shape
        s_sink = jnp.broadcast_to(sink.astype(jnp.float32)[None, :, :, None, None], (B, H, G, Q, 1))
        p = jax.nn.softmax(jnp.concatenate([s, s_sink], axis=-1), axis=-1)[..., :-1]
    else:
        p = jax.nn.softmax(s, axis=-1)
    return jnp.einsum('bhgqk,bkhd->bqhgd', p.astype(v.dtype), v)


def window_attention(hx, hz, w_qkv, b_qkv, sink, w_o, cos, sin, need_ctx):
    B, L, _ = hx.shape
    nb = L // BLOCK_Q
    qx, kx, vx = project_qkv(hx, w_qkv, b_qkv)
    qz, kz, vz = project_qkv(hz, w_qkv, b_qkv)
    qx = apply_rope(qx, cos, sin)
    kx = apply_rope(kx, cos, sin)
    sink_hg = sink.reshape(N_KV_HEADS, GROUP)
    ctx_len = kz.shape[1]

    def band(t):
        tp = jnp.pad(t, ((0, 0), (BLOCK_Q, BLOCK_Q), (0, 0), (0, 0))).reshape(B, nb + 2, BLOCK_Q, N_KV_HEADS, HEAD_DIM)
        return jnp.concatenate([tp[:, :-2], tp[:, 1:-1], tp[:, 2:]], axis=2)

    kb, vb = band(kx), band(vx)
    qb = qx.reshape(B, nb, BLOCK_Q, N_KV_HEADS, GROUP, HEAD_DIM)
    qi = jnp.arange(BLOCK_Q)[:, None]
    kj = jnp.arange(3 * BLOCK_Q)[None, :]
    rel = kj - BLOCK_Q - qi
    ctx_mask = jnp.ones((BLOCK_Q, ctx_len), dtype=bool)

    def block_fn(args):
        q, k, v, blk = args
        key_pos = blk * BLOCK_Q - BLOCK_Q + kj
        m_loc = (jnp.abs(rel) <= WINDOW) & (key_pos >= 0) & (key_pos < L)
        k_cat = jnp.concatenate([k, kz], axis=1)
        v_cat = jnp.concatenate([v, vz], axis=1)
        m_cat = jnp.concatenate([m_loc, ctx_mask], axis=1)
        return attend(q, k_cat, v_cat, mask=m_cat, sink=sink_hg)

    ox = lax.map(block_fn, (jnp.moveaxis(qb, 1, 0), jnp.moveaxis(kb, 1, 0), jnp.moveaxis(vb, 1, 0), jnp.arange(nb)))
    out_x = jnp.moveaxis(ox, 0, 1).reshape(B, L, D_MODEL) @ w_o
    out_z = None
    if need_ctx:
        oz = attend(qz.reshape(B, ctx_len, N_KV_HEADS, GROUP, HEAD_DIM), kz, vz, sink=sink_hg)
        out_z = oz.reshape(B, ctx_len, D_MODEL) @ w_o
    return out_x, out_z


def full_attention(hx, hz, w_qkv, q_norm, k_norm, w_o, cos, sin, need_ctx):
    B, L, _ = hx.shape
    nb = L // BLOCK_Q
    qx, kx, vx = project_qkv(hx, w_qkv)
    qz, kz, vz = project_qkv(hz, w_qkv)
    qx = apply_rope(rms_norm(qx, q_norm), cos, sin)
    kx = apply_rope(rms_norm(kx, k_norm), cos, sin)
    qz = rms_norm(qz, q_norm)
    kz = rms_norm(kz, k_norm)
    ctx_len = kz.shape[1]
    k_all = jnp.concatenate([kx, kz], axis=1)
    v_all = jnp.concatenate([vx, vz], axis=1)
    qb = jnp.moveaxis(qx.reshape(B, nb, BLOCK_Q, N_KV_HEADS, GROUP, HEAD_DIM), 1, 0)
    ox = lax.map(lambda q: attend(q, k_all, v_all), qb)
    out_x = jnp.moveaxis(ox, 0, 1).reshape(B, L, D_MODEL) @ w_o
    out_z = None
    if need_ctx:
        oz = attend(qz.reshape(B, ctx_len, N_KV_HEADS, GROUP, HEAD_DIM), kz, vz)
        out_z = oz.reshape(B, ctx_len, D_MODEL) @ w_o
    return out_x, out_z


def depthwise_conv_centred(u, w):
    pad = CONV_WIDTH // 2
    return lax.conv_general_dilated(u, w[:, None, :], window_strides=(1,), padding=((pad, pad),),
                                    dimension_numbers=('NWC', 'WIO', 'NWC'), feature_group_count=u.shape[-1])


def short_conv_mixer(h, w_in, w_conv, w_out):
    b_gate, c_gate, xv = jnp.split(h @ w_in, 3, axis=-1)
    return (b_gate * depthwise_conv_centred(c_gate * xv, w_conv)) @ w_out


def moe_ffn(h, router_w, router_b, w_gu, b_gu, w_down, b_down):
    T, D = h.shape
    logits = (h @ router_w + router_b).astype(jnp.float32)
    top_val, top_idx = lax.top_k(logits, TOP_K)
    gates = jax.nn.softmax(top_val, axis=-1)
    A = T * TOP_K
    e_flat = top_idx.reshape(A)
    tok_flat = jnp.broadcast_to(jnp.arange(T, dtype=jnp.int32)[:, None], (T, TOP_K)).reshape(A)
    g_flat = gates.reshape(A)
    order = jnp.argsort(e_flat)
    e_sorted = e_flat[order]
    counts = jnp.bincount(e_flat, length=N_EXPERTS)
    starts = jnp.cumsum(counts) - counts
    padded = (counts + EXPERT_BLOCK - 1) // EXPERT_BLOCK * EXPERT_BLOCK
    pends = jnp.cumsum(padded)
    pstarts = pends - padded
    dest = pstarts[e_sorted] + (jnp.arange(A) - starts[e_sorted])
    nblk = -(-A // EXPERT_BLOCK) + N_EXPERTS
    P = nblk * EXPERT_BLOCK
    tok_buf = jnp.full((P,), T, dtype=jnp.int32).at[dest].set(tok_flat[order])
    g_buf = jnp.zeros((P,), dtype=h.dtype).at[dest].set(g_flat[order].astype(h.dtype))
    blk_expert = jnp.minimum(jnp.searchsorted(pends, jnp.arange(nblk) * EXPERT_BLOCK, side='right'), N_EXPERTS - 1)
    h_pad = jnp.concatenate([h, jnp.zeros((1, D), h.dtype)], axis=0)

    def expert_block(args):
        toks, e = args
        xb = h_pad[toks]
        gate, up = jnp.split(xb @ w_gu[e] + b_gu[e], 2, axis=-1)
        gate = jnp.minimum(gate, SWIGLU_LIMIT)
        up = jnp.clip(up, -SWIGLU_LIMIT, SWIGLU_LIMIT)
        act = (up + 1) * gate * jax.nn.sigmoid(SWIGLU_ALPHA * gate)
        return act @ w_down[e] + b_down[e]

    out = lax.map(expert_block, (tok_buf.reshape(nblk, EXPERT_BLOCK), blk_expert)).reshape(P, D)
    y = jnp.zeros((T + 1, D), dtype=h.dtype).at[tok_buf].add(out * g_buf[:, None])
    return y[:T]


def setup_inputs(seed: int = 0) -> dict:
    key = jax.random.key(seed)
    ks = iter(jax.random.split(key, 32))
    D = D_MODEL
    s = D ** -0.5

    def nrm(shape, std):
        return std * jax.random.normal(next(ks), shape, jnp.float32)

    return {
        'x': nrm((BATCH, SEQ, D), 1.0),
        'c': nrm((BATCH, D), 1.0),
        'ctx': nrm((BATCH, CTX_LEN, D), 1.0),
        'c_ctx': nrm((D,), 1.0),
        'mod_w': nrm((DEPTH, D, 6 * D), s),
        'mod_b': nrm((DEPTH, 6 * D), 0.01),
        'ln1_g': 1.0 + nrm((DEPTH, D), 0.01),
        'ln1_b': nrm((DEPTH, D), 0.01),
        'ln2_g': 1.0 + nrm((DEPTH, D), 0.01),
        'ln2_b': nrm((DEPTH, D), 0.01),
        'win_wqkv': nrm((N_WIN_LAYERS, D, QKV_DIM), s),
        'win_bqkv': nrm((N_WIN_LAYERS, QKV_DIM), 0.01),
        'win_sink': nrm((N_WIN_LAYERS, N_Q_HEADS), 0.5),
        'win_wo': nrm((N_WIN_LAYERS, D, D), s * DEEPNORM_BETA),
        'conv_win': nrm((N_CONV_LAYERS, D, 3 * D), s),
        'conv_w': nrm((N_CONV_LAYERS, CONV_WIDTH, D), CONV_WIDTH ** -0.5),
        'conv_wout': nrm((N_CONV_LAYERS, D, D), s * DEEPNORM_BETA),
        'full_wqkv': nrm((N_FULL_LAYERS, D, QKV_DIM), s),
        'full_qnorm': 1.0 + nrm((N_FULL_LAYERS, HEAD_DIM), 0.01),
        'full_knorm': 1.0 + nrm((N_FULL_LAYERS, HEAD_DIM), 0.01),
        'full_wo': nrm((N_FULL_LAYERS, D, D), s * DEEPNORM_BETA),
        'router_w': nrm((DEPTH, D, N_EXPERTS), s),
        'router_b': nrm((DEPTH, N_EXPERTS), 0.01),
        'expert_wgu': nrm((DEPTH, N_EXPERTS, D, 2 * D_FF), s),
        'expert_bgu': nrm((DEPTH, N_EXPERTS, 2 * D_FF), 0.01),
        'expert_wdown': nrm((DEPTH, N_EXPERTS, D_FF, D), D_FF ** -0.5 * DEEPNORM_BETA),
        'expert_bdown': nrm((DEPTH, N_EXPERTS, D), 0.01),
    }


def reference(x, c, ctx, c_ctx, mod_w, mod_b, ln1_g, ln1_b, ln2_g, ln2_b,
              win_wqkv, win_bqkv, win_sink, win_wo, conv_win, conv_w, conv_wout,
              full_wqkv, full_qnorm, full_knorm, full_wo,
              router_w, router_b, expert_wgu, expert_bgu, expert_wdown, expert_bdown):
    B, L, D = x.shape
    ROWS = L // GRID_W
    cos, sin = axial_rope_tables(ROWS)
    silu_c = jax.nn.silu(c)
    silu_cz = jax.nn.silu(c_ctx)
    z = ctx
    for i in range(DEPTH):
        kind, j = i % N_MIXERS, i // N_MIXERS
        need_ctx = i < DEPTH - 1
        mx = (silu_c @ mod_w[i] + mod_b[i])[:, None, :]
        sh1, sc1, g1, sh2, sc2, g2 = jnp.split(mx, 6, axis=-1)
        mz = silu_cz @ mod_w[i] + mod_b[i]
        zsh1, zsc1, zg1, zsh2, zsc2, zg2 = jnp.split(mz, 6, axis=-1)

        hx = x * (1 + sc1) + sh1
        hz = z * (1 + zsc1) + zsh1
        if kind == 0:
            ox, oz = window_attention(hx, hz, win_wqkv[j], win_bqkv[j], win_sink[j], win_wo[j], cos, sin, need_ctx)
        elif kind == 1:
            ox = short_conv_mixer(hx, conv_win[j], conv_w[j], conv_wout[j])
            oz = short_conv_mixer(hz, conv_win[j], conv_w[j], conv_wout[j]) if need_ctx else None
        else:
            ox, oz = full_attention(hx, hz, full_wqkv[j], full_qnorm[j], full_knorm[j], full_wo[j], cos, sin, need_ctx)
        x = layer_norm(DEEPNORM_ALPHA * x + g1 * ox, ln1_g[i], ln1_b[i])

        hx = x * (1 + sc2) + sh2
        moe_args = (router_w[i], router_b[i], expert_wgu[i], expert_bgu[i], expert_wdown[i], expert_bdown[i])
        if need_ctx:
            z = layer_norm(DEEPNORM_ALPHA * z + zg1 * oz, ln1_g[i], ln1_b[i])
            hz = z * (1 + zsc2) + zsh2
            ctx_len = z.shape[1]
            tokens = jnp.concatenate([hx.reshape(B * L, D), hz.reshape(B * ctx_len, D)], axis=0)
            f = moe_ffn(tokens, *moe_args)
            fx = f[:B * L].reshape(B, L, D)
            fz = f[B * L:].reshape(B, ctx_len, D)
            z = layer_norm(DEEPNORM_ALPHA * z + zg2 * fz, ln2_g[i], ln2_b[i])
        else:
            fx = moe_ffn(hx.reshape(B * L, D), *moe_args).reshape(B, L, D)
        x = layer_norm(DEEPNORM_ALPHA * x + g2 * fx, ln2_g[i], ln2_b[i])
    return x
```

```python
import functools

import jax
import jax.numpy as jnp
from jax import lax
from jax.experimental import pallas as pl
from jax.experimental.pallas import tpu as pltpu

HEAD_DIM = 64
GROUP = 4
GRID_W = 64
WINDOW = 128
ROPE_THETA = 10000.0
TOP_K = 4
N_MIXERS = 3
SWIGLU_ALPHA = 1.702
SWIGLU_LIMIT = 7.0
LN_EPS = 1e-5
RMS_EPS = 1e-6
NEG_INF = -1e30

LANES = 128
MOD_ROWS = 8
VMEM_LIMIT = 56 * 1024 * 1024

F32 = jnp.float32
BF16 = jnp.bfloat16


def _cparams(sem):
    return pltpu.CompilerParams(dimension_semantics=sem, vmem_limit_bytes=VMEM_LIMIT)


def _mod_kernel(c_ref, w_ref, b_ref, o_ref):
    c = c_ref[...]
    s = c * jax.nn.sigmoid(c)
    o_ref[0] = jnp.dot(s, w_ref[0], preferred_element_type=F32, precision=lax.Precision.HIGHEST) + b_ref[0]


def _modulation(c_rows, mod_w, mod_b):
    depth, d, n = mod_w.shape
    r = c_rows.shape[0]
    tn = min(n, 1536)
    return pl.pallas_call(
        _mod_kernel,
        grid=(depth, n // tn),
        in_specs=[pl.BlockSpec((r, d), lambda l, j: (0, 0)),
                  pl.BlockSpec((1, d, tn), lambda l, j: (l, 0, j)),
                  pl.BlockSpec((1, 1, tn), lambda l, j: (l, 0, j))],
        out_specs=pl.BlockSpec((1, r, tn), lambda l, j: (l, 0, j)),
        out_shape=jax.ShapeDtypeStruct((depth, r, n), F32),
        compiler_params=_cparams(("arbitrary", "arbitrary")),
        name="modulation",
    )(c_rows, mod_w, mod_b.reshape(depth, 1, n))


def _swap_pairs(t):
    lane = lax.broadcasted_iota(jnp.int32, t.shape, 1)
    return jnp.where(lane % 32 < 16, pltpu.roll(t, LANES - 16, 1), pltpu.roll(t, 16, 1))


def _head_mean_sq(t, seg_ref):
    t2 = t * t
    hi = t2.astype(BF16)
    lo = (t2 - hi.astype(F32)).astype(BF16)
    seg = seg_ref[...]
    s = jnp.dot(hi, seg, preferred_element_type=F32) + jnp.dot(lo, seg, preferred_element_type=F32)
    return s * (1.0 / HEAD_DIM)


def _qkv_kernel(*refs, nq, nk, rope, qk_norm):
    x_ref, mod_ref, w_ref, b_ref = refs[:4]
    pos = 4
    if rope:
        cos_ref, sin_ref = refs[pos:pos + 2]
        pos += 2
    if qk_norm:
        seg_ref, qg_ref, kg_ref = refs[pos:pos + 3]
        pos += 3
    q_ref, k_ref, v_ref = refs[pos:pos + 3]
    m = mod_ref[0]
    h = (x_ref[0] * (1.0 + m[1:2]) + m[0:1]).astype(BF16)
    if rope:
        cos, sin = cos_ref[...], sin_ref[...]
    for c in range((nq + 2 * nk) // LANES):
        lo = c * LANES
        t = jnp.dot(h, w_ref[:, lo:lo + LANES], preferred_element_type=F32) + b_ref[:, lo:lo + LANES]
        if lo < nq + nk:
            is_q = lo < nq
            if qk_norm:
                gain = qg_ref[...] if is_q else kg_ref[...]
                t = t * lax.rsqrt(_head_mean_sq(t, seg_ref) + RMS_EPS) * gain
            if rope:
                t = t * cos + _swap_pairs(t) * sin
            if is_q:
                q_ref[0, :, lo:lo + LANES] = (t * HEAD_DIM ** -0.5).astype(BF16)
            else:
                k_ref[0, :, lo - nq:lo - nq + LANES] = t.astype(BF16)
        else:
            v_ref[0, :, lo - nq - nk:lo - nq - nk + LANES] = t.astype(BF16)


def _dup_heads(w, n_heads):
    lead = w.shape[:-1]
    w = w.reshape(lead + (n_heads, 1, HEAD_DIM))
    return jnp.broadcast_to(w, lead + (n_heads, 2, HEAD_DIM)).reshape(lead + (n_heads * LANES,))


def _qkv_project(x, mod, w, b, *, tables=None, norms=None, tm):
    bsz, seq, d = x.shape
    n_kv = (w.shape[1] - d) // (2 * HEAD_DIM)
    nq, nk = d, n_kv * LANES
    wq, wk, wv = w[:, :d], w[:, d:d + n_kv * HEAD_DIM], w[:, d + n_kv * HEAD_DIM:]
    w_ext = jnp.concatenate([wq, _dup_heads(wk, n_kv), _dup_heads(wv, n_kv)], axis=1).astype(BF16)
    bq, bk, bv = b[:d], b[d:d + n_kv * HEAD_DIM], b[d + n_kv * HEAD_DIM:]
    b_ext = jnp.concatenate([bq, _dup_heads(bk, n_kv), _dup_heads(bv, n_kv)])[None, :].astype(F32)
    n = nq + 2 * nk
    args = [x, mod, w_ext, b_ext]
    specs = [pl.BlockSpec((1, tm, d), lambda bi, i: (bi, i, 0)),
             pl.BlockSpec((1, MOD_ROWS, d), lambda bi, i: (bi, 0, 0)),
             pl.BlockSpec((d, n), lambda bi, i: (0, 0)),
             pl.BlockSpec((1, n), lambda bi, i: (0, 0))]
    if tables is not None:
        args += list(tables)
        specs += [pl.BlockSpec((tm, LANES), lambda bi, i: (i, 0))] * 2
    if norms is not None:
        qn, kn = norms
        seg = (jnp.arange(LANES)[:, None] // HEAD_DIM == jnp.arange(LANES)[None, :] // HEAD_DIM).astype(BF16)
        args += [seg, jnp.tile(qn, 2)[None, :].astype(F32), jnp.tile(kn, 2)[None, :].astype(F32)]
        specs += [pl.BlockSpec((LANES, LANES), lambda bi, i: (0, 0)),
                  pl.BlockSpec((1, LANES), lambda bi, i: (0, 0)),
                  pl.BlockSpec((1, LANES), lambda bi, i: (0, 0))]
    return pl.pallas_call(
        functools.partial(_qkv_kernel, nq=nq, nk=nk, rope=tables is not None, qk_norm=norms is not None),
        grid=(bsz, seq // tm),
        in_specs=specs,
        out_specs=[pl.BlockSpec((1, tm, nq), lambda bi, i: (bi, i, 0)),
                   pl.BlockSpec((1, tm, nk), lambda bi, i: (bi, i, 0)),
                   pl.BlockSpec((1, tm, nk), lambda bi, i: (bi, i, 0))],
        out_shape=[jax.ShapeDtypeStruct((bsz, seq, nq), BF16),
                   jax.ShapeDtypeStruct((bsz, seq, nk), BF16),
                   jax.ShapeDtypeStruct((bsz, seq, nk), BF16)],
        compiler_params=_cparams(("parallel", "parallel")),
        name="qkv_project",
    )(*args)


def _conv_in_kernel(x_ref, mod_ref, w_ref, bg_ref, u_ref, *, d):
    m = mod_ref[0]
    h = (x_ref[0] * (1.0 + m[1:2]) + m[0:1]).astype(BF16)
    for c in range(d // LANES):
        lo = c * LANES
        bg = jnp.dot(h, w_ref[:, lo:lo + LANES], preferred_element_type=F32)
        cg = jnp.dot(h, w_ref[:, d + lo:d + lo + LANES], preferred_element_type=F32)
        xv = jnp.dot(h, w_ref[:, 2 * d + lo:2 * d + lo + LANES], preferred_element_type=F32)
        bg_ref[0, :, lo:lo + LANES] = bg.astype(BF16)
        u_ref[0, :, lo:lo + LANES] = (cg * xv).astype(BF16)


def _conv_in_project(x, mod, w_in, *, tm):
    bsz, seq, d = x.shape
    return pl.pallas_call(
        functools.partial(_conv_in_kernel, d=d),
        grid=(bsz, seq // tm),
        in_specs=[pl.BlockSpec((1, tm, d), lambda bi, i: (bi, i, 0)),
                  pl.BlockSpec((1, MOD_ROWS, d), lambda bi, i: (bi, 0, 0)),
                  pl.BlockSpec((d, 3 * d), lambda bi, i: (0, 0))],
        out_specs=[pl.BlockSpec((1, tm, d), lambda bi, i: (bi, i, 0))] * 2,
        out_shape=[jax.ShapeDtypeStruct((bsz, seq, d), BF16)] * 2,
        compiler_params=_cparams(("parallel", "parallel")),
        name="conv_in_project",
    )(x, mod, w_in.astype(BF16))


def _group_heads(q_ref, lane):
    for pair in range(GROUP // 2):
        qp = q_ref[0, :, pair * LANES:(pair + 1) * LANES]
        yield jnp.where(lane < HEAD_DIM, qp, jnp.zeros_like(qp))
        yield jnp.where(lane >= HEAD_DIM, qp, jnp.zeros_like(qp))


def _store_group(o_ref, outs, lane):
    for pair in range(GROUP // 2):
        o = jnp.where(lane < HEAD_DIM, outs[2 * pair], outs[2 * pair + 1])
        o_ref[0, :, pair * LANES:(pair + 1) * LANES] = o.astype(o_ref.dtype)


def _qk(q, k):
    return lax.dot_general(q, k, (((1,), (1,)), ((), ())), preferred_element_type=F32)


def _win_attn_kernel(q_ref, kp_ref, kc_ref, kn_ref, kz_ref, vp_ref, vc_ref, vn_ref, vz_ref, sink_ref, o_ref,
                     *, tq, seq):
    hk = pl.program_id(1)
    i = pl.program_id(2)
    k = jnp.concatenate([kp_ref[0], kc_ref[0], kn_ref[0], kz_ref[0]], axis=0)
    v = jnp.concatenate([vp_ref[0], vc_ref[0], vn_ref[0], vz_ref[0]], axis=0)
    n_local = tq + 2 * WINDOW
    n_keys = k.shape[0]
    qpos = i * tq + lax.broadcasted_iota(jnp.int32, (tq, n_keys), 0)
    col = lax.broadcasted_iota(jnp.int32, (tq, n_keys), 1)
    kpos = i * tq - WINDOW + col
    mask = (col >= n_local) | ((jnp.abs(kpos - qpos) <= WINDOW) & (kpos >= 0) & (kpos < seq))
    lane = lax.broadcasted_iota(jnp.int32, (tq, LANES), 1)
    outs = []
    for g, qg in enumerate(_group_heads(q_ref, lane)):
        s = jnp.where(mask, _qk(qg, k), NEG_INF)
        sink = sink_ref[hk, g]
        mx = jnp.maximum(jnp.max(s, axis=1, keepdims=True), sink)
        p = jnp.exp(s - mx)
        denom = jnp.sum(p, axis=1, keepdims=True) + jnp.exp(sink - mx)
        outs.append(jnp.dot(p.astype(BF16), v, preferred_element_type=F32) / denom)
    _store_group(o_ref, outs, lane)


def _window_attention(q, k, v, kz, vz, sink, *, tq):
    bsz, seq, d = q.shape
    n_kv = k.shape[2] // LANES
    ctx = kz.shape[1]
    r = tq // WINDOW
    last = seq // WINDOW - 1
    cur = lambda bi, h, i: (bi, i, h)
    prev = lambda bi, h, i: (bi, jnp.maximum(i * r - 1, 0), h)
    nxt = lambda bi, h, i: (bi, jnp.minimum((i + 1) * r, last), h)
    zmap = lambda bi, h, i: (bi, 0, h)
    kv_specs = [pl.BlockSpec((1, WINDOW, LANES), prev), pl.BlockSpec((1, tq, LANES), cur),
                pl.BlockSpec((1, WINDOW, LANES), nxt), pl.BlockSpec((1, ctx, LANES), zmap)]
    return pl.pallas_call(
        functools.partial(_win_attn_kernel, tq=tq, seq=seq),
        grid=(bsz, n_kv, seq // tq),
        in_specs=[pl.BlockSpec((1, tq, GROUP * HEAD_DIM), cur)] + kv_specs + kv_specs
                 + [pl.BlockSpec(memory_space=pltpu.SMEM)],
        out_specs=pl.BlockSpec((1, tq, GROUP * HEAD_DIM), cur),
        out_shape=jax.ShapeDtypeStruct((bsz, seq, d), BF16),
        compiler_params=_cparams(("parallel", "parallel", "parallel")),
        name="window_attention",
    )(q, k, k, k, kz, v, v, v, vz, sink.reshape(n_kv, GROUP).astype(F32))


def _flash_kernel(*refs, has_sink):
    if has_sink:
        q_ref, k_ref, v_ref, sink_ref, o_ref, m_ref, l_ref, acc_ref = refs
    else:
        q_ref, k_ref, v_ref, o_ref, m_ref, l_ref, acc_ref = refs
    hk = pl.program_id(1)
    j = pl.program_id(3)
    tq = q_ref.shape[1]

    @pl.when(j == 0)
    def _():
        for g in range(GROUP):
            if has_sink:
                m_ref[g] = jnp.full((tq, 1), sink_ref[hk, g], F32)
                l_ref[g] = jnp.ones((tq, 1), F32)
            else:
                m_ref[g] = jnp.full((tq, 1), NEG_INF, F32)
                l_ref[g] = jnp.zeros((tq, 1), F32)
        acc_ref[...] = jnp.zeros_like(acc_ref)

    k = k_ref[0]
    v = v_ref[0]
    lane = lax.broadcasted_iota(jnp.int32, (tq, LANES), 1)
    for g, qg in enumerate(_group_heads(q_ref, lane)):
        s = _qk(qg, k)
        m_prev = m_ref[g]
        m_new = jnp.maximum(m_prev, jnp.max(s, axis=1, keepdims=True))
        alpha = jnp.exp(m_prev - m_new)
        p = jnp.exp(s - m_new)
        l_ref[g] = alpha * l_ref[g] + jnp.sum(p, axis=1, keepdims=True)
        acc_ref[g] = alpha * acc_ref[g] + jnp.dot(p.astype(BF16), v, preferred_element_type=F32)
        m_ref[g] = m_new

    @pl.when(j == pl.num_programs(3) - 1)
    def _():
        _store_group(o_ref, [acc_ref[g] / l_ref[g] for g in range(GROUP)], lane)


def _flash_attention(q, k, v, sink=None, *, tq, tk):
    bsz, seq, d = q.shape
    n_kv = k.shape[2] // LANES
    qmap = lambda bi, h, i, j: (bi, i, h)
    kmap = lambda bi, h, i, j: (bi, j, h)
    args = [q, k, v]
    specs = [pl.BlockSpec((1, tq, GROUP * HEAD_DIM), qmap),
             pl.BlockSpec((1, tk, LANES), kmap), pl.BlockSpec((1, tk, LANES), kmap)]
    if sink is not None:
        args.append(sink.reshape(n_kv, GROUP).astype(F32))
        specs.append(pl.BlockSpec(memory_space=pltpu.SMEM))
    return pl.pallas_call(
        functools.partial(_flash_kernel, has_sink=sink is not None),
        grid=(bsz, n_kv, seq // tq, k.shape[1] // tk),
        in_specs=specs,
        out_specs=pl.BlockSpec((1, tq, GROUP * HEAD_DIM), qmap),
        out_shape=jax.ShapeDtypeStruct((bsz, seq, d), BF16),
        scratch_shapes=[pltpu.VMEM((GROUP, tq, 1), F32), pltpu.VMEM((GROUP, tq, 1), F32),
                        pltpu.VMEM((GROUP, tq, LANES), F32)],
        compiler_params=_cparams(("parallel", "parallel", "parallel", "arbitrary")),
        name="flash_attention",
    )(*args)


def _layer_norm(r, g, b):
    mu = jnp.mean(r, axis=-1, keepdims=True)
    rc = r - mu
    var = jnp.mean(rc * rc, axis=-1, keepdims=True)
    return rc * lax.rsqrt(var + LN_EPS) * g + b


def _mixer_out_kernel(*refs, conv, alpha, seq):
    if conv:
        bg_ref, u_ref, up_ref, un_ref, cw_ref = refs[:5]
        refs = refs[5:]
    else:
        o_ref = refs[0]
        refs = refs[1:]
    w_ref, x_ref, mod_ref, lng_ref, lnb_ref, rwh_ref, rwl_ref, rb_ref, xo_ref, h_ref, lg_ref = refs
    if conv:
        i = pl.program_id(1)
        u = u_ref[0].astype(F32)
        tm = u.shape[0]
        row = lax.broadcasted_iota(jnp.int32, u.shape, 0)
        halo = up_ref.shape[1]
        before = jnp.where(i == 0, 0.0, up_ref[0, halo - 1:halo, :].astype(F32))
        after = jnp.where(i == seq // tm - 1, 0.0, un_ref[0, 0:1, :].astype(F32))
        u_prev = jnp.where(row == 0, before, pltpu.roll(u, 1, 0))
        u_next = jnp.where(row == tm - 1, after, pltpu.roll(u, tm - 1, 0))
        cw = cw_ref[...]
        y = cw[0:1] * u_prev + cw[1:2] * u + cw[2:3] * u_next
        mixed = (bg_ref[0].astype(F32) * y).astype(BF16)
    else:
        mixed = o_ref[0]
    m = mod_ref[0]
    ox = jnp.dot(mixed, w_ref[...], preferred_element_type=F32)
    xn = _layer_norm(alpha * x_ref[0] + m[2:3] * ox, lng_ref[...], lnb_ref[...])
    xo_ref[0] = xn
    h2 = xn * (1.0 + m[4:5]) + m[3:4]
    hh = h2.astype(BF16)
    hl = (h2 - hh.astype(F32)).astype(BF16)
    h_ref[0] = hh
    rwh = rwh_ref[...]
    lg_ref[0] = (jnp.dot(hh, rwh, preferred_element_type=F32) + jnp.dot(hl, rwh, preferred_element_type=F32)
                 + jnp.dot(hh, rwl_ref[...], preferred_element_type=F32) + rb_ref[...])


def _mixer_out(mixed, w_out, x, mod, ln_g, ln_b, router_w, router_b, *, alpha, tm, conv_w=None):
    bsz, seq, d = x.shape
    n_exp = router_w.shape[1]
    tile = pl.BlockSpec((1, tm, d), lambda bi, i: (bi, i, 0))
    row = pl.BlockSpec((1, d), lambda bi, i: (0, 0))
    conv = conv_w is not None
    if conv:
        bg, u = mixed
        halo = 16
        r = tm // halo
        last = seq // halo - 1
        args = [bg, u, u, u, jnp.pad(conv_w.astype(F32), ((0, MOD_ROWS - conv_w.shape[0]), (0, 0)))]
        specs = [tile, tile,
                 pl.BlockSpec((1, halo, d), lambda bi, i: (bi, jnp.maximum(i * r - 1, 0), 0)),
                 pl.BlockSpec((1, halo, d), lambda bi, i: (bi, jnp.minimum((i + 1) * r, last), 0)),
                 pl.BlockSpec((MOD_ROWS, d), lambda bi, i: (0, 0))]
    else:
        args, specs = [mixed], [tile]
    rw_hi = router_w.astype(BF16)
    rw_lo = (router_w - rw_hi.astype(F32)).astype(BF16)
    args += [w_out.astype(BF16), x, mod, ln_g[None, :], ln_b[None, :], rw_hi, rw_lo, router_b[None, :]]
    specs += [pl.BlockSpec((d, d), lambda bi, i: (0, 0)), tile,
              pl.BlockSpec((1, MOD_ROWS, d), lambda bi, i: (bi, 0, 0)), row, row,
              pl.BlockSpec((d, n_exp), lambda bi, i: (0, 0)), pl.BlockSpec((d, n_exp), lambda bi, i: (0, 0)),
              pl.BlockSpec((1, n_exp), lambda bi, i: (0, 0))]
    return pl.pallas_call(
        functools.partial(_mixer_out_kernel, conv=conv, alpha=alpha, seq=seq),
        grid=(bsz, seq // tm),
        in_specs=specs,
        out_specs=[tile, tile, pl.BlockSpec((1, tm, n_exp), lambda bi, i: (bi, i, 0))],
        out_shape=[jax.ShapeDtypeStruct((bsz, seq, d), F32), jax.ShapeDtypeStruct((bsz, seq, d), BF16),
                   jax.ShapeDtypeStruct((bsz, seq, n_exp), F32)],
        compiler_params=_cparams(("parallel", "parallel")),
        name="mixer_out",
    )(*args)


def _expert_kernel(be_ref, first_ref, used_ref, x_ref, g_ref, wgu_ref, bgu_ref, wdn_ref, bdn_ref, o_ref,
                   wgu_bf, wdn_bf, *, ff, fc):
    i = pl.program_id(0)

    @pl.when(first_ref[i] == 1)
    def _():
        wgu_bf[...] = wgu_ref[0].astype(BF16)
        wdn_bf[...] = wdn_ref[0].astype(BF16)

    @pl.when(i < used_ref[0])
    def _():
        xb = x_ref[...]
        acc = jnp.zeros(o_ref.shape, F32)
        for c in range(ff // fc):
            lo = c * fc
            gate = jnp.dot(xb, wgu_bf[:, lo:lo + fc], preferred_element_type=F32) + bgu_ref[0, :, lo:lo + fc]
            up = jnp.dot(xb, wgu_bf[:, ff + lo:ff + lo + fc], preferred_element_type=F32) \
                + bgu_ref[0, :, ff + lo:ff + lo + fc]
            gate = jnp.minimum(gate, SWIGLU_LIMIT)
            up = jnp.clip(up, -SWIGLU_LIMIT, SWIGLU_LIMIT)
            act = (up + 1.0) * gate * jax.nn.sigmoid(SWIGLU_ALPHA * gate)
            acc = acc + jnp.dot(act.astype(BF16), wdn_bf[lo:lo + fc, :], preferred_element_type=F32)
        o_ref[...] = (acc + bdn_ref[0]) * g_ref[...]

    @pl.when(i >= used_ref[0])
    def _():
        o_ref[...] = jnp.zeros_like(o_ref)


def _expert_ffn(xs, gates, blk_expert, blk_first, n_used, w_gu, b_gu, w_down, b_down, *, bm):
    p, d = xs.shape
    n_exp, _, ff2 = w_gu.shape
    ff = ff2 // 2
    grid_spec = pltpu.PrefetchScalarGridSpec(
        num_scalar_prefetch=3,
        grid=(p // bm,),
        in_specs=[pl.BlockSpec((bm, d), lambda i, be, fi, nu: (i, 0)),
                  pl.BlockSpec((bm, 1), lambda i, be, fi, nu: (i, 0)),
                  pl.BlockSpec((1, d, ff2), lambda i, be, fi, nu: (be[i], 0, 0)),
                  pl.BlockSpec((1, 1, ff2), lambda i, be, fi, nu: (be[i], 0, 0)),
                  pl.BlockSpec((1, ff, d), lambda i, be, fi, nu: (be[i], 0, 0)),
                  pl.BlockSpec((1, 1, d), lambda i, be, fi, nu: (be[i], 0, 0))],
        out_specs=pl.BlockSpec((bm, d), lambda i, be, fi, nu: (i, 0)),
        scratch_shapes=[pltpu.VMEM((d, ff2), BF16), pltpu.VMEM((ff, d), BF16)],
    )
    return pl.pallas_call(
        functools.partial(_expert_kernel, ff=ff, fc=min(ff, 512)),
        grid_spec=grid_spec,
        out_shape=jax.ShapeDtypeStruct((p, d), F32),
        compiler_params=_cparams(("arbitrary",)),
        name="expert_ffn",
    )(blk_expert, blk_first, n_used, xs, gates, w_gu, b_gu.reshape(n_exp, 1, ff2), w_down,
      b_down.reshape(n_exp, 1, d))


def _route(logits, n_exp, bm):
    t = logits.shape[0]
    a = t * TOP_K
    top_val, top_idx = lax.top_k(logits, TOP_K)
    gates = jax.nn.softmax(top_val, axis=-1)
    e_flat = top_idx.reshape(a)
    onehot = (e_flat[:, None] == jnp.arange(n_exp, dtype=e_flat.dtype)[None, :]).astype(jnp.int32)
    csum = jnp.cumsum(onehot, axis=0)
    counts = csum[-1]
    rank = jnp.sum(csum * onehot, axis=1) - 1
    padded = (counts + bm - 1) // bm * bm
    pends = jnp.cumsum(padded)
    pstarts = pends - padded
    dest = pstarts[e_flat] + rank
    nblk = -(-a // bm) + n_exp
    tok_flat = jnp.arange(a, dtype=jnp.int32) // TOP_K
    tok_buf = jnp.zeros((nblk * bm,), jnp.int32).at[dest].set(tok_flat)
    g_buf = jnp.zeros((nblk * bm,), F32).at[dest].set(gates.reshape(a))
    blk_start = jnp.arange(nblk, dtype=jnp.int32) * bm
    blk_expert = jnp.minimum(jnp.searchsorted(pends, blk_start, side='right'), n_exp - 1).astype(jnp.int32)
    blk_first = jnp.concatenate([jnp.ones((1,), jnp.int32),
                                 (blk_expert[1:] != blk_expert[:-1]).astype(jnp.int32)])
    n_used = (pends[-1] // bm).astype(jnp.int32).reshape(1)
    return tok_buf, g_buf, dest.reshape(t, TOP_K), blk_expert, blk_first, n_used


def _combine_kernel(y_ref, x_ref, mod_ref, lng_ref, lnb_ref, o_ref, *, alpha):
    fx = y_ref[0]
    for kk in range(1, y_ref.shape[0]):
        fx = fx + y_ref[kk]
    m = mod_ref[0]
    o_ref[0] = _layer_norm(alpha * x_ref[0] + m[5:6] * fx, lng_ref[...], lnb_ref[...])


def _combine(y4, row0, x, mod, ln_g, ln_b, *, alpha, tm):
    bsz, seq, d = x.shape
    nt = seq // tm
    off = row0 // tm
    tile = pl.BlockSpec((1, tm, d), lambda bi, i: (bi, i, 0))
    row = pl.BlockSpec((1, d), lambda bi, i: (0, 0))
    return pl.pallas_call(
        functools.partial(_combine_kernel, alpha=alpha),
        grid=(bsz, nt),
        in_specs=[pl.BlockSpec((y4.shape[0], tm, d), lambda bi, i: (0, off + bi * nt + i, 0)),
                  tile, pl.BlockSpec((1, MOD_ROWS, d), lambda bi, i: (bi, 0, 0)), row, row],
        out_specs=tile,
        out_shape=jax.ShapeDtypeStruct((bsz, seq, d), F32),
        compiler_params=_cparams(("parallel", "parallel")),
        name="combine_norm",
    )(y4, x, mod, ln_g[None, :], ln_b[None, :])


def _rope_tables(seq):
    t = jnp.arange(seq, dtype=jnp.int32)
    row = (t // GRID_W).astype(F32)
    col = (t % GRID_W).astype(F32)
    n_freq = HEAD_DIM // 4
    inv_freq = ROPE_THETA ** (-jnp.arange(n_freq, dtype=F32) / n_freq)
    ar, ac = row[:, None] * inv_freq, col[:, None] * inv_freq
    cos = jnp.concatenate([jnp.cos(ar), jnp.cos(ar), jnp.cos(ac), jnp.cos(ac)], axis=1)
    sin = jnp.concatenate([-jnp.sin(ar), jnp.sin(ar), -jnp.sin(ac), jnp.sin(ac)], axis=1)
    return jnp.tile(cos, (1, 2)), jnp.tile(sin, (1, 2))


def _tile(n, want):
    t = min(n, want)
    while n % t:
        t //= 2
    return t


def kernel(x, c, ctx, c_ctx, mod_w, mod_b, ln1_g, ln1_b, ln2_g, ln2_b, win_wqkv, win_bqkv, win_sink, win_wo,
           conv_win, conv_w, conv_wout, full_wqkv, full_qnorm, full_knorm, full_wo,
           router_w, router_b, expert_wgu, expert_bgu, expert_wdown, expert_bdown):
    bsz, seq, d = x.shape
    ctx_len = ctx.shape[1]
    depth = mod_w.shape[0]
    n_exp = router_w.shape[2]
    alpha = (2.0 * depth) ** 0.25
    tables = _rope_tables(seq)
    tm_x, tm_z = _tile(seq, 512), _tile(ctx_len, 512)
    tq_win = _tile(seq, 256)
    tq_full = _tile(seq, 256)
    bm = 512

    c_rows = jnp.zeros((2 * MOD_ROWS, d), F32).at[:bsz].set(c).at[bsz].set(c_ctx)
    mods = _modulation(c_rows, mod_w, mod_b)
    z = ctx
    for i in range(depth):
        kind, j = i % N_MIXERS, i // N_MIXERS
        need_ctx = i < depth - 1
        mod_x = jnp.pad(mods[i, :bsz].reshape(bsz, 6, d), ((0, 0), (0, MOD_ROWS - 6), (0, 0)))
        mod_z = jnp.broadcast_to(jnp.pad(mods[i, bsz].reshape(1, 6, d), ((0, 0), (0, MOD_ROWS - 6), (0, 0))),
                                 (bsz, MOD_ROWS, d))
        route_args = (router_w[i], router_b[i])
        ln1 = (ln1_g[i], ln1_b[i])
        oz = None
        if kind == 0:
            qx, kx, vx = _qkv_project(x, mod_x, win_wqkv[j], win_bqkv[j], tables=tables, tm=tm_x)
            qz, kz, vz = _qkv_project(z, mod_z, win_wqkv[j], win_bqkv[j], tm=tm_z)
            ox = _window_attention(qx, kx, vx, kz, vz, win_sink[j], tq=tq_win)
            if need_ctx:
                oz = _flash_attention(qz, kz, vz, win_sink[j], tq=ctx_len, tk=ctx_len)
            w_out, conv_taps = win_wo[j], None
        elif kind == 1:
            ox = _conv_in_project(x, mod_x, conv_win[j], tm=tm_x)
            if need_ctx:
                oz = _conv_in_project(z, mod_z, conv_win[j], tm=tm_z)
            w_out, conv_taps = conv_wout[j], conv_w[j]
        else:
            zero_b = jnp.zeros((full_wqkv.shape[2],), F32)
            norms = (full_qnorm[j], full_knorm[j])
            qx, kx, vx = _qkv_project(x, mod_x, full_wqkv[j], zero_b, tables=tables, norms=norms, tm=tm_x)
            qz, kz, vz = _qkv_project(z, mod_z, full_wqkv[j], zero_b, norms=norms, tm=tm_z)
            k_all = jnp.concatenate([kx, kz], axis=1)
            v_all = jnp.concatenate([vx, vz], axis=1)
            n_all = seq + ctx_len
            tk = n_all // 4 if n_all % 32 == 0 else n_all
            ox = _flash_attention(qx, k_all, v_all, tq=tq_full, tk=tk)
            if need_ctx:
                oz = _flash_attention(qz, kz, vz, tq=ctx_len, tk=ctx_len)
            w_out, conv_taps = full_wo[j], None

        x, hx, lgx = _mixer_out(ox, w_out, x, mod_x, *ln1, *route_args, alpha=alpha, tm=tm_x, conv_w=conv_taps)
        h_all, lg_all = hx.reshape(bsz * seq, d), lgx.reshape(bsz * seq, n_exp)
        if need_ctx:
            z, hz, lgz = _mixer_out(oz, w_out, z, mod_z, *ln1, *route_args, alpha=alpha, tm=tm_z,
                                    conv_w=conv_taps)
            h_all = jnp.concatenate([h_all, hz.reshape(bsz * ctx_len, d)], axis=0)
            lg_all = jnp.concatenate([lg_all, lgz.reshape(bsz * ctx_len, n_exp)], axis=0)

        tok_buf, g_buf, pos, blk_expert, blk_first, n_used = _route(lg_all, n_exp, bm)
        xs = h_all[tok_buf]
        out = _expert_ffn(xs, g_buf[:, None], blk_expert, blk_first, n_used,
                          expert_wgu[i], expert_bgu[i], expert_wdown[i], expert_bdown[i], bm=bm)
        y4 = out[pos.T]
        x = _combine(y4, 0, x, mod_x, ln2_g[i], ln2_b[i], alpha=alpha, tm=tm_x)
        if need_ctx:
            z = _combine(y4, bsz * seq, z, mod_z, ln2_g[i], ln2_b[i], alpha=alpha, tm=tm_z)
    return x
```

```python
import functools

import jax
import jax.numpy as jnp
from jax import lax
from jax.experimental import pallas as pl
from jax.experimental.pallas import tpu as pltpu

HEAD_DIM = 64
GROUP = 4
GRID_W = 64
WINDOW = 128
ROPE_THETA = 10000.0
TOP_K = 4
N_MIXERS = 3
SWIGLU_ALPHA = 1.702
SWIGLU_LIMIT = 7.0
LN_EPS = 1e-5
RMS_EPS = 1e-6
NEG_INF = -1e30

LANES = 128
MOD_ROWS = 8
VMEM_LIMIT = 56 * 1024 * 1024

F32 = jnp.float32
BF16 = jnp.bfloat16


def _cparams(sem):
    return pltpu.CompilerParams(dimension_semantics=sem, vmem_limit_bytes=VMEM_LIMIT)


def _mod_kernel(c_ref, w_ref, b_ref, o_ref):
    c = c_ref[...]
    s = c * jax.nn.sigmoid(c)
    o_ref[0] = jnp.dot(s, w_ref[0], preferred_element_type=F32, precision=lax.Precision.HIGHEST) + b_ref[0]


def _modulation(c_rows, mod_w, mod_b):
    depth, d, n = mod_w.shape
    r = c_rows.shape[0]
    tn = min(n, 1536)
    return pl.pallas_call(
        _mod_kernel,
        grid=(depth, n // tn),
        in_specs=[pl.BlockSpec((r, d), lambda l, j: (0, 0)),
                  pl.BlockSpec((1, d, tn), lambda l, j: (l, 0, j)),
                  pl.BlockSpec((1, 1, tn), lambda l, j: (l, 0, j))],
        out_specs=pl.BlockSpec((1, r, tn), lambda l, j: (l, 0, j)),
        out_shape=jax.ShapeDtypeStruct((depth, r, n), F32),
        compiler_params=_cparams(("arbitrary", "arbitrary")),
        name="modulation",
    )(c_rows, mod_w, mod_b.reshape(depth, 1, n))


def _swap_pairs(t):
    lane = lax.broadcasted_iota(jnp.int32, t.shape, 1)
    return jnp.where(lane % 32 < 16, pltpu.roll(t, LANES - 16, 1), pltpu.roll(t, 16, 1))


def _head_mean_sq(t, seg_ref):
    t2 = t * t
    hi = t2.astype(BF16)
    lo = (t2 - hi.astype(F32)).astype(BF16)
    seg = seg_ref[...]
    s = jnp.dot(hi, seg, preferred_element_type=F32) + jnp.dot(lo, seg, preferred_element_type=F32)
    return s * (1.0 / HEAD_DIM)


def _qkv_kernel(*refs, nq, nk, rope, qk_norm):
    x_ref, mod_ref, w_ref, b_ref = refs[:4]
    pos = 4
    if rope:
        cos_ref, sin_ref = refs[pos:pos + 2]
        pos += 2
    if qk_norm:
        seg_ref, qg_ref, kg_ref = refs[pos:pos + 3]
        pos += 3
    q_ref, k_ref, v_ref = refs[pos:pos + 3]
    m = mod_ref[0]
    h = (x_ref[0] * (1.0 + m[1:2]) + m[0:1]).astype(BF16)
    if rope:
        cos, sin = cos_ref[...], sin_ref[...]
    for c in range((nq + 2 * nk) // LANES):
        lo = c * LANES
        t = jnp.dot(h, w_ref[:, lo:lo + LANES], preferred_element_type=F32) + b_ref[:, lo:lo + LANES]
        if lo < nq + nk:
            is_q = lo < nq
            if qk_norm:
                gain = qg_ref[...] if is_q else kg_ref[...]
                t = t * lax.rsqrt(_head_mean_sq(t, seg_ref) + RMS_EPS) * gain
            if rope:
                t = t * cos + _swap_pairs(t) * sin
            if is_q:
                q_ref[0, :, lo:lo + LANES] = (t * HEAD_DIM ** -0.5).astype(BF16)
            else:
                k_ref[0, :, lo - nq:lo - nq + LANES] = t.astype(BF16)
        else:
            v_ref[0, :, lo - nq - nk:lo - nq - nk + LANES] = t.astype(BF16)


def _dup_heads(w, n_heads):
    lead = w.shape[:-1]
    w = w.reshape(lead + (n_heads, 1, HEAD_DIM))
    return jnp.broadcast_to(w, lead + (n_heads, 2, HEAD_DIM)).reshape(lead + (n_heads * LANES,))


def _pad_heads(w, n_heads, fill):
    lead = w.shape[:-1]
    w = w.reshape(lead + (n_heads, HEAD_DIM))
    pad = jnp.full(lead + (n_heads, HEAD_DIM), fill, w.dtype)
    return jnp.concatenate([w, pad], axis=-1).reshape(lead + (n_heads * LANES,))


def _qkv_project(x, mod, w, b, *, tables=None, norms=None, tm):
    bsz, seq, d = x.shape
    n_kv = (w.shape[1] - d) // (2 * HEAD_DIM)
    nq, nk = d, n_kv * LANES
    wq, wk, wv = w[:, :d], w[:, d:d + n_kv * HEAD_DIM], w[:, d + n_kv * HEAD_DIM:]
    w_ext = jnp.concatenate([wq, _dup_heads(wk, n_kv), _pad_heads(wv, n_kv, 0.0)], axis=1).astype(BF16)
    bq, bk, bv = b[:d], b[d:d + n_kv * HEAD_DIM], b[d + n_kv * HEAD_DIM:]
    b_ext = jnp.concatenate([bq, _dup_heads(bk, n_kv), _pad_heads(bv, n_kv, 1.0)])[None, :].astype(F32)
    n = nq + 2 * nk
    args = [x, mod, w_ext, b_ext]
    specs = [pl.BlockSpec((1, tm, d), lambda bi, i: (bi, i, 0)),
             pl.BlockSpec((1, MOD_ROWS, d), lambda bi, i: (bi, 0, 0)),
             pl.BlockSpec((d, n), lambda bi, i: (0, 0)),
             pl.BlockSpec((1, n), lambda bi, i: (0, 0))]
    if tables is not None:
        args += list(tables)
        specs += [pl.BlockSpec((tm, LANES), lambda bi, i: (i, 0))] * 2
    if norms is not None:
        qn, kn = norms
        seg = (jnp.arange(LANES)[:, None] // HEAD_DIM == jnp.arange(LANES)[None, :] // HEAD_DIM).astype(BF16)
        args += [seg, jnp.tile(qn, 2)[None, :].astype(F32), jnp.tile(kn, 2)[None, :].astype(F32)]
        specs += [pl.BlockSpec((LANES, LANES), lambda bi, i: (0, 0)),
                  pl.BlockSpec((1, LANES), lambda bi, i: (0, 0)),
                  pl.BlockSpec((1, LANES), lambda bi, i: (0, 0))]
    return pl.pallas_call(
        functools.partial(_qkv_kernel, nq=nq, nk=nk, rope=tables is not None, qk_norm=norms is not None),
        grid=(bsz, seq // tm),
        in_specs=specs,
        out_specs=[pl.BlockSpec((1, tm, nq), lambda bi, i: (bi, i, 0)),
                   pl.BlockSpec((1, tm, nk), lambda bi, i: (bi, i, 0)),
                   pl.BlockSpec((1, tm, nk), lambda bi, i: (bi, i, 0))],
        out_shape=[jax.ShapeDtypeStruct((bsz, seq, nq), BF16),
                   jax.ShapeDtypeStruct((bsz, seq, nk), BF16),
                   jax.ShapeDtypeStruct((bsz, seq, nk), BF16)],
        compiler_params=_cparams(("parallel", "parallel")),
        name="qkv_project",
    )(*args)


def _conv_in_kernel(x_ref, mod_ref, w_ref, bg_ref, u_ref, *, d):
    m = mod_ref[0]
    h = (x_ref[0] * (1.0 + m[1:2]) + m[0:1]).astype(BF16)
    for c in range(d // LANES):
        lo = c * LANES
        bg = jnp.dot(h, w_ref[:, lo:lo + LANES], preferred_element_type=F32)
        cg = jnp.dot(h, w_ref[:, d + lo:d + lo + LANES], preferred_element_type=F32)
        xv = jnp.dot(h, w_ref[:, 2 * d + lo:2 * d + lo + LANES], preferred_element_type=F32)
        bg_ref[0, :, lo:lo + LANES] = bg.astype(BF16)
        u_ref[0, :, lo:lo + LANES] = (cg * xv).astype(BF16)


def _conv_in_project(x, mod, w_in, *, tm):
    bsz, seq, d = x.shape
    return pl.pallas_call(
        functools.partial(_conv_in_kernel, d=d),
        grid=(bsz, seq // tm),
        in_specs=[pl.BlockSpec((1, tm, d), lambda bi, i: (bi, i, 0)),
                  pl.BlockSpec((1, MOD_ROWS, d), lambda bi, i: (bi, 0, 0)),
                  pl.BlockSpec((d, 3 * d), lambda bi, i: (0, 0))],
        out_specs=[pl.BlockSpec((1, tm, d), lambda bi, i: (bi, i, 0))] * 2,
        out_shape=[jax.ShapeDtypeStruct((bsz, seq, d), BF16)] * 2,
        compiler_params=_cparams(("parallel", "parallel")),
        name="conv_in_project",
    )(x, mod, w_in.astype(BF16))


def _group_heads(q_ref, lane):
    for pair in range(GROUP // 2):
        qp = q_ref[0, :, pair * LANES:(pair + 1) * LANES]
        yield jnp.where(lane < HEAD_DIM, qp, jnp.zeros_like(qp))
        yield jnp.where(lane >= HEAD_DIM, qp, jnp.zeros_like(qp))


def _qk(q, k):
    return lax.dot_general(q, k, (((1,), (1,)), ((), ())), preferred_element_type=F32)


def _softmax_pv(qg, chunks, sink):
    def scores(kc, bias):
        s = _qk(qg, kc)
        return s if bias is None else s + bias

    mx = None
    for kc, _, bias in chunks:
        s = scores(kc, bias)
        for j in range(s.shape[1] // LANES):
            part = s[:, j * LANES:(j + 1) * LANES]
            mx = part if mx is None else jnp.maximum(mx, part)
    m = jnp.max(mx, axis=1, keepdims=True)
    if sink is not None:
        m = jnp.maximum(m, sink)
    acc = None
    for kc, vc, bias in chunks:
        p = jnp.exp(scores(kc, bias) - m).astype(BF16)
        d = jnp.dot(p, vc, preferred_element_type=F32)
        acc = d if acc is None else acc + d
    return acc, (None if sink is None else jnp.exp(sink - m))


def _store_group(o_ref, accs, extras, lane):
    for pair in range(GROUP // 2):
        a0, a1 = accs[2 * pair], accs[2 * pair + 1]
        r0, r1 = pltpu.roll(a0, HEAD_DIM, 1), pltpu.roll(a1, HEAD_DIM, 1)
        d0 = r0 if extras[2 * pair] is None else r0 + extras[2 * pair]
        d1 = a1 if extras[2 * pair + 1] is None else a1 + extras[2 * pair + 1]
        o = jnp.where(lane < HEAD_DIM, a0 / d0, r1 / d1)
        o_ref[0, :, pair * LANES:(pair + 1) * LANES] = o.astype(o_ref.dtype)


def _win_attn_kernel(q_ref, kp_ref, kc_ref, kn_ref, kz_ref, vp_ref, vc_ref, vn_ref, vz_ref, sink_ref, o_ref,
                     *, tq, seq):
    hk = pl.program_id(1)
    i = pl.program_id(2)
    r = lax.broadcasted_iota(jnp.int32, (tq, WINDOW), 0)
    c = lax.broadcasted_iota(jnp.int32, (tq, WINDOW), 1)
    bias_prev = jnp.where((c >= r) & (i > 0), 0.0, NEG_INF)
    bias_next = jnp.where((r - c >= tq - WINDOW) & (i < seq // tq - 1), 0.0, NEG_INF)
    bias_cur = None
    if tq - 1 > WINDOW:
        rr = lax.broadcasted_iota(jnp.int32, (tq, tq), 0)
        cc = lax.broadcasted_iota(jnp.int32, (tq, tq), 1)
        bias_cur = jnp.where(jnp.abs(cc - rr) <= WINDOW, 0.0, NEG_INF)
    chunks = [(kp_ref[0], vp_ref[0], bias_prev), (kc_ref[0], vc_ref[0], bias_cur),
              (kn_ref[0], vn_ref[0], bias_next), (kz_ref[0], vz_ref[0], None)]
    lane = lax.broadcasted_iota(jnp.int32, (tq, LANES), 1)
    accs, extras = [], []
    for g, qg in enumerate(_group_heads(q_ref, lane)):
        acc, extra = _softmax_pv(qg, chunks, sink_ref[hk, g])
        accs.append(acc)
        extras.append(extra)
    _store_group(o_ref, accs, extras, lane)


def _window_attention(q, k, v, kz, vz, sink, *, tq):
    bsz, seq, d = q.shape
    n_kv = k.shape[2] // LANES
    ctx = kz.shape[1]
    r = tq // WINDOW
    last = seq // WINDOW - 1
    cur = lambda bi, h, i: (bi, i, h)
    prev = lambda bi, h, i: (bi, jnp.maximum(i * r - 1, 0), h)
    nxt = lambda bi, h, i: (bi, jnp.minimum((i + 1) * r, last), h)
    zmap = lambda bi, h, i: (bi, 0, h)
    kv_specs = [pl.BlockSpec((1, WINDOW, LANES), prev), pl.BlockSpec((1, tq, LANES), cur),
                pl.BlockSpec((1, WINDOW, LANES), nxt), pl.BlockSpec((1, ctx, LANES), zmap)]
    return pl.pallas_call(
        functools.partial(_win_attn_kernel, tq=tq, seq=seq),
        grid=(bsz, n_kv, seq // tq),
        in_specs=[pl.BlockSpec((1, tq, GROUP * HEAD_DIM), cur)] + kv_specs + kv_specs
                 + [pl.BlockSpec(memory_space=pltpu.SMEM)],
        out_specs=pl.BlockSpec((1, tq, GROUP * HEAD_DIM), cur),
        out_shape=jax.ShapeDtypeStruct((bsz, seq, d), BF16),
        compiler_params=_cparams(("parallel", "parallel", "parallel")),
        name="window_attention",
    )(q, k, k, k, kz, v, v, v, vz, sink.reshape(n_kv, GROUP).astype(F32))


def _dense_attn_kernel(*refs, ck, has_sink):
    if has_sink:
        q_ref, k_ref, v_ref, sink_ref, o_ref = refs
    else:
        q_ref, k_ref, v_ref, o_ref = refs
    hk = pl.program_id(1)
    tq = q_ref.shape[1]
    chunks = [(k_ref[0, lo:lo + ck, :], v_ref[0, lo:lo + ck, :], None) for lo in range(0, k_ref.shape[1], ck)]
    lane = lax.broadcasted_iota(jnp.int32, (tq, LANES), 1)
    accs, extras = [], []
    for g, qg in enumerate(_group_heads(q_ref, lane)):
        acc, extra = _softmax_pv(qg, chunks, sink_ref[hk, g] if has_sink else None)
        accs.append(acc)
        extras.append(extra)
    _store_group(o_ref, accs, extras, lane)


def _dense_attention(q, k, v, sink=None, *, tq):
    bsz, seq, d = q.shape
    n_keys = k.shape[1]
    n_kv = k.shape[2] // LANES
    ck = 256 if n_keys % 256 == 0 else LANES
    qmap = lambda bi, h, i: (bi, i, h)
    kmap = lambda bi, h, i: (bi, 0, h)
    args = [q, k, v]
    specs = [pl.BlockSpec((1, tq, GROUP * HEAD_DIM), qmap),
             pl.BlockSpec((1, n_keys, LANES), kmap), pl.BlockSpec((1, n_keys, LANES), kmap)]
    if sink is not None:
        args.append(sink.reshape(n_kv, GROUP).astype(F32))
        specs.append(pl.BlockSpec(memory_space=pltpu.SMEM))
    return pl.pallas_call(
        functools.partial(_dense_attn_kernel, ck=ck, has_sink=sink is not None),
        grid=(bsz, n_kv, seq // tq),
        in_specs=specs,
        out_specs=pl.BlockSpec((1, tq, GROUP * HEAD_DIM), qmap),
        out_shape=jax.ShapeDtypeStruct((bsz, seq, d), BF16),
        compiler_params=_cparams(("parallel", "parallel", "parallel")),
        name="dense_attention",
    )(*args)


def _layer_norm(r, g, b):
    mu = jnp.mean(r, axis=-1, keepdims=True)
    rc = r - mu
    var = jnp.mean(rc * rc, axis=-1, keepdims=True)
    return rc * lax.rsqrt(var + LN_EPS) * g + b


def _top_k_route(logits, gate_ref, idx_ref, rank_ref, cnt_ref):
    tm, n_exp = logits.shape
    lane_e = lax.broadcasted_iota(jnp.int32, (tm, n_exp), 1).astype(F32)
    work = logits
    sels, vals, idxs = [], [], []
    for _ in range(TOP_K):
        mk = jnp.max(work, axis=1, keepdims=True)
        ik = jnp.min(jnp.where(work == mk, lane_e, float(n_exp)), axis=1, keepdims=True)
        sel = lane_e == ik
        work = jnp.where(sel, -jnp.inf, work)
        sels.append(sel)
        vals.append(mk)
        idxs.append(ik)
    exps = [jnp.exp(v - vals[0]) for v in vals]
    denom = exps[0]
    for e in exps[1:]:
        denom = denom + e
    routed = sels[0].astype(F32)
    for sel in sels[1:]:
        routed = routed + sel.astype(F32)
    row = lax.broadcasted_iota(jnp.int32, (tm, tm), 0)
    col = lax.broadcasted_iota(jnp.int32, (tm, tm), 1)
    earlier = jnp.where(col < row, 1.0, 0.0).astype(BF16)
    before = jnp.dot(earlier, routed.astype(BF16), preferred_element_type=F32)
    lane_k = lax.broadcasted_iota(jnp.int32, (tm, TOP_K), 1)
    gates = jnp.zeros((tm, TOP_K), F32)
    idx = jnp.zeros((tm, TOP_K), F32)
    rank = jnp.zeros((tm, TOP_K), F32)
    for kk in range(TOP_K):
        here = lane_k == kk
        gates = jnp.where(here, exps[kk] / denom, gates)
        idx = jnp.where(here, idxs[kk], idx)
        rank = jnp.where(here, jnp.sum(jnp.where(sels[kk], before, 0.0), axis=1, keepdims=True), rank)
    gate_ref[0] = gates
    idx_ref[0] = idx.astype(jnp.int32)
    rank_ref[0] = rank.astype(jnp.int32)
    cnt_ref[0] = jnp.sum(routed, axis=0, keepdims=True).astype(jnp.int32)


def _mixer_out_kernel(*refs, conv, alpha, seq):
    if conv:
        bg_ref, u_ref, up_ref, un_ref, cw_ref = refs[:5]
        refs = refs[5:]
    else:
        o_ref = refs[0]
        refs = refs[1:]
    (w_ref, x_ref, mod_ref, lng_ref, lnb_ref, rwh_ref, rwl_ref, rb_ref,
     xo_ref, h_ref, gate_ref, idx_ref, rank_ref, cnt_ref) = refs
    if conv:
        i = pl.program_id(1)
        u = u_ref[0].astype(F32)
        tm = u.shape[0]
        row = lax.broadcasted_iota(jnp.int32, u.shape, 0)
        halo = up_ref.shape[1]
        before = jnp.where(i == 0, 0.0, up_ref[0, halo - 1:halo, :].astype(F32))
        after = jnp.where(i == seq // tm - 1, 0.0, un_ref[0, 0:1, :].astype(F32))
        u_prev = jnp.where(row == 0, before, pltpu.roll(u, 1, 0))
        u_next = jnp.where(row == tm - 1, after, pltpu.roll(u, tm - 1, 0))
        cw = cw_ref[...]
        y = cw[0:1] * u_prev + cw[1:2] * u + cw[2:3] * u_next
        mixed = (bg_ref[0].astype(F32) * y).astype(BF16)
    else:
        mixed = o_ref[0]
    m = mod_ref[0]
    ox = jnp.dot(mixed, w_ref[...], preferred_element_type=F32)
    xn = _layer_norm(alpha * x_ref[0] + m[2:3] * ox, lng_ref[...], lnb_ref[...])
    xo_ref[0] = xn
    h2 = xn * (1.0 + m[4:5]) + m[3:4]
    hh = h2.astype(BF16)
    hl = (h2 - hh.astype(F32)).astype(BF16)
    h_ref[0] = hh
    rwh = rwh_ref[...]
    logits = (jnp.dot(hh, rwh, preferred_element_type=F32) + jnp.dot(hl, rwh, preferred_element_type=F32)
              + jnp.dot(hh, rwl_ref[...], preferred_element_type=F32) + rb_ref[...])
    _top_k_route(logits, gate_ref, idx_ref, rank_ref, cnt_ref)


def _mixer_out(mixed, w_out, x, mod, ln_g, ln_b, router_w, router_b, *, alpha, tm, conv_w=None):
    bsz, seq, d = x.shape
    n_exp = router_w.shape[1]
    nt = seq // tm
    tile = pl.BlockSpec((1, tm, d), lambda bi, i: (bi, i, 0))
    ktile = pl.BlockSpec((1, tm, TOP_K), lambda bi, i: (bi, i, 0))
    row = pl.BlockSpec((1, d), lambda bi, i: (0, 0))
    conv = conv_w is not None
    if conv:
        bg, u = mixed
        halo = 16
        r = tm // halo
        last = seq // halo - 1
        args = [bg, u, u, u, jnp.pad(conv_w.astype(F32), ((0, MOD_ROWS - conv_w.shape[0]), (0, 0)))]
        specs = [tile, tile,
                 pl.BlockSpec((1, halo, d), lambda bi, i: (bi, jnp.maximum(i * r - 1, 0), 0)),
                 pl.BlockSpec((1, halo, d), lambda bi, i: (bi, jnp.minimum((i + 1) * r, last), 0)),
                 pl.BlockSpec((MOD_ROWS, d), lambda bi, i: (0, 0))]
    else:
        args, specs = [mixed], [tile]
    rw_hi = router_w.astype(BF16)
    rw_lo = (router_w - rw_hi.astype(F32)).astype(BF16)
    args += [w_out.astype(BF16), x, mod, ln_g[None, :], ln_b[None, :], rw_hi, rw_lo, router_b[None, :]]
    specs += [pl.BlockSpec((d, d), lambda bi, i: (0, 0)), tile,
              pl.BlockSpec((1, MOD_ROWS, d), lambda bi, i: (bi, 0, 0)), row, row,
              pl.BlockSpec((d, n_exp), lambda bi, i: (0, 0)), pl.BlockSpec((d, n_exp), lambda bi, i: (0, 0)),
              pl.BlockSpec((1, n_exp), lambda bi, i: (0, 0))]
    return pl.pallas_call(
        functools.partial(_mixer_out_kernel, conv=conv, alpha=alpha, seq=seq),
        grid=(bsz, nt),
        in_specs=specs,
        out_specs=[tile, tile, ktile, ktile, ktile,
                   pl.BlockSpec((1, 1, n_exp), lambda bi, i: (bi * nt + i, 0, 0))],
        out_shape=[jax.ShapeDtypeStruct((bsz, seq, d), F32), jax.ShapeDtypeStruct((bsz, seq, d), BF16),
                   jax.ShapeDtypeStruct((bsz, seq, TOP_K), F32), jax.ShapeDtypeStruct((bsz, seq, TOP_K), jnp.int32),
                   jax.ShapeDtypeStruct((bsz, seq, TOP_K), jnp.int32),
                   jax.ShapeDtypeStruct((bsz * nt, 1, n_exp), jnp.int32)],
        compiler_params=_cparams(("parallel", "parallel")),
        name="mixer_out",
    )(*args)


def _expert_kernel(be_ref, first_ref, used_ref, x_ref, wgu_ref, bgu_ref, wdn_ref, bdn_ref, o_ref,
                   wgu_bf, wdn_bf, *, ff, fc):
    i = pl.program_id(0)

    @pl.when(first_ref[i] == 1)
    def _():
        wgu_bf[...] = wgu_ref[0, 0].astype(BF16)
        wdn_bf[...] = wdn_ref[0, 0].astype(BF16)

    @pl.when(i < used_ref[0])
    def _():
        xb = x_ref[...]
        acc = jnp.zeros(o_ref.shape, F32)
        for c in range(ff // fc):
            lo = c * fc
            gate = jnp.dot(xb, wgu_bf[:, lo:lo + fc], preferred_element_type=F32) + bgu_ref[0, 0, :, lo:lo + fc]
            up = jnp.dot(xb, wgu_bf[:, ff + lo:ff + lo + fc], preferred_element_type=F32) \
                + bgu_ref[0, 0, :, ff + lo:ff + lo + fc]
            gate = jnp.minimum(gate, SWIGLU_LIMIT)
            up = jnp.clip(up, -SWIGLU_LIMIT, SWIGLU_LIMIT)
            act = (up + 1.0) * gate * jax.nn.sigmoid(SWIGLU_ALPHA * gate)
            acc = acc + jnp.dot(act.astype(BF16), wdn_bf[lo:lo + fc, :], preferred_element_type=F32)
        o_ref[...] = acc + bdn_ref[0, 0]

    @pl.when(i >= used_ref[0])
    def _():
        o_ref[...] = jnp.zeros_like(o_ref)


def _expert_ffn(xs, blk_expert, blk_first, n_used, layer, w_gu, b_gu, w_down, b_down, *, bm):
    p, d = xs.shape
    depth, n_exp, _, ff2 = w_gu.shape
    ff = ff2 // 2
    wmap = lambda i, be, fi, nu: (layer, be[i], 0, 0)
    grid_spec = pltpu.PrefetchScalarGridSpec(
        num_scalar_prefetch=3,
        grid=(p // bm,),
        in_specs=[pl.BlockSpec((bm, d), lambda i, be, fi, nu: (i, 0)),
                  pl.BlockSpec((1, 1, d, ff2), wmap), pl.BlockSpec((1, 1, 1, ff2), wmap),
                  pl.BlockSpec((1, 1, ff, d), wmap), pl.BlockSpec((1, 1, 1, d), wmap)],
        out_specs=pl.BlockSpec((bm, d), lambda i, be, fi, nu: (i, 0)),
        scratch_shapes=[pltpu.VMEM((d, ff2), BF16), pltpu.VMEM((ff, d), BF16)],
    )
    return pl.pallas_call(
        functools.partial(_expert_kernel, ff=ff, fc=min(ff, 512)),
        grid_spec=grid_spec,
        out_shape=jax.ShapeDtypeStruct((p, d), F32),
        compiler_params=_cparams(("arbitrary",)),
        name="expert_ffn",
    )(blk_expert, blk_first, n_used, xs, w_gu, b_gu.reshape(depth, n_exp, 1, ff2), w_down,
      b_down.reshape(depth, n_exp, 1, d))


def _route(idx, rank, cnt_tiles, tile_rows, bm):
    t = idx.shape[0]
    a = t * TOP_K
    n_exp = cnt_tiles.shape[1]
    counts = jnp.sum(cnt_tiles, axis=0)
    padded = (counts + bm - 1) // bm * bm
    pends = jnp.cumsum(padded)
    pstarts = pends - padded
    tile_off = pstarts[None, :] + jnp.cumsum(cnt_tiles, axis=0) - cnt_tiles
    off_tok = jnp.repeat(tile_off, jnp.asarray(tile_rows), axis=0, total_repeat_length=t)
    chosen = idx[:, :, None] == jnp.arange(n_exp, dtype=jnp.int32)[None, None, :]
    dest = jnp.sum(jnp.where(chosen, off_tok[:, None, :], 0), axis=-1) + rank
    nblk = -(-a // bm) + n_exp
    tok_flat = jnp.arange(a, dtype=jnp.int32) // TOP_K
    tok_buf = jnp.zeros((nblk * bm,), jnp.int32).at[dest.reshape(a)].set(tok_flat)
    blk_start = jnp.arange(nblk, dtype=jnp.int32) * bm
    blk_expert = jnp.minimum(jnp.searchsorted(pends, blk_start, side='right'), n_exp - 1).astype(jnp.int32)
    blk_first = jnp.concatenate([jnp.ones((1,), jnp.int32),
                                 (blk_expert[1:] != blk_expert[:-1]).astype(jnp.int32)])
    n_used = (pends[-1] // bm).astype(jnp.int32).reshape(1)
    return tok_buf, dest, blk_expert, blk_first, n_used


def _combine_kernel(y_ref, gate_ref, x_ref, mod_ref, lng_ref, lnb_ref, o_ref, *, alpha):
    g = gate_ref[0]
    fx = g[:, 0:1] * y_ref[0]
    for kk in range(1, y_ref.shape[0]):
        fx = fx + g[:, kk:kk + 1] * y_ref[kk]
    m = mod_ref[0]
    o_ref[0] = _layer_norm(alpha * x_ref[0] + m[5:6] * fx, lng_ref[...], lnb_ref[...])


def _combine(y4, row0, gates, x, mod, ln_g, ln_b, *, alpha, tm):
    bsz, seq, d = x.shape
    nt = seq // tm
    off = row0 // tm
    tile = pl.BlockSpec((1, tm, d), lambda bi, i: (bi, i, 0))
    row = pl.BlockSpec((1, d), lambda bi, i: (0, 0))
    return pl.pallas_call(
        functools.partial(_combine_kernel, alpha=alpha),
        grid=(bsz, nt),
        in_specs=[pl.BlockSpec((y4.shape[0], tm, d), lambda bi, i: (0, off + bi * nt + i, 0)),
                  pl.BlockSpec((1, tm, TOP_K), lambda bi, i: (bi, i, 0)),
                  tile, pl.BlockSpec((1, MOD_ROWS, d), lambda bi, i: (bi, 0, 0)), row, row],
        out_specs=tile,
        out_shape=jax.ShapeDtypeStruct((bsz, seq, d), F32),
        compiler_params=_cparams(("parallel", "parallel")),
        name="combine_norm",
    )(y4, gates, x, mod, ln_g[None, :], ln_b[None, :])


def _rope_tables(seq):
    t = jnp.arange(seq, dtype=jnp.int32)
    row = (t // GRID_W).astype(F32)
    col = (t % GRID_W).astype(F32)
    n_freq = HEAD_DIM // 4
    inv_freq = ROPE_THETA ** (-jnp.arange(n_freq, dtype=F32) / n_freq)
    ar, ac = row[:, None] * inv_freq, col[:, None] * inv_freq
    cos = jnp.concatenate([jnp.cos(ar), jnp.cos(ar), jnp.cos(ac), jnp.cos(ac)], axis=1)
    sin = jnp.concatenate([-jnp.sin(ar), jnp.sin(ar), -jnp.sin(ac), jnp.sin(ac)], axis=1)
    return jnp.tile(cos, (1, 2)), jnp.tile(sin, (1, 2))


def _tile(n, want):
    t = min(n, want)
    while n % t:
        t //= 2
    return t


def kernel(x, c, ctx, c_ctx, mod_w, mod_b, ln1_g, ln1_b, ln2_g, ln2_b, win_wqkv, win_bqkv, win_sink, win_wo,
           conv_win, conv_w, conv_wout, full_wqkv, full_qnorm, full_knorm, full_wo,
           router_w, router_b, expert_wgu, expert_bgu, expert_wdown, expert_bdown):
    bsz, seq, d = x.shape
    ctx_len = ctx.shape[1]
    depth = mod_w.shape[0]
    alpha = (2.0 * depth) ** 0.25
    tables = _rope_tables(seq)
    tm_x, tm_z = _tile(seq, 512), _tile(ctx_len, 512)
    tq_win = _tile(seq, 256)
    tq_full = _tile(seq, 256)
    bm = 512

    c_rows = jnp.zeros((2 * MOD_ROWS, d), F32).at[:bsz].set(c).at[bsz].set(c_ctx)
    mods = _modulation(c_rows, mod_w, mod_b)
    z = ctx
    for i in range(depth):
        kind, j = i % N_MIXERS, i // N_MIXERS
        need_ctx = i < depth - 1
        mod_x = jnp.pad(mods[i, :bsz].reshape(bsz, 6, d), ((0, 0), (0, MOD_ROWS - 6), (0, 0)))
        mod_z = jnp.broadcast_to(jnp.pad(mods[i, bsz].reshape(1, 6, d), ((0, 0), (0, MOD_ROWS - 6), (0, 0))),
                                 (bsz, MOD_ROWS, d))
        route_args = (router_w[i], router_b[i])
        ln1 = (ln1_g[i], ln1_b[i])
        oz = None
        if kind == 0:
            qx, kx, vx = _qkv_project(x, mod_x, win_wqkv[j], win_bqkv[j], tables=tables, tm=tm_x)
            qz, kz, vz = _qkv_project(z, mod_z, win_wqkv[j], win_bqkv[j], tm=tm_z)
            ox = _window_attention(qx, kx, vx, kz, vz, win_sink[j], tq=tq_win)
            if need_ctx:
                oz = _dense_attention(qz, kz, vz, win_sink[j], tq=ctx_len)
            w_out, conv_taps = win_wo[j], None
        elif kind == 1:
            ox = _conv_in_project(x, mod_x, conv_win[j], tm=tm_x)
            if need_ctx:
                oz = _conv_in_project(z, mod_z, conv_win[j], tm=tm_z)
            w_out, conv_taps = conv_wout[j], conv_w[j]
        else:
            zero_b = jnp.zeros((full_wqkv.shape[2],), F32)
            norms = (full_qnorm[j], full_knorm[j])
            qx, kx, vx = _qkv_project(x, mod_x, full_wqkv[j], zero_b, tables=tables, norms=norms, tm=tm_x)
            qz, kz, vz = _qkv_project(z, mod_z, full_wqkv[j], zero_b, norms=norms, tm=tm_z)
            k_all = jnp.concatenate([kx, kz], axis=1)
            v_all = jnp.concatenate([vx, vz], axis=1)
            ox = _dense_attention(qx, k_all, v_all, tq=tq_full)
            if need_ctx:
                oz = _dense_attention(qz, kz, vz, tq=ctx_len)
            w_out, conv_taps = full_wo[j], None

        x, hx, gx, ix, rx, cx = _mixer_out(ox, w_out, x, mod_x, *ln1, *route_args, alpha=alpha, tm=tm_x,
                                           conv_w=conv_taps)
        n_x = bsz * seq
        h_all, idx, rank, cnt = hx.reshape(n_x, d), ix.reshape(n_x, TOP_K), rx.reshape(n_x, TOP_K), cx[:, 0]
        tile_rows = [tm_x] * (n_x // tm_x)
        if need_ctx:
            z, hz, gz, iz, rz, cz = _mixer_out(oz, w_out, z, mod_z, *ln1, *route_args, alpha=alpha, tm=tm_z,
                                               conv_w=conv_taps)
            n_z = bsz * ctx_len
            h_all = jnp.concatenate([h_all, hz.reshape(n_z, d)], axis=0)
            idx = jnp.concatenate([idx, iz.reshape(n_z, TOP_K)], axis=0)
            rank = jnp.concatenate([rank, rz.reshape(n_z, TOP_K)], axis=0)
            cnt = jnp.concatenate([cnt, cz[:, 0]], axis=0)
            tile_rows += [tm_z] * (n_z // tm_z)

        tok_buf, dest, blk_expert, blk_first, n_used = _route(idx, rank, cnt, tile_rows, bm)
        xs = h_all[tok_buf]
        out = _expert_ffn(xs, blk_expert, blk_first, n_used, i,
                          expert_wgu, expert_bgu, expert_wdown, expert_bdown, bm=bm)
        y4 = out[dest.T]
        x = _combine(y4, 0, gx, x, mod_x, ln2_g[i], ln2_b[i], alpha=alpha, tm=tm_x)
        if need_ctx:
            z = _combine(y4, n_x, gz, z, mod_z, ln2_g[i], ln2_b[i], alpha=alpha, tm=tm_z)
    return x
```

```python
import functools

import jax
import jax.numpy as jnp
from jax import lax
from jax.experimental import pallas as pl
from jax.experimental.pallas import tpu as pltpu

HEAD_DIM = 64
GROUP = 4
GRID_W = 64
WINDOW = 128
ROPE_THETA = 10000.0
TOP_K = 4
N_MIXERS = 3
SWIGLU_ALPHA = 1.702
SWIGLU_LIMIT = 7.0
LN_EPS = 1e-5
RMS_EPS = 1e-6
NEG_INF = -1e30

LANES = 128
SUBLANES = 8
MOD_ROWS = 8
VMEM_LIMIT = 56 * 1024 * 1024

F32 = jnp.float32
BF16 = jnp.bfloat16


def _cparams(sem):
    return pltpu.CompilerParams(dimension_semantics=sem, vmem_limit_bytes=VMEM_LIMIT)


def _mod_kernel(c_ref, w_ref, b_ref, o_ref):
    c = c_ref[...]
    s = c * jax.nn.sigmoid(c)
    o_ref[0] = jnp.dot(s, w_ref[0], preferred_element_type=F32, precision=lax.Precision.HIGHEST) + b_ref[0]


def _modulation(c_rows, mod_w, mod_b):
    depth, d, n = mod_w.shape
    r = c_rows.shape[0]
    tn = min(n, 1536)
    return pl.pallas_call(
        _mod_kernel,
        grid=(depth, n // tn),
        in_specs=[pl.BlockSpec((r, d), lambda l, j: (0, 0)),
                  pl.BlockSpec((1, d, tn), lambda l, j: (l, 0, j)),
                  pl.BlockSpec((1, 1, tn), lambda l, j: (l, 0, j))],
        out_specs=pl.BlockSpec((1, r, tn), lambda l, j: (l, 0, j)),
        out_shape=jax.ShapeDtypeStruct((depth, r, n), F32),
        compiler_params=_cparams(("arbitrary", "arbitrary")),
        name="modulation",
    )(c_rows, mod_w, mod_b.reshape(depth, 1, n))


def _swap_pairs(t):
    lane = lax.broadcasted_iota(jnp.int32, t.shape, 1)
    return jnp.where(lane % 32 < 16, pltpu.roll(t, LANES - 16, 1), pltpu.roll(t, 16, 1))


def _head_mean_sq(t, seg_ref):
    t2 = t * t
    hi = t2.astype(BF16)
    lo = (t2 - hi.astype(F32)).astype(BF16)
    seg = seg_ref[...]
    s = jnp.dot(hi, seg, preferred_element_type=F32) + jnp.dot(lo, seg, preferred_element_type=F32)
    return s * (1.0 / HEAD_DIM)


def _qkv_kernel(*refs, nq, nk, rope, qk_norm):
    x_ref, mod_ref, w_ref, b_ref = refs[:4]
    pos = 4
    if rope:
        cos_ref, sin_ref = refs[pos:pos + 2]
        pos += 2
    if qk_norm:
        seg_ref, qg_ref, kg_ref = refs[pos:pos + 3]
        pos += 3
    q_ref, k_ref, v_ref = refs[pos:pos + 3]
    m = mod_ref[0]
    h = (x_ref[0] * (1.0 + m[1:2]) + m[0:1]).astype(BF16)
    if rope:
        cos, sin = cos_ref[...], sin_ref[...]
    for c in range((nq + 2 * nk) // LANES):
        lo = c * LANES
        t = jnp.dot(h, w_ref[:, lo:lo + LANES], preferred_element_type=F32) + b_ref[:, lo:lo + LANES]
        if lo < nq + nk:
            is_q = lo < nq
            if qk_norm:
                gain = qg_ref[...] if is_q else kg_ref[...]
                t = t * lax.rsqrt(_head_mean_sq(t, seg_ref) + RMS_EPS) * gain
            if rope:
                t = t * cos + _swap_pairs(t) * sin
            if is_q:
                q_ref[0, :, lo:lo + LANES] = (t * HEAD_DIM ** -0.5).astype(BF16)
            else:
                k_ref[0, :, lo - nq:lo - nq + LANES] = t.astype(BF16)
        else:
            v_ref[0, :, lo - nq - nk:lo - nq - nk + LANES] = t.astype(BF16)


def _dup_heads(w, n_heads):
    lead = w.shape[:-1]
    w = w.reshape(lead + (n_heads, 1, HEAD_DIM))
    return jnp.broadcast_to(w, lead + (n_heads, 2, HEAD_DIM)).reshape(lead + (n_heads * LANES,))


def _pad_heads(w, n_heads, fill):
    lead = w.shape[:-1]
    w = w.reshape(lead + (n_heads, HEAD_DIM))
    pad = jnp.full(lead + (n_heads, HEAD_DIM), fill, w.dtype)
    return jnp.concatenate([w, pad], axis=-1).reshape(lead + (n_heads * LANES,))


def _qkv_project(x, mod, w, b, *, tables=None, norms=None, tm):
    bsz, seq, d = x.shape
    n_kv = (w.shape[1] - d) // (2 * HEAD_DIM)
    nq, nk = d, n_kv * LANES
    wq, wk, wv = w[:, :d], w[:, d:d + n_kv * HEAD_DIM], w[:, d + n_kv * HEAD_DIM:]
    w_ext = jnp.concatenate([wq, _dup_heads(wk, n_kv), _pad_heads(wv, n_kv, 0.0)], axis=1).astype(BF16)
    bq, bk, bv = b[:d], b[d:d + n_kv * HEAD_DIM], b[d + n_kv * HEAD_DIM:]
    b_ext = jnp.concatenate([bq, _dup_heads(bk, n_kv), _pad_heads(bv, n_kv, 1.0)])[None, :].astype(F32)
    n = nq + 2 * nk
    args = [x, mod, w_ext, b_ext]
    specs = [pl.BlockSpec((1, tm, d), lambda bi, i: (bi, i, 0)),
             pl.BlockSpec((1, MOD_ROWS, d), lambda bi, i: (bi, 0, 0)),
             pl.BlockSpec((d, n), lambda bi, i: (0, 0)),
             pl.BlockSpec((1, n), lambda bi, i: (0, 0))]
    if tables is not None:
        args += list(tables)
        specs += [pl.BlockSpec((tm, LANES), lambda bi, i: (i, 0))] * 2
    if norms is not None:
        qn, kn = norms
        seg = (jnp.arange(LANES)[:, None] // HEAD_DIM == jnp.arange(LANES)[None, :] // HEAD_DIM).astype(BF16)
        args += [seg, jnp.tile(qn, 2)[None, :].astype(F32), jnp.tile(kn, 2)[None, :].astype(F32)]
        specs += [pl.BlockSpec((LANES, LANES), lambda bi, i: (0, 0)),
                  pl.BlockSpec((1, LANES), lambda bi, i: (0, 0)),
                  pl.BlockSpec((1, LANES), lambda bi, i: (0, 0))]
    return pl.pallas_call(
        functools.partial(_qkv_kernel, nq=nq, nk=nk, rope=tables is not None, qk_norm=norms is not None),
        grid=(bsz, seq // tm),
        in_specs=specs,
        out_specs=[pl.BlockSpec((1, tm, nq), lambda bi, i: (bi, i, 0)),
                   pl.BlockSpec((1, tm, nk), lambda bi, i: (bi, i, 0)),
                   pl.BlockSpec((1, tm, nk), lambda bi, i: (bi, i, 0))],
        out_shape=[jax.ShapeDtypeStruct((bsz, seq, nq), BF16),
                   jax.ShapeDtypeStruct((bsz, seq, nk), BF16),
                   jax.ShapeDtypeStruct((bsz, seq, nk), BF16)],
        compiler_params=_cparams(("parallel", "parallel")),
        name="qkv_project",
    )(*args)


def _conv_in_kernel(x_ref, mod_ref, w_ref, bg_ref, u_ref, *, d):
    m = mod_ref[0]
    h = (x_ref[0] * (1.0 + m[1:2]) + m[0:1]).astype(BF16)
    for c in range(d // LANES):
        lo = c * LANES
        bg = jnp.dot(h, w_ref[:, lo:lo + LANES], preferred_element_type=F32)
        cg = jnp.dot(h, w_ref[:, d + lo:d + lo + LANES], preferred_element_type=F32)
        xv = jnp.dot(h, w_ref[:, 2 * d + lo:2 * d + lo + LANES], preferred_element_type=F32)
        bg_ref[0, :, lo:lo + LANES] = bg.astype(BF16)
        u_ref[0, :, lo:lo + LANES] = (cg * xv).astype(BF16)


def _conv_in_project(x, mod, w_in, *, tm):
    bsz, seq, d = x.shape
    return pl.pallas_call(
        functools.partial(_conv_in_kernel, d=d),
        grid=(bsz, seq // tm),
        in_specs=[pl.BlockSpec((1, tm, d), lambda bi, i: (bi, i, 0)),
                  pl.BlockSpec((1, MOD_ROWS, d), lambda bi, i: (bi, 0, 0)),
                  pl.BlockSpec((d, 3 * d), lambda bi, i: (0, 0))],
        out_specs=[pl.BlockSpec((1, tm, d), lambda bi, i: (bi, i, 0))] * 2,
        out_shape=[jax.ShapeDtypeStruct((bsz, seq, d), BF16)] * 2,
        compiler_params=_cparams(("parallel", "parallel")),
        name="conv_in_project",
    )(x, mod, w_in.astype(BF16))


def _qk(q, k):
    return lax.dot_general(q, k, (((1,), (1,)), ((), ())), preferred_element_type=F32)


def _scores(qg, kc, bias):
    s = _qk(qg, kc)
    return s if bias is None else s + bias


def _row_max(qg, chunks, sink):
    mx = None
    for kc, _, bias in chunks:
        s = _scores(qg, kc, bias)
        for j in range(s.shape[1] // LANES):
            part = s[:, j * LANES:(j + 1) * LANES]
            mx = part if mx is None else jnp.maximum(mx, part)
    m = jnp.max(mx, axis=1, keepdims=True)
    return m if sink is None else jnp.maximum(m, sink)


def _exp_pv(qg, chunks, m):
    acc = None
    for kc, vc, bias in chunks:
        p = jnp.exp(_scores(qg, kc, bias) - m).astype(BF16)
        d = jnp.dot(p, vc, preferred_element_type=F32)
        acc = d if acc is None else acc + d
    return acc


def _attend_group(q_ref, o_ref, col0, chunks, sinks, lane):
    heads = []
    for pair in range(GROUP // 2):
        qp = q_ref[0, :, col0 + pair * LANES:col0 + (pair + 1) * LANES]
        heads.append(jnp.where(lane < HEAD_DIM, qp, jnp.zeros_like(qp)))
        heads.append(jnp.where(lane >= HEAD_DIM, qp, jnp.zeros_like(qp)))
    ms = [_row_max(qg, chunks, sink) for qg, sink in zip(heads, sinks)]
    accs = [_exp_pv(qg, chunks, m) for qg, m in zip(heads, ms)]
    for pair in range(GROUP // 2):
        a0, a1 = accs[2 * pair], accs[2 * pair + 1]
        r0, r1 = pltpu.roll(a0, HEAD_DIM, 1), pltpu.roll(a1, HEAD_DIM, 1)
        d0, d1 = r0, a1
        if sinks[0] is not None:
            d0 = d0 + jnp.exp(sinks[2 * pair] - ms[2 * pair])
            d1 = d1 + jnp.exp(sinks[2 * pair + 1] - ms[2 * pair + 1])
        o = jnp.where(lane < HEAD_DIM, a0 / d0, r1 / d1)
        o_ref[0, :, col0 + pair * LANES:col0 + (pair + 1) * LANES] = o.astype(o_ref.dtype)


def _win_attn_kernel(q_ref, kp_ref, kc_ref, kn_ref, kz_ref, vp_ref, vc_ref, vn_ref, vz_ref, sink_ref, o_ref,
                     *, tq, seq):
    i = pl.program_id(1)
    r = lax.broadcasted_iota(jnp.int32, (tq, WINDOW), 0)
    c = lax.broadcasted_iota(jnp.int32, (tq, WINDOW), 1)
    bias_prev = jnp.where((c >= r) & (i > 0), 0.0, NEG_INF)
    bias_next = jnp.where((r - c >= tq - WINDOW) & (i < seq // tq - 1), 0.0, NEG_INF)
    bias_cur = None
    if tq - 1 > WINDOW:
        rr = lax.broadcasted_iota(jnp.int32, (tq, tq), 0)
        cc = lax.broadcasted_iota(jnp.int32, (tq, tq), 1)
        bias_cur = jnp.where(jnp.abs(cc - rr) <= WINDOW, 0.0, NEG_INF)
    lane = lax.broadcasted_iota(jnp.int32, (tq, LANES), 1)
    for hk in range(kc_ref.shape[2] // LANES):
        lo = hk * LANES
        chunks = [(kp_ref[0, :, lo:lo + LANES], vp_ref[0, :, lo:lo + LANES], bias_prev),
                  (kc_ref[0, :, lo:lo + LANES], vc_ref[0, :, lo:lo + LANES], bias_cur),
                  (kn_ref[0, :, lo:lo + LANES], vn_ref[0, :, lo:lo + LANES], bias_next),
                  (kz_ref[0, :, lo:lo + LANES], vz_ref[0, :, lo:lo + LANES], None)]
        sinks = [sink_ref[hk, g] for g in range(GROUP)]
        _attend_group(q_ref, o_ref, hk * GROUP * HEAD_DIM, chunks, sinks, lane)


def _window_attention(q, k, v, kz, vz, sink, *, tq):
    bsz, seq, d = q.shape
    nk = k.shape[2]
    ctx = kz.shape[1]
    r = tq // WINDOW
    last = seq // WINDOW - 1
    cur = lambda bi, i: (bi, i, 0)
    prev = lambda bi, i: (bi, jnp.maximum(i * r - 1, 0), 0)
    nxt = lambda bi, i: (bi, jnp.minimum((i + 1) * r, last), 0)
    zmap = lambda bi, i: (bi, 0, 0)
    kv_specs = [pl.BlockSpec((1, WINDOW, nk), prev), pl.BlockSpec((1, tq, nk), cur),
                pl.BlockSpec((1, WINDOW, nk), nxt), pl.BlockSpec((1, ctx, nk), zmap)]
    return pl.pallas_call(
        functools.partial(_win_attn_kernel, tq=tq, seq=seq),
        grid=(bsz, seq // tq),
        in_specs=[pl.BlockSpec((1, tq, d), cur)] + kv_specs + kv_specs
                 + [pl.BlockSpec(memory_space=pltpu.SMEM)],
        out_specs=pl.BlockSpec((1, tq, d), cur),
        out_shape=jax.ShapeDtypeStruct((bsz, seq, d), BF16),
        compiler_params=_cparams(("parallel", "parallel")),
        name="window_attention",
    )(q, k, k, k, kz, v, v, v, vz, sink.reshape(nk // LANES, GROUP).astype(F32))


def _dense_attn_kernel(*refs, ck, has_sink):
    if has_sink:
        q_ref, k_ref, v_ref, sink_ref, o_ref = refs
    else:
        q_ref, k_ref, v_ref, o_ref = refs
    hk = pl.program_id(1)
    tq = q_ref.shape[1]
    chunks = [(k_ref[0, lo:lo + ck, :], v_ref[0, lo:lo + ck, :], None) for lo in range(0, k_ref.shape[1], ck)]
    lane = lax.broadcasted_iota(jnp.int32, (tq, LANES), 1)
    sinks = [sink_ref[hk, g] if has_sink else None for g in range(GROUP)]
    _attend_group(q_ref, o_ref, 0, chunks, sinks, lane)


def _dense_attention(q, k, v, sink=None, *, tq):
    bsz, seq, d = q.shape
    n_keys = k.shape[1]
    n_kv = k.shape[2] // LANES
    ck = 256 if n_keys % 256 == 0 else LANES
    qmap = lambda bi, h, i: (bi, i, h)
    kmap = lambda bi, h, i: (bi, 0, h)
    args = [q, k, v]
    specs = [pl.BlockSpec((1, tq, GROUP * HEAD_DIM), qmap),
             pl.BlockSpec((1, n_keys, LANES), kmap), pl.BlockSpec((1, n_keys, LANES), kmap)]
    if sink is not None:
        args.append(sink.reshape(n_kv, GROUP).astype(F32))
        specs.append(pl.BlockSpec(memory_space=pltpu.SMEM))
    return pl.pallas_call(
        functools.partial(_dense_attn_kernel, ck=ck, has_sink=sink is not None),
        grid=(bsz, n_kv, seq // tq),
        in_specs=specs,
        out_specs=pl.BlockSpec((1, tq, GROUP * HEAD_DIM), qmap),
        out_shape=jax.ShapeDtypeStruct((bsz, seq, d), BF16),
        compiler_params=_cparams(("parallel", "parallel", "parallel")),
        name="dense_attention",
    )(*args)


def _layer_norm(r, g, b):
    mu = jnp.mean(r, axis=-1, keepdims=True)
    rc = r - mu
    var = jnp.mean(rc * rc, axis=-1, keepdims=True)
    return rc * lax.rsqrt(var + LN_EPS) * g + b


def _top_k_route(logits, gate_ref, idx_ref, rank_ref, cnt_ref):
    tm, n_exp = logits.shape
    lane_e = lax.broadcasted_iota(jnp.int32, (tm, n_exp), 1).astype(F32)
    work = logits
    sels, vals, idxs = [], [], []
    for _ in range(TOP_K):
        mk = jnp.max(work, axis=1, keepdims=True)
        ik = jnp.min(jnp.where(work == mk, lane_e, float(n_exp)), axis=1, keepdims=True)
        sel = lane_e == ik
        work = jnp.where(sel, -jnp.inf, work)
        sels.append(sel)
        vals.append(mk)
        idxs.append(ik)
    exps = [jnp.exp(v - vals[0]) for v in vals]
    denom = exps[0]
    for e in exps[1:]:
        denom = denom + e
    routed = sels[0].astype(F32)
    for sel in sels[1:]:
        routed = routed + sel.astype(F32)
    row = lax.broadcasted_iota(jnp.int32, (tm, tm), 0)
    col = lax.broadcasted_iota(jnp.int32, (tm, tm), 1)
    earlier = jnp.where(col < row, 1.0, 0.0).astype(BF16)
    before = jnp.dot(earlier, routed.astype(BF16), preferred_element_type=F32)
    lane_k = lax.broadcasted_iota(jnp.int32, (tm, TOP_K), 1)
    gates = jnp.zeros((tm, TOP_K), F32)
    idx = jnp.zeros((tm, TOP_K), F32)
    rank = jnp.zeros((tm, TOP_K), F32)
    for kk in range(TOP_K):
        here = lane_k == kk
        gates = jnp.where(here, exps[kk] / denom, gates)
        idx = jnp.where(here, idxs[kk], idx)
        rank = jnp.where(here, jnp.sum(jnp.where(sels[kk], before, 0.0), axis=1, keepdims=True), rank)
    gate_ref[0] = gates
    idx_ref[0] = idx.astype(jnp.int32)
    rank_ref[0] = rank.astype(jnp.int32)
    cnt_ref[0] = jnp.sum(routed, axis=0, keepdims=True).astype(jnp.int32)


def _mixer_out_kernel(*refs, conv, alpha, seq):
    if conv:
        bg_ref, u_ref, up_ref, un_ref, cw_ref = refs[:5]
        refs = refs[5:]
    else:
        o_ref = refs[0]
        refs = refs[1:]
    (w_ref, x_ref, mod_ref, lng_ref, lnb_ref, rwh_ref, rwl_ref, rb_ref,
     xo_ref, h_ref, gate_ref, idx_ref, rank_ref, cnt_ref) = refs
    if conv:
        i = pl.program_id(1)
        u = u_ref[0].astype(F32)
        tm = u.shape[0]
        row = lax.broadcasted_iota(jnp.int32, u.shape, 0)
        halo = up_ref.shape[1]
        before = jnp.where(i == 0, 0.0, up_ref[0, halo - 1:halo, :].astype(F32))
        after = jnp.where(i == seq // tm - 1, 0.0, un_ref[0, 0:1, :].astype(F32))
        u_prev = jnp.where(row == 0, before, pltpu.roll(u, 1, 0))
        u_next = jnp.where(row == tm - 1, after, pltpu.roll(u, tm - 1, 0))
        cw = cw_ref[...]
        y = cw[0:1] * u_prev + cw[1:2] * u + cw[2:3] * u_next
        mixed = (bg_ref[0].astype(F32) * y).astype(BF16)
    else:
        mixed = o_ref[0]
    m = mod_ref[0]
    ox = jnp.dot(mixed, w_ref[...], preferred_element_type=F32)
    xn = _layer_norm(alpha * x_ref[0] + m[2:3] * ox, lng_ref[...], lnb_ref[...])
    xo_ref[0] = xn
    h2 = xn * (1.0 + m[4:5]) + m[3:4]
    hh = h2.astype(BF16)
    hl = (h2 - hh.astype(F32)).astype(BF16)
    h_ref[0] = h2
    rwh = rwh_ref[...]
    logits = (jnp.dot(hh, rwh, preferred_element_type=F32) + jnp.dot(hl, rwh, preferred_element_type=F32)
              + jnp.dot(hh, rwl_ref[...], preferred_element_type=F32) + rb_ref[...])
    _top_k_route(logits, gate_ref, idx_ref, rank_ref, cnt_ref)


def _mixer_out(mixed, w_out, x, mod, ln_g, ln_b, router_w, router_b, *, alpha, tm, conv_w=None):
    bsz, seq, d = x.shape
    n_exp = router_w.shape[1]
    nt = seq // tm
    tile = pl.BlockSpec((1, tm, d), lambda bi, i: (bi, i, 0))
    ktile = pl.BlockSpec((1, tm, TOP_K), lambda bi, i: (bi, i, 0))
    row = pl.BlockSpec((1, d), lambda bi, i: (0, 0))
    conv = conv_w is not None
    if conv:
        bg, u = mixed
        halo = 16
        r = tm // halo
        last = seq // halo - 1
        args = [bg, u, u, u, jnp.pad(conv_w.astype(F32), ((0, MOD_ROWS - conv_w.shape[0]), (0, 0)))]
        specs = [tile, tile,
                 pl.BlockSpec((1, halo, d), lambda bi, i: (bi, jnp.maximum(i * r - 1, 0), 0)),
                 pl.BlockSpec((1, halo, d), lambda bi, i: (bi, jnp.minimum((i + 1) * r, last), 0)),
                 pl.BlockSpec((MOD_ROWS, d), lambda bi, i: (0, 0))]
    else:
        args, specs = [mixed], [tile]
    rw_hi = router_w.astype(BF16)
    rw_lo = (router_w - rw_hi.astype(F32)).astype(BF16)
    args += [w_out.astype(BF16), x, mod, ln_g[None, :], ln_b[None, :], rw_hi, rw_lo, router_b[None, :]]
    specs += [pl.BlockSpec((d, d), lambda bi, i: (0, 0)), tile,
              pl.BlockSpec((1, MOD_ROWS, d), lambda bi, i: (bi, 0, 0)), row, row,
              pl.BlockSpec((d, n_exp), lambda bi, i: (0, 0)), pl.BlockSpec((d, n_exp), lambda bi, i: (0, 0)),
              pl.BlockSpec((1, n_exp), lambda bi, i: (0, 0))]
    return pl.pallas_call(
        functools.partial(_mixer_out_kernel, conv=conv, alpha=alpha, seq=seq),
        grid=(bsz, nt),
        in_specs=specs,
        out_specs=[tile, tile, ktile, ktile, ktile,
                   pl.BlockSpec((1, 1, n_exp), lambda bi, i: (bi * nt + i, 0, 0))],
        out_shape=[jax.ShapeDtypeStruct((bsz, seq, d), F32), jax.ShapeDtypeStruct((bsz, seq, d), F32),
                   jax.ShapeDtypeStruct((bsz, seq, TOP_K), F32), jax.ShapeDtypeStruct((bsz, seq, TOP_K), jnp.int32),
                   jax.ShapeDtypeStruct((bsz, seq, TOP_K), jnp.int32),
                   jax.ShapeDtypeStruct((bsz * nt, 1, n_exp), jnp.int32)],
        compiler_params=_cparams(("parallel", "parallel")),
        name="mixer_out",
    )(*args)


def _expert_kernel(be_ref, first_ref, used_ref, x_ref, wgu_ref, bgu_ref, wdn_ref, bdn_ref, o_ref,
                   wgu_bf, wdn_bf, *, ff, fc):
    i = pl.program_id(0)

    @pl.when(first_ref[i] == 1)
    def _():
        wgu_bf[...] = wgu_ref[0, 0].astype(BF16)
        wdn_bf[...] = wdn_ref[0, 0].astype(BF16)

    @pl.when(i < used_ref[0])
    def _():
        xb = x_ref[...].astype(BF16)
        acc = jnp.zeros(o_ref.shape, F32)
        for c in range(ff // fc):
            lo = c * fc
            gate = jnp.dot(xb, wgu_bf[:, lo:lo + fc], preferred_element_type=F32) + bgu_ref[0, 0, :, lo:lo + fc]
            up = jnp.dot(xb, wgu_bf[:, ff + lo:ff + lo + fc], preferred_element_type=F32) \
                + bgu_ref[0, 0, :, ff + lo:ff + lo + fc]
            gate = jnp.minimum(gate, SWIGLU_LIMIT)
            up = jnp.clip(up, -SWIGLU_LIMIT, SWIGLU_LIMIT)
            act = (up + 1.0) * gate * jax.nn.sigmoid(SWIGLU_ALPHA * gate)
            acc = acc + jnp.dot(act.astype(BF16), wdn_bf[lo:lo + fc, :], preferred_element_type=F32)
        o_ref[...] = acc + bdn_ref[0, 0]

    @pl.when(i >= used_ref[0])
    def _():
        o_ref[...] = jnp.zeros_like(o_ref)


def _expert_ffn(xs, blk_expert, blk_first, n_used, layer, w_gu, b_gu, w_down, b_down, *, bm):
    p, d = xs.shape
    depth, n_exp, _, ff2 = w_gu.shape
    ff = ff2 // 2
    wmap = lambda i, be, fi, nu: (layer, be[i], 0, 0)
    grid_spec = pltpu.PrefetchScalarGridSpec(
        num_scalar_prefetch=3,
        grid=(p // bm,),
        in_specs=[pl.BlockSpec((bm, d), lambda i, be, fi, nu: (i, 0)),
                  pl.BlockSpec((1, 1, d, ff2), wmap), pl.BlockSpec((1, 1, 1, ff2), wmap),
                  pl.BlockSpec((1, 1, ff, d), wmap), pl.BlockSpec((1, 1, 1, d), wmap)],
        out_specs=pl.BlockSpec((bm, d), lambda i, be, fi, nu: (i, 0)),
        scratch_shapes=[pltpu.VMEM((d, ff2), BF16), pltpu.VMEM((ff, d), BF16)],
    )
    return pl.pallas_call(
        functools.partial(_expert_kernel, ff=ff, fc=min(ff, 512)),
        grid_spec=grid_spec,
        out_shape=jax.ShapeDtypeStruct((p, d), F32),
        compiler_params=_cparams(("arbitrary",)),
        name="expert_ffn",
    )(blk_expert, blk_first, n_used, xs, w_gu, b_gu.reshape(depth, n_exp, 1, ff2), w_down,
      b_down.reshape(depth, n_exp, 1, d))


def _route(idx, rank, cnt_tiles, tile_rows, bm):
    t = idx.shape[0]
    a = t * TOP_K
    n_exp = cnt_tiles.shape[1]
    counts = jnp.sum(cnt_tiles, axis=0)
    padded = (counts + bm - 1) // bm * bm
    pends = jnp.cumsum(padded)
    pstarts = pends - padded
    tile_off = pstarts[None, :] + jnp.cumsum(cnt_tiles, axis=0) - cnt_tiles
    off_tok, lo = [], 0
    for rows, n_tiles in tile_rows:
        part = tile_off[lo:lo + n_tiles]
        off_tok.append(jnp.broadcast_to(part[:, None, :], (n_tiles, rows, n_exp)).reshape(n_tiles * rows, n_exp))
        lo += n_tiles
    off_tok = jnp.concatenate(off_tok, axis=0)
    chosen = idx[:, :, None] == jnp.arange(n_exp, dtype=jnp.int32)[None, None, :]
    dest = jnp.sum(jnp.where(chosen, off_tok[:, None, :], 0), axis=-1) + rank
    nblk = -(-a // bm) + n_exp
    blk_start = jnp.arange(nblk, dtype=jnp.int32) * bm
    blk_expert = jnp.minimum(jnp.sum((blk_start[:, None] >= pends[None, :]).astype(jnp.int32), axis=1),
                             n_exp - 1)
    blk_first = jnp.concatenate([jnp.ones((1,), jnp.int32),
                                 (blk_expert[1:] != blk_expert[:-1]).astype(jnp.int32)])
    n_used = (pends[-1] // bm).astype(jnp.int32).reshape(1)
    return dest, nblk, blk_expert, blk_first, n_used


def _dispatch_kernel(dest_ref, h_ref, xs_in_ref, xs_ref, sem):
    del xs_in_ref
    tm = h_ref.shape[0]

    def issue(j, carry):
        base = pl.multiple_of(j * SUBLANES, SUBLANES)
        for s in range(SUBLANES):
            for kk in range(TOP_K):
                dst = dest_ref[0, 0, base * TOP_K + s * TOP_K + kk]
                pltpu.make_async_copy(h_ref.at[pl.ds(base + s, 1), :], xs_ref.at[pl.ds(dst, 1), :], sem).start()
        return carry

    lax.fori_loop(0, tm // SUBLANES, issue, 0)
    for kk in range(TOP_K):
        pltpu.make_async_copy(h_ref, xs_ref.at[pl.ds(0, tm), :], sem).wait()


def _dispatch(h, dest, xs, *, tm):
    n, d = h.shape
    nt = n // tm
    return pl.pallas_call(
        _dispatch_kernel,
        grid=(nt,),
        in_specs=[pl.BlockSpec((1, 1, tm * TOP_K), lambda i: (i, 0, 0), memory_space=pltpu.SMEM),
                  pl.BlockSpec((tm, d), lambda i: (i, 0)),
                  pl.BlockSpec(memory_space=pl.ANY)],
        out_specs=pl.BlockSpec(memory_space=pl.ANY),
        out_shape=jax.ShapeDtypeStruct(xs.shape, xs.dtype),
        scratch_shapes=[pltpu.SemaphoreType.DMA(())],
        input_output_aliases={2: 0},
        compiler_params=_cparams(("arbitrary",)),
        name="dispatch_rows",
    )(dest.reshape(nt, 1, tm * TOP_K), h, xs)


def _combine_kernel(dest_ref, out_ref, gate_ref, x_ref, mod_ref, lng_ref, lnb_ref, o_ref, y_buf, sem, *, alpha):
    tm = x_ref.shape[1]

    def issue(j, carry):
        base = pl.multiple_of(j * SUBLANES, SUBLANES)
        for s in range(SUBLANES):
            for kk in range(TOP_K):
                src = dest_ref[0, 0, base * TOP_K + s * TOP_K + kk]
                pltpu.make_async_copy(out_ref.at[pl.ds(src, 1), :], y_buf.at[kk, pl.ds(base + s, 1), :],
                                      sem).start()
        return carry

    lax.fori_loop(0, tm // SUBLANES, issue, 0)
    for kk in range(TOP_K):
        pltpu.make_async_copy(out_ref.at[pl.ds(0, tm), :], y_buf.at[kk], sem).wait()
    g = gate_ref[0]
    fx = g[:, 0:1] * y_buf[0]
    for kk in range(1, TOP_K):
        fx = fx + g[:, kk:kk + 1] * y_buf[kk]
    m = mod_ref[0]
    o_ref[0] = _layer_norm(alpha * x_ref[0] + m[5:6] * fx, lng_ref[...], lnb_ref[...])


def _combine(out, dest, gates, x, mod, ln_g, ln_b, *, alpha, tm):
    bsz, seq, d = x.shape
    nt = seq // tm
    tile = pl.BlockSpec((1, tm, d), lambda bi, i: (bi, i, 0))
    row = pl.BlockSpec((1, d), lambda bi, i: (0, 0))
    return pl.pallas_call(
        functools.partial(_combine_kernel, alpha=alpha),
        grid=(bsz, nt),
        in_specs=[pl.BlockSpec((1, 1, tm * TOP_K), lambda bi, i: (bi * nt + i, 0, 0), memory_space=pltpu.SMEM),
                  pl.BlockSpec(memory_space=pl.ANY),
                  pl.BlockSpec((1, tm, TOP_K), lambda bi, i: (bi, i, 0)),
                  tile, pl.BlockSpec((1, MOD_ROWS, d), lambda bi, i: (bi, 0, 0)), row, row],
        out_specs=tile,
        out_shape=jax.ShapeDtypeStruct((bsz, seq, d), F32),
        scratch_shapes=[pltpu.VMEM((TOP_K, tm, d), F32), pltpu.SemaphoreType.DMA(())],
        compiler_params=_cparams(("arbitrary", "arbitrary")),
        name="combine_norm",
    )(dest.reshape(bsz * nt, 1, tm * TOP_K), out, gates, x, mod, ln_g[None, :], ln_b[None, :])


def _rope_tables(seq):
    t = jnp.arange(seq, dtype=jnp.int32)
    row = (t // GRID_W).astype(F32)
    col = (t % GRID_W).astype(F32)
    n_freq = HEAD_DIM // 4
    inv_freq = ROPE_THETA ** (-jnp.arange(n_freq, dtype=F32) / n_freq)
    ar, ac = row[:, None] * inv_freq, col[:, None] * inv_freq
    cos = jnp.concatenate([jnp.cos(ar), jnp.cos(ar), jnp.cos(ac), jnp.cos(ac)], axis=1)
    sin = jnp.concatenate([-jnp.sin(ar), jnp.sin(ar), -jnp.sin(ac), jnp.sin(ac)], axis=1)
    return jnp.tile(cos, (1, 2)), jnp.tile(sin, (1, 2))


def _tile(n, want):
    t = min(n, want)
    while n % t:
        t //= 2
    return t


def kernel(x, c, ctx, c_ctx, mod_w, mod_b, ln1_g, ln1_b, ln2_g, ln2_b, win_wqkv, win_bqkv, win_sink, win_wo,
           conv_win, conv_w, conv_wout, full_wqkv, full_qnorm, full_knorm, full_wo,
           router_w, router_b, expert_wgu, expert_bgu, expert_wdown, expert_bdown):
    bsz, seq, d = x.shape
    ctx_len = ctx.shape[1]
    depth = mod_w.shape[0]
    alpha = (2.0 * depth) ** 0.25
    tables = _rope_tables(seq)
    tm_x, tm_z = _tile(seq, 512), _tile(ctx_len, 512)
    tq_win = _tile(seq, 256)
    tq_full = _tile(seq, 256)
    bm = 512

    c_rows = jnp.zeros((2 * MOD_ROWS, d), F32).at[:bsz].set(c).at[bsz].set(c_ctx)
    mods = _modulation(c_rows, mod_w, mod_b)
    z = ctx
    for i in range(depth):
        kind, j = i % N_MIXERS, i // N_MIXERS
        need_ctx = i < depth - 1
        mod_x = jnp.pad(mods[i, :bsz].reshape(bsz, 6, d), ((0, 0), (0, MOD_ROWS - 6), (0, 0)))
        mod_z = jnp.broadcast_to(jnp.pad(mods[i, bsz].reshape(1, 6, d), ((0, 0), (0, MOD_ROWS - 6), (0, 0))),
                                 (bsz, MOD_ROWS, d))
        route_args = (router_w[i], router_b[i])
        ln1 = (ln1_g[i], ln1_b[i])
        oz = None
        if kind == 0:
            qx, kx, vx = _qkv_project(x, mod_x, win_wqkv[j], win_bqkv[j], tables=tables, tm=tm_x)
            qz, kz, vz = _qkv_project(z, mod_z, win_wqkv[j], win_bqkv[j], tm=tm_z)
            ox = _window_attention(qx, kx, vx, kz, vz, win_sink[j], tq=tq_win)
            if need_ctx:
                oz = _dense_attention(qz, kz, vz, win_sink[j], tq=ctx_len)
            w_out, conv_taps = win_wo[j], None
        elif kind == 1:
            ox = _conv_in_project(x, mod_x, conv_win[j], tm=tm_x)
            if need_ctx:
                oz = _conv_in_project(z, mod_z, conv_win[j], tm=tm_z)
            w_out, conv_taps = conv_wout[j], conv_w[j]
        else:
            zero_b = jnp.zeros((full_wqkv.shape[2],), F32)
            norms = (full_qnorm[j], full_knorm[j])
            qx, kx, vx = _qkv_project(x, mod_x, full_wqkv[j], zero_b, tables=tables, norms=norms, tm=tm_x)
            qz, kz, vz = _qkv_project(z, mod_z, full_wqkv[j], zero_b, norms=norms, tm=tm_z)
            k_all = jnp.concatenate([kx, kz], axis=1)
            v_all = jnp.concatenate([vx, vz], axis=1)
            ox = _dense_attention(qx, k_all, v_all, tq=tq_full)
            if need_ctx:
                oz = _dense_attention(qz, kz, vz, tq=ctx_len)
            w_out, conv_taps = full_wo[j], None

        x, hx, gx, ix, rx, cx = _mixer_out(ox, w_out, x, mod_x, *ln1, *route_args, alpha=alpha, tm=tm_x,
                                           conv_w=conv_taps)
        n_x = bsz * seq
        idx, rank, cnt = ix.reshape(n_x, TOP_K), rx.reshape(n_x, TOP_K), cx[:, 0]
        tile_rows = [(tm_x, n_x // tm_x)]
        if need_ctx:
            z, hz, gz, iz, rz, cz = _mixer_out(oz, w_out, z, mod_z, *ln1, *route_args, alpha=alpha, tm=tm_z,
                                               conv_w=conv_taps)
            n_z = bsz * ctx_len
            idx = jnp.concatenate([idx, iz.reshape(n_z, TOP_K)], axis=0)
            rank = jnp.concatenate([rank, rz.reshape(n_z, TOP_K)], axis=0)
            cnt = jnp.concatenate([cnt, cz[:, 0]], axis=0)
            tile_rows.append((tm_z, n_z // tm_z))

        dest, nblk, blk_expert, blk_first, n_used = _route(idx, rank, cnt, tile_rows, bm)
        xs = _dispatch(hx.reshape(n_x, d), dest[:n_x], jnp.zeros((nblk * bm, d), F32), tm=tm_x)
        if need_ctx:
            xs = _dispatch(hz.reshape(n_z, d), dest[n_x:], xs, tm=tm_z)
        out = _expert_ffn(xs, blk_expert, blk_first, n_used, i,
                          expert_wgu, expert_bgu, expert_wdown, expert_bdown, bm=bm)
        x = _combine(out, dest[:n_x], gx, x, mod_x, ln2_g[i], ln2_b[i], alpha=alpha, tm=tm_x)
        if need_ctx:
            z = _combine(out, dest[n_x:], gz, z, mod_z, ln2_g[i], ln2_b[i], alpha=alpha, tm=tm_z)
    return x
```

```python
import functools

import jax
import jax.numpy as jnp
from jax import lax
from jax.experimental import pallas as pl
from jax.experimental.pallas import tpu as pltpu

HEAD_DIM = 64
GROUP = 4
GRID_W = 64
WINDOW = 128
ROPE_THETA = 10000.0
TOP_K = 4
N_MIXERS = 3
SWIGLU_ALPHA = 1.702
SWIGLU_LIMIT = 7.0
LN_EPS = 1e-5
RMS_EPS = 1e-6
NEG_INF = -1e30

LANES = 128
SUBLANES = 8
MOD_ROWS = 8
VMEM_LIMIT = 56 * 1024 * 1024

F32 = jnp.float32
BF16 = jnp.bfloat16


def _cparams(sem):
    return pltpu.CompilerParams(dimension_semantics=sem, vmem_limit_bytes=VMEM_LIMIT)


def _mod_kernel(c_ref, w_ref, b_ref, o_ref):
    c = c_ref[...]
    s = c * jax.nn.sigmoid(c)
    o_ref[0] = jnp.dot(s, w_ref[0], preferred_element_type=F32, precision=lax.Precision.HIGHEST) + b_ref[0]


def _modulation(c_rows, mod_w, mod_b):
    depth, d, n = mod_w.shape
    r = c_rows.shape[0]
    tn = min(n, 1536)
    return pl.pallas_call(
        _mod_kernel,
        grid=(depth, n // tn),
        in_specs=[pl.BlockSpec((r, d), lambda l, j: (0, 0)),
                  pl.BlockSpec((1, d, tn), lambda l, j: (l, 0, j)),
                  pl.BlockSpec((1, 1, tn), lambda l, j: (l, 0, j))],
        out_specs=pl.BlockSpec((1, r, tn), lambda l, j: (l, 0, j)),
        out_shape=jax.ShapeDtypeStruct((depth, r, n), F32),
        compiler_params=_cparams(("arbitrary", "arbitrary")),
        name="modulation",
    )(c_rows, mod_w, mod_b.reshape(depth, 1, n))


def _swap_pairs(t):
    lane = lax.broadcasted_iota(jnp.int32, t.shape, 1)
    return jnp.where(lane % 32 < 16, pltpu.roll(t, LANES - 16, 1), pltpu.roll(t, 16, 1))


def _head_mean_sq(t, seg_ref):
    t2 = t * t
    hi = t2.astype(BF16)
    lo = (t2 - hi.astype(F32)).astype(BF16)
    seg = seg_ref[...]
    s = jnp.dot(hi, seg, preferred_element_type=F32) + jnp.dot(lo, seg, preferred_element_type=F32)
    return s * (1.0 / HEAD_DIM)


def _qkv_kernel(*refs, nq, nk, rope, qk_norm):
    x_ref, mod_ref, w_ref, b_ref = refs[:4]
    pos = 4
    if rope:
        cos_ref, sin_ref = refs[pos:pos + 2]
        pos += 2
    if qk_norm:
        seg_ref, qg_ref, kg_ref = refs[pos:pos + 3]
        pos += 3
    q_ref, k_ref, v_ref = refs[pos:pos + 3]
    m = mod_ref[0]
    h = (x_ref[0] * (1.0 + m[1:2]) + m[0:1]).astype(BF16)
    if rope:
        cos, sin = cos_ref[...], sin_ref[...]
    for c in range((nq + 2 * nk) // LANES):
        lo = c * LANES
        t = jnp.dot(h, w_ref[:, lo:lo + LANES], preferred_element_type=F32) + b_ref[:, lo:lo + LANES]
        if lo < nq + nk:
            is_q = lo < nq
            if qk_norm:
                gain = qg_ref[...] if is_q else kg_ref[...]
                t = t * lax.rsqrt(_head_mean_sq(t, seg_ref) + RMS_EPS) * gain
            if rope:
                t = t * cos + _swap_pairs(t) * sin
            if is_q:
                q_ref[0, :, lo:lo + LANES] = (t * HEAD_DIM ** -0.5).astype(BF16)
            else:
                k_ref[0, :, lo - nq:lo - nq + LANES] = t.astype(BF16)
        else:
            v_ref[0, :, lo - nq - nk:lo - nq - nk + LANES] = t.astype(BF16)


def _dup_heads(w, n_heads):
    lead = w.shape[:-1]
    w = w.reshape(lead + (n_heads, 1, HEAD_DIM))
    return jnp.broadcast_to(w, lead + (n_heads, 2, HEAD_DIM)).reshape(lead + (n_heads * LANES,))


def _pad_heads(w, n_heads, fill):
    lead = w.shape[:-1]
    w = w.reshape(lead + (n_heads, HEAD_DIM))
    pad = jnp.full(lead + (n_heads, HEAD_DIM), fill, w.dtype)
    return jnp.concatenate([w, pad], axis=-1).reshape(lead + (n_heads * LANES,))


def _qkv_project(x, mod, w, b, *, tables=None, norms=None, tm):
    bsz, seq, d = x.shape
    n_kv = (w.shape[1] - d) // (2 * HEAD_DIM)
    nq, nk = d, n_kv * LANES
    wq, wk, wv = w[:, :d], w[:, d:d + n_kv * HEAD_DIM], w[:, d + n_kv * HEAD_DIM:]
    w_ext = jnp.concatenate([wq, _dup_heads(wk, n_kv), _pad_heads(wv, n_kv, 0.0)], axis=1).astype(BF16)
    bq, bk, bv = b[:d], b[d:d + n_kv * HEAD_DIM], b[d + n_kv * HEAD_DIM:]
    b_ext = jnp.concatenate([bq, _dup_heads(bk, n_kv), _pad_heads(bv, n_kv, 1.0)])[None, :].astype(F32)
    n = nq + 2 * nk
    args = [x, mod, w_ext, b_ext]
    specs = [pl.BlockSpec((1, tm, d), lambda bi, i: (bi, i, 0)),
             pl.BlockSpec((1, MOD_ROWS, d), lambda bi, i: (bi, 0, 0)),
             pl.BlockSpec((d, n), lambda bi, i: (0, 0)),
             pl.BlockSpec((1, n), lambda bi, i: (0, 0))]
    if tables is not None:
        args += list(tables)
        specs += [pl.BlockSpec((tm, LANES), lambda bi, i: (i, 0))] * 2
    if norms is not None:
        qn, kn = norms
        seg = (jnp.arange(LANES)[:, None] // HEAD_DIM == jnp.arange(LANES)[None, :] // HEAD_DIM).astype(BF16)
        args += [seg, jnp.tile(qn, 2)[None, :].astype(F32), jnp.tile(kn, 2)[None, :].astype(F32)]
        specs += [pl.BlockSpec((LANES, LANES), lambda bi, i: (0, 0)),
                  pl.BlockSpec((1, LANES), lambda bi, i: (0, 0)),
                  pl.BlockSpec((1, LANES), lambda bi, i: (0, 0))]
    return pl.pallas_call(
        functools.partial(_qkv_kernel, nq=nq, nk=nk, rope=tables is not None, qk_norm=norms is not None),
        grid=(bsz, seq // tm),
        in_specs=specs,
        out_specs=[pl.BlockSpec((1, tm, nq), lambda bi, i: (bi, i, 0)),
                   pl.BlockSpec((1, tm, nk), lambda bi, i: (bi, i, 0)),
                   pl.BlockSpec((1, tm, nk), lambda bi, i: (bi, i, 0))],
        out_shape=[jax.ShapeDtypeStruct((bsz, seq, nq), BF16),
                   jax.ShapeDtypeStruct((bsz, seq, nk), BF16),
                   jax.ShapeDtypeStruct((bsz, seq, nk), BF16)],
        compiler_params=_cparams(("parallel", "parallel")),
        name="qkv_project",
    )(*args)


def _conv_in_kernel(x_ref, mod_ref, w_ref, bg_ref, u_ref, *, d):
    m = mod_ref[0]
    h = (x_ref[0] * (1.0 + m[1:2]) + m[0:1]).astype(BF16)
    for c in range(d // LANES):
        lo = c * LANES
        bg = jnp.dot(h, w_ref[:, lo:lo + LANES], preferred_element_type=F32)
        cg = jnp.dot(h, w_ref[:, d + lo:d + lo + LANES], preferred_element_type=F32)
        xv = jnp.dot(h, w_ref[:, 2 * d + lo:2 * d + lo + LANES], preferred_element_type=F32)
        bg_ref[0, :, lo:lo + LANES] = bg.astype(BF16)
        u_ref[0, :, lo:lo + LANES] = (cg * xv).astype(BF16)


def _conv_in_project(x, mod, w_in, *, tm):
    bsz, seq, d = x.shape
    return pl.pallas_call(
        functools.partial(_conv_in_kernel, d=d),
        grid=(bsz, seq // tm),
        in_specs=[pl.BlockSpec((1, tm, d), lambda bi, i: (bi, i, 0)),
                  pl.BlockSpec((1, MOD_ROWS, d), lambda bi, i: (bi, 0, 0)),
                  pl.BlockSpec((d, 3 * d), lambda bi, i: (0, 0))],
        out_specs=[pl.BlockSpec((1, tm, d), lambda bi, i: (bi, i, 0))] * 2,
        out_shape=[jax.ShapeDtypeStruct((bsz, seq, d), BF16)] * 2,
        compiler_params=_cparams(("parallel", "parallel")),
        name="conv_in_project",
    )(x, mod, w_in.astype(BF16))


def _qk(q, k):
    return lax.dot_general(q, k, (((1,), (1,)), ((), ())), preferred_element_type=F32)


def _scores(qg, kc, bias):
    s = _qk(qg, kc)
    return s if bias is None else s + bias


def _row_max(qg, chunks, sink):
    mx = None
    for kc, _, bias in chunks:
        s = _scores(qg, kc, bias)
        for j in range(s.shape[1] // LANES):
            part = s[:, j * LANES:(j + 1) * LANES]
            mx = part if mx is None else jnp.maximum(mx, part)
    m = jnp.max(mx, axis=1, keepdims=True)
    return m if sink is None else jnp.maximum(m, sink)


def _exp_pv(qg, chunks, m):
    acc = None
    for kc, vc, bias in chunks:
        p = jnp.exp(_scores(qg, kc, bias) - m).astype(BF16)
        d = jnp.dot(p, vc, preferred_element_type=F32)
        acc = d if acc is None else acc + d
    return acc


def _attend_group(q_ref, o_ref, col0, chunks, sinks, lane):
    heads = []
    for pair in range(GROUP // 2):
        qp = q_ref[0, :, col0 + pair * LANES:col0 + (pair + 1) * LANES]
        heads.append(jnp.where(lane < HEAD_DIM, qp, jnp.zeros_like(qp)))
        heads.append(jnp.where(lane >= HEAD_DIM, qp, jnp.zeros_like(qp)))
    ms = [_row_max(qg, chunks, sink) for qg, sink in zip(heads, sinks)]
    accs = [_exp_pv(qg, chunks, m) for qg, m in zip(heads, ms)]
    for pair in range(GROUP // 2):
        a0, a1 = accs[2 * pair], accs[2 * pair + 1]
        r0, r1 = pltpu.roll(a0, HEAD_DIM, 1), pltpu.roll(a1, HEAD_DIM, 1)
        d0, d1 = r0, a1
        if sinks[0] is not None:
            d0 = d0 + jnp.exp(sinks[2 * pair] - ms[2 * pair])
            d1 = d1 + jnp.exp(sinks[2 * pair + 1] - ms[2 * pair + 1])
        o = jnp.where(lane < HEAD_DIM, a0 / d0, r1 / d1)
        o_ref[0, :, col0 + pair * LANES:col0 + (pair + 1) * LANES] = o.astype(o_ref.dtype)


def _win_attn_kernel(q_ref, kp_ref, kc_ref, kn_ref, kz_ref, vp_ref, vc_ref, vn_ref, vz_ref, sink_ref, o_ref,
                     *, tq, seq):
    i = pl.program_id(1)
    r = lax.broadcasted_iota(jnp.int32, (tq, WINDOW), 0)
    c = lax.broadcasted_iota(jnp.int32, (tq, WINDOW), 1)
    bias_prev = jnp.where((c >= r) & (i > 0), 0.0, NEG_INF)
    bias_next = jnp.where((r - c >= tq - WINDOW) & (i < seq // tq - 1), 0.0, NEG_INF)
    bias_cur = None
    if tq - 1 > WINDOW:
        rr = lax.broadcasted_iota(jnp.int32, (tq, tq), 0)
        cc = lax.broadcasted_iota(jnp.int32, (tq, tq), 1)
        bias_cur = jnp.where(jnp.abs(cc - rr) <= WINDOW, 0.0, NEG_INF)
    lane = lax.broadcasted_iota(jnp.int32, (tq, LANES), 1)
    for hk in range(kc_ref.shape[2] // LANES):
        lo = hk * LANES
        chunks = [(kp_ref[0, :, lo:lo + LANES], vp_ref[0, :, lo:lo + LANES], bias_prev),
                  (kc_ref[0, :, lo:lo + LANES], vc_ref[0, :, lo:lo + LANES], bias_cur),
                  (kn_ref[0, :, lo:lo + LANES], vn_ref[0, :, lo:lo + LANES], bias_next),
                  (kz_ref[0, :, lo:lo + LANES], vz_ref[0, :, lo:lo + LANES], None)]
        sinks = [sink_ref[hk, g] for g in range(GROUP)]
        _attend_group(q_ref, o_ref, hk * GROUP * HEAD_DIM, chunks, sinks, lane)


def _window_attention(q, k, v, kz, vz, sink, *, tq):
    bsz, seq, d = q.shape
    nk = k.shape[2]
    ctx = kz.shape[1]
    r = tq // WINDOW
    last = seq // WINDOW - 1
    cur = lambda bi, i: (bi, i, 0)
    prev = lambda bi, i: (bi, jnp.maximum(i * r - 1, 0), 0)
    nxt = lambda bi, i: (bi, jnp.minimum((i + 1) * r, last), 0)
    zmap = lambda bi, i: (bi, 0, 0)
    kv_specs = [pl.BlockSpec((1, WINDOW, nk), prev), pl.BlockSpec((1, tq, nk), cur),
                pl.BlockSpec((1, WINDOW, nk), nxt), pl.BlockSpec((1, ctx, nk), zmap)]
    return pl.pallas_call(
        functools.partial(_win_attn_kernel, tq=tq, seq=seq),
        grid=(bsz, seq // tq),
        in_specs=[pl.BlockSpec((1, tq, d), cur)] + kv_specs + kv_specs
                 + [pl.BlockSpec(memory_space=pltpu.SMEM)],
        out_specs=pl.BlockSpec((1, tq, d), cur),
        out_shape=jax.ShapeDtypeStruct((bsz, seq, d), BF16),
        compiler_params=_cparams(("parallel", "parallel")),
        name="window_attention",
    )(q, k, k, k, kz, v, v, v, vz, sink.reshape(nk // LANES, GROUP).astype(F32))


def _dense_attn_kernel(*refs, ck, has_sink):
    if has_sink:
        q_ref, k_ref, v_ref, sink_ref, o_ref = refs
    else:
        q_ref, k_ref, v_ref, o_ref = refs
    hk = pl.program_id(1)
    tq = q_ref.shape[1]
    chunks = [(k_ref[0, lo:lo + ck, :], v_ref[0, lo:lo + ck, :], None) for lo in range(0, k_ref.shape[1], ck)]
    lane = lax.broadcasted_iota(jnp.int32, (tq, LANES), 1)
    sinks = [sink_ref[hk, g] if has_sink else None for g in range(GROUP)]
    _attend_group(q_ref, o_ref, 0, chunks, sinks, lane)


def _dense_attention(q, k, v, sink=None, *, tq):
    bsz, seq, d = q.shape
    n_keys = k.shape[1]
    n_kv = k.shape[2] // LANES
    ck = 256 if n_keys % 256 == 0 else LANES
    qmap = lambda bi, h, i: (bi, i, h)
    kmap = lambda bi, h, i: (bi, 0, h)
    args = [q, k, v]
    specs = [pl.BlockSpec((1, tq, GROUP * HEAD_DIM), qmap),
             pl.BlockSpec((1, n_keys, LANES), kmap), pl.BlockSpec((1, n_keys, LANES), kmap)]
    if sink is not None:
        args.append(sink.reshape(n_kv, GROUP).astype(F32))
        specs.append(pl.BlockSpec(memory_space=pltpu.SMEM))
    return pl.pallas_call(
        functools.partial(_dense_attn_kernel, ck=ck, has_sink=sink is not None),
        grid=(bsz, n_kv, seq // tq),
        in_specs=specs,
        out_specs=pl.BlockSpec((1, tq, GROUP * HEAD_DIM), qmap),
        out_shape=jax.ShapeDtypeStruct((bsz, seq, d), BF16),
        compiler_params=_cparams(("parallel", "parallel", "parallel")),
        name="dense_attention",
    )(*args)


def _store_token_tiles(ref, val):
    rows, d = val.shape
    rt = d // LANES
    for s in range(rt):
        ref[pl.ds(s, rows, stride=rt), :] = val[:, s * LANES:(s + 1) * LANES]


def _load_token_tiles(ref, rows, rt):
    return jnp.concatenate([ref[pl.ds(s, rows, stride=rt), :] for s in range(rt)], axis=1)


def _layer_norm(r, g, b):
    mu = jnp.mean(r, axis=-1, keepdims=True)
    rc = r - mu
    var = jnp.mean(rc * rc, axis=-1, keepdims=True)
    return rc * lax.rsqrt(var + LN_EPS) * g + b


def _top_k_route(logits, gate_ref, idx_ref, rank_ref, cnt_ref):
    tm, n_exp = logits.shape
    lane_e = lax.broadcasted_iota(jnp.int32, (tm, n_exp), 1).astype(F32)
    work = logits
    sels, vals, idxs = [], [], []
    for _ in range(TOP_K):
        mk = jnp.max(work, axis=1, keepdims=True)
        ik = jnp.min(jnp.where(work == mk, lane_e, float(n_exp)), axis=1, keepdims=True)
        sel = lane_e == ik
        work = jnp.where(sel, -jnp.inf, work)
        sels.append(sel)
        vals.append(mk)
        idxs.append(ik)
    exps = [jnp.exp(v - vals[0]) for v in vals]
    denom = exps[0]
    for e in exps[1:]:
        denom = denom + e
    routed = sels[0].astype(F32)
    for sel in sels[1:]:
        routed = routed + sel.astype(F32)
    row = lax.broadcasted_iota(jnp.int32, (tm, tm), 0)
    col = lax.broadcasted_iota(jnp.int32, (tm, tm), 1)
    earlier = jnp.where(col < row, 1.0, 0.0).astype(BF16)
    before = jnp.dot(earlier, routed.astype(BF16), preferred_element_type=F32)
    lane_k = lax.broadcasted_iota(jnp.int32, (tm, TOP_K), 1)
    gates = jnp.zeros((tm, TOP_K), F32)
    idx = jnp.zeros((tm, TOP_K), F32)
    rank = jnp.zeros((tm, TOP_K), F32)
    for kk in range(TOP_K):
        here = lane_k == kk
        gates = jnp.where(here, exps[kk] / denom, gates)
        idx = jnp.where(here, idxs[kk], idx)
        rank = jnp.where(here, jnp.sum(jnp.where(sels[kk], before, 0.0), axis=1, keepdims=True), rank)
    gate_ref[0] = gates
    idx_ref[0] = idx.astype(jnp.int32)
    rank_ref[0] = rank.astype(jnp.int32)
    cnt_ref[0] = jnp.sum(routed, axis=0, keepdims=True).astype(jnp.int32)


def _mixer_out_kernel(*refs, conv, alpha, seq):
    if conv:
        bg_ref, u_ref, up_ref, un_ref, cw_ref = refs[:5]
        refs = refs[5:]
    else:
        o_ref = refs[0]
        refs = refs[1:]
    (w_ref, x_ref, mod_ref, lng_ref, lnb_ref, rwh_ref, rwl_ref, rb_ref,
     xo_ref, h_ref, gate_ref, idx_ref, rank_ref, cnt_ref) = refs
    if conv:
        i = pl.program_id(1)
        u = u_ref[0].astype(F32)
        tm = u.shape[0]
        row = lax.broadcasted_iota(jnp.int32, u.shape, 0)
        halo = up_ref.shape[1]
        before = jnp.where(i == 0, 0.0, up_ref[0, halo - 1:halo, :].astype(F32))
        after = jnp.where(i == seq // tm - 1, 0.0, un_ref[0, 0:1, :].astype(F32))
        u_prev = jnp.where(row == 0, before, pltpu.roll(u, 1, 0))
        u_next = jnp.where(row == tm - 1, after, pltpu.roll(u, tm - 1, 0))
        cw = cw_ref[...]
        y = cw[0:1] * u_prev + cw[1:2] * u + cw[2:3] * u_next
        mixed = (bg_ref[0].astype(F32) * y).astype(BF16)
    else:
        mixed = o_ref[0]
    m = mod_ref[0]
    ox = jnp.dot(mixed, w_ref[...], preferred_element_type=F32)
    xn = _layer_norm(alpha * x_ref[0] + m[2:3] * ox, lng_ref[...], lnb_ref[...])
    xo_ref[0] = xn
    h2 = xn * (1.0 + m[4:5]) + m[3:4]
    hh = h2.astype(BF16)
    hl = (h2 - hh.astype(F32)).astype(BF16)
    _store_token_tiles(h_ref, h2)
    rwh = rwh_ref[...]
    logits = (jnp.dot(hh, rwh, preferred_element_type=F32) + jnp.dot(hl, rwh, preferred_element_type=F32)
              + jnp.dot(hh, rwl_ref[...], preferred_element_type=F32) + rb_ref[...])
    _top_k_route(logits, gate_ref, idx_ref, rank_ref, cnt_ref)


def _mixer_out(mixed, w_out, x, mod, ln_g, ln_b, router_w, router_b, *, alpha, tm, conv_w=None):
    bsz, seq, d = x.shape
    n_exp = router_w.shape[1]
    nt = seq // tm
    tile = pl.BlockSpec((1, tm, d), lambda bi, i: (bi, i, 0))
    ktile = pl.BlockSpec((1, tm, TOP_K), lambda bi, i: (bi, i, 0))
    row = pl.BlockSpec((1, d), lambda bi, i: (0, 0))
    conv = conv_w is not None
    if conv:
        bg, u = mixed
        halo = 16
        r = tm // halo
        last = seq // halo - 1
        args = [bg, u, u, u, jnp.pad(conv_w.astype(F32), ((0, MOD_ROWS - conv_w.shape[0]), (0, 0)))]
        specs = [tile, tile,
                 pl.BlockSpec((1, halo, d), lambda bi, i: (bi, jnp.maximum(i * r - 1, 0), 0)),
                 pl.BlockSpec((1, halo, d), lambda bi, i: (bi, jnp.minimum((i + 1) * r, last), 0)),
                 pl.BlockSpec((MOD_ROWS, d), lambda bi, i: (0, 0))]
    else:
        args, specs = [mixed], [tile]
    rw_hi = router_w.astype(BF16)
    rw_lo = (router_w - rw_hi.astype(F32)).astype(BF16)
    args += [w_out.astype(BF16), x, mod, ln_g[None, :], ln_b[None, :], rw_hi, rw_lo, router_b[None, :]]
    specs += [pl.BlockSpec((d, d), lambda bi, i: (0, 0)), tile,
              pl.BlockSpec((1, MOD_ROWS, d), lambda bi, i: (bi, 0, 0)), row, row,
              pl.BlockSpec((d, n_exp), lambda bi, i: (0, 0)), pl.BlockSpec((d, n_exp), lambda bi, i: (0, 0)),
              pl.BlockSpec((1, n_exp), lambda bi, i: (0, 0))]
    return pl.pallas_call(
        functools.partial(_mixer_out_kernel, conv=conv, alpha=alpha, seq=seq),
        grid=(bsz, nt),
        in_specs=specs,
        out_specs=[tile, pl.BlockSpec((tm * (d // LANES), LANES), lambda bi, i: (bi * nt + i, 0)),
                   ktile, ktile, ktile,
                   pl.BlockSpec((1, 1, n_exp), lambda bi, i: (bi * nt + i, 0, 0))],
        out_shape=[jax.ShapeDtypeStruct((bsz, seq, d), F32),
                   jax.ShapeDtypeStruct((bsz * seq * (d // LANES), LANES), F32),
                   jax.ShapeDtypeStruct((bsz, seq, TOP_K), F32), jax.ShapeDtypeStruct((bsz, seq, TOP_K), jnp.int32),
                   jax.ShapeDtypeStruct((bsz, seq, TOP_K), jnp.int32),
                   jax.ShapeDtypeStruct((bsz * nt, 1, n_exp), jnp.int32)],
        compiler_params=_cparams(("parallel", "parallel")),
        name="mixer_out",
    )(*args)


def _expert_kernel(be_ref, first_ref, used_ref, x_ref, wgu_ref, bgu_ref, wdn_ref, bdn_ref, o_ref,
                   wgu_bf, wdn_bf, *, ff, fc):
    i = pl.program_id(0)

    @pl.when(first_ref[i] == 1)
    def _():
        wgu_bf[...] = wgu_ref[0, 0].astype(BF16)
        wdn_bf[...] = wdn_ref[0, 0].astype(BF16)

    @pl.when(i < used_ref[0])
    def _():
        rt = wdn_bf.shape[1] // LANES
        bm = x_ref.shape[0] // rt
        xb = _load_token_tiles(x_ref, bm, rt).astype(BF16)
        acc = jnp.zeros((bm, rt * LANES), F32)
        for c in range(ff // fc):
            lo = c * fc
            gate = jnp.dot(xb, wgu_bf[:, lo:lo + fc], preferred_element_type=F32) + bgu_ref[0, 0, :, lo:lo + fc]
            up = jnp.dot(xb, wgu_bf[:, ff + lo:ff + lo + fc], preferred_element_type=F32) \
                + bgu_ref[0, 0, :, ff + lo:ff + lo + fc]
            gate = jnp.minimum(gate, SWIGLU_LIMIT)
            up = jnp.clip(up, -SWIGLU_LIMIT, SWIGLU_LIMIT)
            act = (up + 1.0) * gate * jax.nn.sigmoid(SWIGLU_ALPHA * gate)
            acc = acc + jnp.dot(act.astype(BF16), wdn_bf[lo:lo + fc, :], preferred_element_type=F32)
        _store_token_tiles(o_ref, acc + bdn_ref[0, 0])

    @pl.when(i >= used_ref[0])
    def _():
        o_ref[...] = jnp.zeros_like(o_ref)


def _expert_ffn(xs, blk_expert, blk_first, n_used, layer, w_gu, b_gu, w_down, b_down, *, bm):
    depth, n_exp, d, ff2 = w_gu.shape
    ff = ff2 // 2
    rt = d // LANES
    wmap = lambda i, be, fi, nu: (layer, be[i], 0, 0)
    grid_spec = pltpu.PrefetchScalarGridSpec(
        num_scalar_prefetch=3,
        grid=(xs.shape[0] // (bm * rt),),
        in_specs=[pl.BlockSpec((bm * rt, LANES), lambda i, be, fi, nu: (jnp.minimum(i, nu[0] - 1), 0)),
                  pl.BlockSpec((1, 1, d, ff2), wmap), pl.BlockSpec((1, 1, 1, ff2), wmap),
                  pl.BlockSpec((1, 1, ff, d), wmap), pl.BlockSpec((1, 1, 1, d), wmap)],
        out_specs=pl.BlockSpec((bm * rt, LANES), lambda i, be, fi, nu: (i, 0)),
        scratch_shapes=[pltpu.VMEM((d, ff2), BF16), pltpu.VMEM((ff, d), BF16)],
    )
    return pl.pallas_call(
        functools.partial(_expert_kernel, ff=ff, fc=min(ff, 512)),
        grid_spec=grid_spec,
        out_shape=jax.ShapeDtypeStruct(xs.shape, F32),
        compiler_params=_cparams(("arbitrary",)),
        name="expert_ffn",
    )(blk_expert, blk_first, n_used, xs, w_gu, b_gu.reshape(depth, n_exp, 1, ff2), w_down,
      b_down.reshape(depth, n_exp, 1, d))


def _route(idx, rank, cnt_tiles, tile_rows, bm):
    t = idx.shape[0]
    a = t * TOP_K
    n_exp = cnt_tiles.shape[1]
    counts = jnp.sum(cnt_tiles, axis=0)
    padded = (counts + bm - 1) // bm * bm
    pends = jnp.cumsum(padded)
    pstarts = pends - padded
    tile_off = pstarts[None, :] + jnp.cumsum(cnt_tiles, axis=0) - cnt_tiles
    off_tok, lo = [], 0
    for rows, n_tiles in tile_rows:
        part = tile_off[lo:lo + n_tiles]
        off_tok.append(jnp.broadcast_to(part[:, None, :], (n_tiles, rows, n_exp)).reshape(n_tiles * rows, n_exp))
        lo += n_tiles
    off_tok = jnp.concatenate(off_tok, axis=0)
    chosen = idx[:, :, None] == jnp.arange(n_exp, dtype=jnp.int32)[None, None, :]
    dest = jnp.sum(jnp.where(chosen, off_tok[:, None, :], 0), axis=-1) + rank
    nblk = -(-a // bm) + n_exp
    blk_start = jnp.arange(nblk, dtype=jnp.int32) * bm
    blk_expert = jnp.minimum(jnp.sum((blk_start[:, None] >= pends[None, :]).astype(jnp.int32), axis=1),
                             n_exp - 1)
    blk_first = jnp.concatenate([jnp.ones((1,), jnp.int32),
                                 (blk_expert[1:] != blk_expert[:-1]).astype(jnp.int32)])
    n_used = (pends[-1] // bm).astype(jnp.int32).reshape(1)
    last_blk = jnp.maximum(pends // bm - 1, 0).astype(jnp.int32)
    return dest, nblk, blk_expert, blk_first, n_used, last_blk


def _dispatch_kernel(*refs, rt, n_exp, nblk, tiles):
    n_s = len(tiles)
    lb_ref, nu_ref = refs[:2]
    dest_refs = refs[2:2 + n_s]
    h_refs = refs[2 + n_s:2 + 2 * n_s]
    xs_ref, zero_buf, sem, zero_sem = refs[2 + 2 * n_s:]
    i = pl.program_id(0)
    blk_rows = zero_buf.shape[0]

    @pl.when(i == 0)
    def _():
        zero_buf[...] = jnp.zeros_like(zero_buf)

    @pl.when(i < 2 * n_exp)
    def _():
        blk = jnp.where(i < n_exp, lb_ref[jnp.minimum(i, n_exp - 1)], jnp.minimum(nu_ref[0] + i - n_exp, nblk - 1))
        copy = pltpu.make_async_copy(
            zero_buf, xs_ref.at[pl.ds(pl.multiple_of(blk * blk_rows, blk_rows), blk_rows), :], zero_sem)
        copy.start()
        copy.wait()

    first = 2 * n_exp
    for dest_ref, h_ref, (tm, nt) in zip(dest_refs, h_refs, tiles):
        @pl.when((i >= first) & (i < first + nt))
        def _(dest_ref=dest_ref, h_ref=h_ref, tm=tm):
            def issue(j, carry):
                base = j * SUBLANES
                for s in range(SUBLANES):
                    src = h_ref.at[pl.ds(pl.multiple_of((base + s) * rt, rt), rt), :]
                    for kk in range(TOP_K):
                        dst = pl.multiple_of(dest_ref[0, 0, (base + s) * TOP_K + kk], rt)
                        pltpu.make_async_copy(src, xs_ref.at[pl.ds(dst, rt), :], sem).start()
                return carry

            lax.fori_loop(0, tm // SUBLANES, issue, 0)
            for kk in range(TOP_K):
                pltpu.make_async_copy(h_ref, xs_ref.at[pl.ds(0, tm * rt), :], sem).wait()

        first += nt


def _dispatch(streams, last_blk, n_used, nblk, *, bm, rt):
    n_exp = last_blk.shape[0]
    tiles = [(tm, h.shape[0] // (tm * rt)) for h, _, tm in streams]
    dest_specs, h_specs, dests, hs = [], [], [], []
    first = 2 * n_exp
    for (h, dest_rows, tm), (_, nt) in zip(streams, tiles):
        tile_of = lambda i, lb, nu, first=first, nt=nt: jnp.clip(i - first, 0, nt - 1)
        dest_specs.append(pl.BlockSpec((1, 1, tm * TOP_K), lambda i, lb, nu, f=tile_of: (f(i, lb, nu), 0, 0),
                                       memory_space=pltpu.SMEM))
        h_specs.append(pl.BlockSpec((tm * rt, LANES), lambda i, lb, nu, f=tile_of: (f(i, lb, nu), 0)))
        dests.append(dest_rows.reshape(nt, 1, tm * TOP_K))
        hs.append(h)
        first += nt
    return pl.pallas_call(
        functools.partial(_dispatch_kernel, rt=rt, n_exp=n_exp, nblk=nblk, tiles=tiles),
        grid_spec=pltpu.PrefetchScalarGridSpec(
            num_scalar_prefetch=2, grid=(first,), in_specs=dest_specs + h_specs,
            out_specs=pl.BlockSpec(memory_space=pl.ANY),
            scratch_shapes=[pltpu.VMEM((bm * rt, LANES), F32), pltpu.SemaphoreType.DMA(()),
                            pltpu.SemaphoreType.DMA(())]),
        out_shape=jax.ShapeDtypeStruct((nblk * bm * rt, LANES), F32),
        compiler_params=_cparams(("arbitrary",)),
        name="dispatch_rows",
    )(last_blk, n_used, *dests, *hs)


def _combine_kernel(dest_ref, out_ref, gate_ref, x_ref, mod_ref, lng_ref, lnb_ref, o_ref, y_buf, sem, *, alpha):
    tm, d = x_ref.shape[1:]
    rt = d // LANES

    def issue(j, carry):
        base = j * SUBLANES
        for s in range(SUBLANES):
            row = pl.multiple_of((base + s) * rt, rt)
            for kk in range(TOP_K):
                src = pl.multiple_of(dest_ref[0, 0, (base + s) * TOP_K + kk], rt)
                pltpu.make_async_copy(out_ref.at[pl.ds(src, rt), :], y_buf.at[kk, pl.ds(row, rt), :], sem).start()
        return carry

    lax.fori_loop(0, tm // SUBLANES, issue, 0)
    for kk in range(TOP_K):
        pltpu.make_async_copy(out_ref.at[pl.ds(0, tm * rt), :], y_buf.at[kk], sem).wait()
    g = gate_ref[0]
    fx = g[:, 0:1] * _load_token_tiles(y_buf.at[0], tm, rt)
    for kk in range(1, TOP_K):
        fx = fx + g[:, kk:kk + 1] * _load_token_tiles(y_buf.at[kk], tm, rt)
    m = mod_ref[0]
    o_ref[0] = _layer_norm(alpha * x_ref[0] + m[5:6] * fx, lng_ref[...], lnb_ref[...])


def _combine(out, dest_rows, gates, x, mod, ln_g, ln_b, *, alpha, tm):
    bsz, seq, d = x.shape
    nt = seq // tm
    dest = dest_rows
    tile = pl.BlockSpec((1, tm, d), lambda bi, i: (bi, i, 0))
    row = pl.BlockSpec((1, d), lambda bi, i: (0, 0))
    return pl.pallas_call(
        functools.partial(_combine_kernel, alpha=alpha),
        grid=(bsz, nt),
        in_specs=[pl.BlockSpec((1, 1, tm * TOP_K), lambda bi, i: (bi * nt + i, 0, 0), memory_space=pltpu.SMEM),
                  pl.BlockSpec(memory_space=pl.ANY),
                  pl.BlockSpec((1, tm, TOP_K), lambda bi, i: (bi, i, 0)),
                  tile, pl.BlockSpec((1, MOD_ROWS, d), lambda bi, i: (bi, 0, 0)), row, row],
        out_specs=tile,
        out_shape=jax.ShapeDtypeStruct((bsz, seq, d), F32),
        scratch_shapes=[pltpu.VMEM((TOP_K, tm * (d // LANES), LANES), F32), pltpu.SemaphoreType.DMA(())],
        compiler_params=_cparams(("arbitrary", "arbitrary")),
        name="combine_norm",
    )(dest.reshape(bsz * nt, 1, tm * TOP_K), out, gates, x, mod, ln_g[None, :], ln_b[None, :])


def _rope_tables(seq):
    t = jnp.arange(seq, dtype=jnp.int32)
    row = (t // GRID_W).astype(F32)
    col = (t % GRID_W).astype(F32)
    n_freq = HEAD_DIM // 4
    inv_freq = ROPE_THETA ** (-jnp.arange(n_freq, dtype=F32) / n_freq)
    ar, ac = row[:, None] * inv_freq, col[:, None] * inv_freq
    cos = jnp.concatenate([jnp.cos(ar), jnp.cos(ar), jnp.cos(ac), jnp.cos(ac)], axis=1)
    sin = jnp.concatenate([-jnp.sin(ar), jnp.sin(ar), -jnp.sin(ac), jnp.sin(ac)], axis=1)
    return jnp.tile(cos, (1, 2)), jnp.tile(sin, (1, 2))


def _tile(n, want):
    t = min(n, want)
    while n % t:
        t //= 2
    return t


def kernel(x, c, ctx, c_ctx, mod_w, mod_b, ln1_g, ln1_b, ln2_g, ln2_b, win_wqkv, win_bqkv, win_sink, win_wo,
           conv_win, conv_w, conv_wout, full_wqkv, full_qnorm, full_knorm, full_wo,
           router_w, router_b, expert_wgu, expert_bgu, expert_wdown, expert_bdown):
    bsz, seq, d = x.shape
    ctx_len = ctx.shape[1]
    depth = mod_w.shape[0]
    alpha = (2.0 * depth) ** 0.25
    tables = _rope_tables(seq)
    tm_x, tm_z = _tile(seq, 512), _tile(ctx_len, 512)
    tq_win = _tile(seq, 256)
    tq_full = _tile(seq, 256)
    bm = 512

    c_rows = jnp.zeros((2 * MOD_ROWS, d), F32).at[:bsz].set(c).at[bsz].set(c_ctx)
    mods = _modulation(c_rows, mod_w, mod_b)
    z = ctx
    for i in range(depth):
        kind, j = i % N_MIXERS, i // N_MIXERS
        need_ctx = i < depth - 1
        mod_x = jnp.pad(mods[i, :bsz].reshape(bsz, 6, d), ((0, 0), (0, MOD_ROWS - 6), (0, 0)))
        mod_z = jnp.broadcast_to(jnp.pad(mods[i, bsz].reshape(1, 6, d), ((0, 0), (0, MOD_ROWS - 6), (0, 0))),
                                 (bsz, MOD_ROWS, d))
        route_args = (router_w[i], router_b[i])
        ln1 = (ln1_g[i], ln1_b[i])
        oz = None
        if kind == 0:
            qx, kx, vx = _qkv_project(x, mod_x, win_wqkv[j], win_bqkv[j], tables=tables, tm=tm_x)
            qz, kz, vz = _qkv_project(z, mod_z, win_wqkv[j], win_bqkv[j], tm=tm_z)
            ox = _window_attention(qx, kx, vx, kz, vz, win_sink[j], tq=tq_win)
            if need_ctx:
                oz = _dense_attention(qz, kz, vz, win_sink[j], tq=ctx_len)
            w_out, conv_taps = win_wo[j], None
        elif kind == 1:
            ox = _conv_in_project(x, mod_x, conv_win[j], tm=tm_x)
            if need_ctx:
                oz = _conv_in_project(z, mod_z, conv_win[j], tm=tm_z)
            w_out, conv_taps = conv_wout[j], conv_w[j]
        else:
            zero_b = jnp.zeros((full_wqkv.shape[2],), F32)
            norms = (full_qnorm[j], full_knorm[j])
            qx, kx, vx = _qkv_project(x, mod_x, full_wqkv[j], zero_b, tables=tables, norms=norms, tm=tm_x)
            qz, kz, vz = _qkv_project(z, mod_z, full_wqkv[j], zero_b, norms=norms, tm=tm_z)
            k_all = jnp.concatenate([kx, kz], axis=1)
            v_all = jnp.concatenate([vx, vz], axis=1)
            ox = _dense_attention(qx, k_all, v_all, tq=tq_full)
            if need_ctx:
                oz = _dense_attention(qz, kz, vz, tq=ctx_len)
            w_out, conv_taps = full_wo[j], None

        x, hx, gx, ix, rx, cx = _mixer_out(ox, w_out, x, mod_x, *ln1, *route_args, alpha=alpha, tm=tm_x,
                                           conv_w=conv_taps)
        n_x = bsz * seq
        idx, rank, cnt = ix.reshape(n_x, TOP_K), rx.reshape(n_x, TOP_K), cx[:, 0]
        tile_rows = [(tm_x, n_x // tm_x)]
        if need_ctx:
            z, hz, gz, iz, rz, cz = _mixer_out(oz, w_out, z, mod_z, *ln1, *route_args, alpha=alpha, tm=tm_z,
                                               conv_w=conv_taps)
            n_z = bsz * ctx_len
            idx = jnp.concatenate([idx, iz.reshape(n_z, TOP_K)], axis=0)
            rank = jnp.concatenate([rank, rz.reshape(n_z, TOP_K)], axis=0)
            cnt = jnp.concatenate([cnt, cz[:, 0]], axis=0)
            tile_rows.append((tm_z, n_z // tm_z))

        dest, nblk, blk_expert, blk_first, n_used, last_blk = _route(idx, rank, cnt, tile_rows, bm)
        rt = d // LANES
        dest_rows = dest * rt
        streams = [(hx, dest_rows[:n_x], tm_x)]
        if need_ctx:
            streams.append((hz, dest_rows[n_x:], tm_z))
        xs = _dispatch(streams, last_blk, n_used, nblk, bm=bm, rt=rt)
        out = _expert_ffn(xs, blk_expert, blk_first, n_used, i,
                          expert_wgu, expert_bgu, expert_wdown, expert_bdown, bm=bm)
        x = _combine(out, dest_rows[:n_x], gx, x, mod_x, ln2_g[i], ln2_b[i], alpha=alpha, tm=tm_x)
        if need_ctx:
            z = _combine(out, dest_rows[n_x:], gz, z, mod_z, ln2_g[i], ln2_b[i], alpha=alpha, tm=tm_z)
    return x
```

```python
import functools

import jax
import jax.numpy as jnp
from jax import lax
from jax.experimental import pallas as pl
from jax.experimental.pallas import tpu as pltpu

HEAD_DIM = 64
GROUP = 4
GRID_W = 64
WINDOW = 128
ROPE_THETA = 10000.0
TOP_K = 4
N_MIXERS = 3
SWIGLU_ALPHA = 1.702
SWIGLU_LIMIT = 7.0
LN_EPS = 1e-5
RMS_EPS = 1e-6
NEG_INF = -1e30

LANES = 128
SUBLANES = 8
MOD_ROWS = 8
VMEM_LIMIT = 56 * 1024 * 1024

F32 = jnp.float32
BF16 = jnp.bfloat16


def _cparams(sem):
    return pltpu.CompilerParams(dimension_semantics=sem, vmem_limit_bytes=VMEM_LIMIT)


def _mod_kernel(c_ref, w_ref, b_ref, o_ref):
    c = c_ref[...]
    s = c * jax.nn.sigmoid(c)
    o_ref[0] = jnp.dot(s, w_ref[0], preferred_element_type=F32, precision=lax.Precision.HIGHEST) + b_ref[0]


def _modulation(c_rows, mod_w, mod_b):
    depth, d, n = mod_w.shape
    r = c_rows.shape[0]
    tn = min(n, 1536)
    return pl.pallas_call(
        _mod_kernel,
        grid=(depth, n // tn),
        in_specs=[pl.BlockSpec((r, d), lambda l, j: (0, 0)),
                  pl.BlockSpec((1, d, tn), lambda l, j: (l, 0, j)),
                  pl.BlockSpec((1, 1, tn), lambda l, j: (l, 0, j))],
        out_specs=pl.BlockSpec((1, r, tn), lambda l, j: (l, 0, j)),
        out_shape=jax.ShapeDtypeStruct((depth, r, n), F32),
        compiler_params=_cparams(("arbitrary", "arbitrary")),
        name="modulation",
    )(c_rows, mod_w, mod_b.reshape(depth, 1, n))


def _swap_pairs(t):
    lane = lax.broadcasted_iota(jnp.int32, t.shape, 1)
    return jnp.where(lane % 32 < 16, pltpu.roll(t, LANES - 16, 1), pltpu.roll(t, 16, 1))


def _head_mean_sq(t, seg_ref):
    t2 = t * t
    hi = t2.astype(BF16)
    lo = (t2 - hi.astype(F32)).astype(BF16)
    seg = seg_ref[...]
    s = jnp.dot(hi, seg, preferred_element_type=F32) + jnp.dot(lo, seg, preferred_element_type=F32)
    return s * (1.0 / HEAD_DIM)


def _qkv_kernel(*refs, nq, nk, rope, qk_norm):
    x_ref, mod_ref, w_ref, b_ref = refs[:4]
    pos = 4
    if rope:
        cos_ref, sin_ref = refs[pos:pos + 2]
        pos += 2
    if qk_norm:
        seg_ref, qg_ref, kg_ref = refs[pos:pos + 3]
        pos += 3
    q_ref, k_ref, v_ref = refs[pos:pos + 3]
    m = mod_ref[0]
    h = (x_ref[0] * (1.0 + m[1:2]) + m[0:1]).astype(BF16)
    if rope:
        cos, sin = cos_ref[...], sin_ref[...]
    for c in range((nq + 2 * nk) // LANES):
        lo = c * LANES
        t = jnp.dot(h, w_ref[:, lo:lo + LANES], preferred_element_type=F32) + b_ref[:, lo:lo + LANES]
        if lo < nq + nk:
            is_q = lo < nq
            if qk_norm:
                gain = qg_ref[...] if is_q else kg_ref[...]
                t = t * lax.rsqrt(_head_mean_sq(t, seg_ref) + RMS_EPS) * gain
            if rope:
                t = t * cos + _swap_pairs(t) * sin
            if is_q:
                q_ref[0, :, lo:lo + LANES] = (t * HEAD_DIM ** -0.5).astype(BF16)
            else:
                k_ref[0, :, lo - nq:lo - nq + LANES] = t.astype(BF16)
        else:
            v_ref[0, :, lo - nq - nk:lo - nq - nk + LANES] = t.astype(BF16)


def _dup_heads(w, n_heads):
    lead = w.shape[:-1]
    w = w.reshape(lead + (n_heads, 1, HEAD_DIM))
    return jnp.broadcast_to(w, lead + (n_heads, 2, HEAD_DIM)).reshape(lead + (n_heads * LANES,))


def _pad_heads(w, n_heads, fill):
    lead = w.shape[:-1]
    w = w.reshape(lead + (n_heads, HEAD_DIM))
    pad = jnp.full(lead + (n_heads, HEAD_DIM), fill, w.dtype)
    return jnp.concatenate([w, pad], axis=-1).reshape(lead + (n_heads * LANES,))


def _qkv_project(x, mod, w, b, *, tables=None, norms=None, tm):
    bsz, seq, d = x.shape
    n_kv = (w.shape[1] - d) // (2 * HEAD_DIM)
    nq, nk = d, n_kv * LANES
    wq, wk, wv = w[:, :d], w[:, d:d + n_kv * HEAD_DIM], w[:, d + n_kv * HEAD_DIM:]
    w_ext = jnp.concatenate([wq, _dup_heads(wk, n_kv), _pad_heads(wv, n_kv, 0.0)], axis=1).astype(BF16)
    bq, bk, bv = b[:d], b[d:d + n_kv * HEAD_DIM], b[d + n_kv * HEAD_DIM:]
    b_ext = jnp.concatenate([bq, _dup_heads(bk, n_kv), _pad_heads(bv, n_kv, 1.0)])[None, :].astype(F32)
    n = nq + 2 * nk
    args = [x, mod, w_ext, b_ext]
    specs = [pl.BlockSpec((1, tm, d), lambda bi, i: (bi, i, 0)),
             pl.BlockSpec((1, MOD_ROWS, d), lambda bi, i: (bi, 0, 0)),
             pl.BlockSpec((d, n), lambda bi, i: (0, 0)),
             pl.BlockSpec((1, n), lambda bi, i: (0, 0))]
    if tables is not None:
        args += list(tables)
        specs += [pl.BlockSpec((tm, LANES), lambda bi, i: (i, 0))] * 2
    if norms is not None:
        qn, kn = norms
        seg = (jnp.arange(LANES)[:, None] // HEAD_DIM == jnp.arange(LANES)[None, :] // HEAD_DIM).astype(BF16)
        args += [seg, jnp.tile(qn, 2)[None, :].astype(F32), jnp.tile(kn, 2)[None, :].astype(F32)]
        specs += [pl.BlockSpec((LANES, LANES), lambda bi, i: (0, 0)),
                  pl.BlockSpec((1, LANES), lambda bi, i: (0, 0)),
                  pl.BlockSpec((1, LANES), lambda bi, i: (0, 0))]
    return pl.pallas_call(
        functools.partial(_qkv_kernel, nq=nq, nk=nk, rope=tables is not None, qk_norm=norms is not None),
        grid=(bsz, seq // tm),
        in_specs=specs,
        out_specs=[pl.BlockSpec((1, tm, nq), lambda bi, i: (bi, i, 0)),
                   pl.BlockSpec((1, tm, nk), lambda bi, i: (bi, i, 0)),
                   pl.BlockSpec((1, tm, nk), lambda bi, i: (bi, i, 0))],
        out_shape=[jax.ShapeDtypeStruct((bsz, seq, nq), BF16),
                   jax.ShapeDtypeStruct((bsz, seq, nk), BF16),
                   jax.ShapeDtypeStruct((bsz, seq, nk), BF16)],
        compiler_params=_cparams(("parallel", "parallel")),
        name="qkv_project",
    )(*args)


def _conv_in_kernel(x_ref, mod_ref, w_ref, bg_ref, u_ref, *, d):
    m = mod_ref[0]
    h = (x_ref[0] * (1.0 + m[1:2]) + m[0:1]).astype(BF16)
    for c in range(d // LANES):
        lo = c * LANES
        bg = jnp.dot(h, w_ref[:, lo:lo + LANES], preferred_element_type=F32)
        cg = jnp.dot(h, w_ref[:, d + lo:d + lo + LANES], preferred_element_type=F32)
        xv = jnp.dot(h, w_ref[:, 2 * d + lo:2 * d + lo + LANES], preferred_element_type=F32)
        bg_ref[0, :, lo:lo + LANES] = bg.astype(BF16)
        u_ref[0, :, lo:lo + LANES] = (cg * xv).astype(BF16)


def _conv_in_project(x, mod, w_in, *, tm):
    bsz, seq, d = x.shape
    return pl.pallas_call(
        functools.partial(_conv_in_kernel, d=d),
        grid=(bsz, seq // tm),
        in_specs=[pl.BlockSpec((1, tm, d), lambda bi, i: (bi, i, 0)),
                  pl.BlockSpec((1, MOD_ROWS, d), lambda bi, i: (bi, 0, 0)),
                  pl.BlockSpec((d, 3 * d), lambda bi, i: (0, 0))],
        out_specs=[pl.BlockSpec((1, tm, d), lambda bi, i: (bi, i, 0))] * 2,
        out_shape=[jax.ShapeDtypeStruct((bsz, seq, d), BF16)] * 2,
        compiler_params=_cparams(("parallel", "parallel")),
        name="conv_in_project",
    )(x, mod, w_in.astype(BF16))


def _qk(q, k):
    return lax.dot_general(q, k, (((1,), (1,)), ((), ())), preferred_element_type=F32)


def _scores(qg, kc, bias):
    s = _qk(qg, kc)
    return s if bias is None else s + bias


def _row_max(qg, chunks, sink):
    mx = None
    for kc, _, bias in chunks:
        s = _scores(qg, kc, bias)
        for j in range(s.shape[1] // LANES):
            part = s[:, j * LANES:(j + 1) * LANES]
            mx = part if mx is None else jnp.maximum(mx, part)
    m = jnp.max(mx, axis=1, keepdims=True)
    return m if sink is None else jnp.maximum(m, sink)


def _exp_pv(qg, chunks, m):
    acc = None
    for kc, vc, bias in chunks:
        p = jnp.exp(_scores(qg, kc, bias) - m).astype(BF16)
        d = jnp.dot(p, vc, preferred_element_type=F32)
        acc = d if acc is None else acc + d
    return acc


def _attend_group(q_ref, o_ref, col0, chunks, sinks, lane):
    heads = []
    for pair in range(GROUP // 2):
        qp = q_ref[0, :, col0 + pair * LANES:col0 + (pair + 1) * LANES]
        heads.append(jnp.where(lane < HEAD_DIM, qp, jnp.zeros_like(qp)))
        heads.append(jnp.where(lane >= HEAD_DIM, qp, jnp.zeros_like(qp)))
    ms = [_row_max(qg, chunks, sink) for qg, sink in zip(heads, sinks)]
    accs = [_exp_pv(qg, chunks, m) for qg, m in zip(heads, ms)]
    for pair in range(GROUP // 2):
        a0, a1 = accs[2 * pair], accs[2 * pair + 1]
        r0, r1 = pltpu.roll(a0, HEAD_DIM, 1), pltpu.roll(a1, HEAD_DIM, 1)
        d0, d1 = r0, a1
        if sinks[0] is not None:
            d0 = d0 + jnp.exp(sinks[2 * pair] - ms[2 * pair])
            d1 = d1 + jnp.exp(sinks[2 * pair + 1] - ms[2 * pair + 1])
        o = jnp.where(lane < HEAD_DIM, a0 / d0, r1 / d1)
        o_ref[0, :, col0 + pair * LANES:col0 + (pair + 1) * LANES] = o.astype(o_ref.dtype)


def _win_attn_kernel(q_ref, kp_ref, kc_ref, kn_ref, kz_ref, vp_ref, vc_ref, vn_ref, vz_ref, sink_ref, o_ref,
                     *, tq, seq):
    i = pl.program_id(1)
    r = lax.broadcasted_iota(jnp.int32, (tq, WINDOW), 0)
    c = lax.broadcasted_iota(jnp.int32, (tq, WINDOW), 1)
    bias_prev = jnp.where((c >= r) & (i > 0), 0.0, NEG_INF)
    bias_next = jnp.where((r - c >= tq - WINDOW) & (i < seq // tq - 1), 0.0, NEG_INF)
    bias_cur = None
    if tq - 1 > WINDOW:
        rr = lax.broadcasted_iota(jnp.int32, (tq, tq), 0)
        cc = lax.broadcasted_iota(jnp.int32, (tq, tq), 1)
        bias_cur = jnp.where(jnp.abs(cc - rr) <= WINDOW, 0.0, NEG_INF)
    lane = lax.broadcasted_iota(jnp.int32, (tq, LANES), 1)
    for hk in range(kc_ref.shape[2] // LANES):
        lo = hk * LANES
        chunks = [(kp_ref[0, :, lo:lo + LANES], vp_ref[0, :, lo:lo + LANES], bias_prev),
                  (kc_ref[0, :, lo:lo + LANES], vc_ref[0, :, lo:lo + LANES], bias_cur),
                  (kn_ref[0, :, lo:lo + LANES], vn_ref[0, :, lo:lo + LANES], bias_next),
                  (kz_ref[0, :, lo:lo + LANES], vz_ref[0, :, lo:lo + LANES], None)]
        sinks = [sink_ref[hk, g] for g in range(GROUP)]
        _attend_group(q_ref, o_ref, hk * GROUP * HEAD_DIM, chunks, sinks, lane)


def _window_attention(q, k, v, kz, vz, sink, *, tq):
    bsz, seq, d = q.shape
    nk = k.shape[2]
    ctx = kz.shape[1]
    r = tq // WINDOW
    last = seq // WINDOW - 1
    cur = lambda bi, i: (bi, i, 0)
    prev = lambda bi, i: (bi, jnp.maximum(i * r - 1, 0), 0)
    nxt = lambda bi, i: (bi, jnp.minimum((i + 1) * r, last), 0)
    zmap = lambda bi, i: (bi, 0, 0)
    kv_specs = [pl.BlockSpec((1, WINDOW, nk), prev), pl.BlockSpec((1, tq, nk), cur),
                pl.BlockSpec((1, WINDOW, nk), nxt), pl.BlockSpec((1, ctx, nk), zmap)]
    return pl.pallas_call(
        functools.partial(_win_attn_kernel, tq=tq, seq=seq),
        grid=(bsz, seq // tq),
        in_specs=[pl.BlockSpec((1, tq, d), cur)] + kv_specs + kv_specs
                 + [pl.BlockSpec(memory_space=pltpu.SMEM)],
        out_specs=pl.BlockSpec((1, tq, d), cur),
        out_shape=jax.ShapeDtypeStruct((bsz, seq, d), BF16),
        compiler_params=_cparams(("parallel", "parallel")),
        name="window_attention",
    )(q, k, k, k, kz, v, v, v, vz, sink.reshape(nk // LANES, GROUP).astype(F32))


def _dense_attn_kernel(*refs, ck, has_sink):
    if has_sink:
        q_ref, k_ref, v_ref, sink_ref, o_ref = refs
    else:
        q_ref, k_ref, v_ref, o_ref = refs
    hk = pl.program_id(1)
    tq = q_ref.shape[1]
    chunks = [(k_ref[0, lo:lo + ck, :], v_ref[0, lo:lo + ck, :], None) for lo in range(0, k_ref.shape[1], ck)]
    lane = lax.broadcasted_iota(jnp.int32, (tq, LANES), 1)
    sinks = [sink_ref[hk, g] if has_sink else None for g in range(GROUP)]
    _attend_group(q_ref, o_ref, 0, chunks, sinks, lane)


def _dense_attention(q, k, v, sink=None, *, tq):
    bsz, seq, d = q.shape
    n_keys = k.shape[1]
    n_kv = k.shape[2] // LANES
    ck = 256 if n_keys % 256 == 0 else LANES
    qmap = lambda bi, h, i: (bi, i, h)
    kmap = lambda bi, h, i: (bi, 0, h)
    args = [q, k, v]
    specs = [pl.BlockSpec((1, tq, GROUP * HEAD_DIM), qmap),
             pl.BlockSpec((1, n_keys, LANES), kmap), pl.BlockSpec((1, n_keys, LANES), kmap)]
    if sink is not None:
        args.append(sink.reshape(n_kv, GROUP).astype(F32))
        specs.append(pl.BlockSpec(memory_space=pltpu.SMEM))
    return pl.pallas_call(
        functools.partial(_dense_attn_kernel, ck=ck, has_sink=sink is not None),
        grid=(bsz, n_kv, seq // tq),
        in_specs=specs,
        out_specs=pl.BlockSpec((1, tq, GROUP * HEAD_DIM), qmap),
        out_shape=jax.ShapeDtypeStruct((bsz, seq, d), BF16),
        compiler_params=_cparams(("parallel", "parallel", "parallel")),
        name="dense_attention",
    )(*args)


def _store_token_tiles(ref, val):
    rows, d = val.shape
    rt = d // LANES
    for s in range(rt):
        ref[pl.ds(s, rows, stride=rt), :] = val[:, s * LANES:(s + 1) * LANES]


def _load_token_tiles(ref, rows, rt):
    return jnp.concatenate([ref[pl.ds(s, rows, stride=rt), :] for s in range(rt)], axis=1)


def _layer_norm(r, g, b):
    mu = jnp.mean(r, axis=-1, keepdims=True)
    rc = r - mu
    var = jnp.mean(rc * rc, axis=-1, keepdims=True)
    return rc * lax.rsqrt(var + LN_EPS) * g + b


def _top_k_route(logits, gate_ref, idx_ref, rank_ref, cnt_ref):
    tm, n_exp = logits.shape
    lane_e = lax.broadcasted_iota(jnp.int32, (tm, n_exp), 1).astype(F32)
    work = logits
    sels, vals, idxs = [], [], []
    for _ in range(TOP_K):
        mk = jnp.max(work, axis=1, keepdims=True)
        ik = jnp.min(jnp.where(work == mk, lane_e, float(n_exp)), axis=1, keepdims=True)
        sel = lane_e == ik
        work = jnp.where(sel, -jnp.inf, work)
        sels.append(sel)
        vals.append(mk)
        idxs.append(ik)
    exps = [jnp.exp(v - vals[0]) for v in vals]
    denom = exps[0]
    for e in exps[1:]:
        denom = denom + e
    routed = sels[0].astype(F32)
    for sel in sels[1:]:
        routed = routed + sel.astype(F32)
    row = lax.broadcasted_iota(jnp.int32, (tm, tm), 0)
    col = lax.broadcasted_iota(jnp.int32, (tm, tm), 1)
    earlier = jnp.where(col < row, 1.0, 0.0).astype(BF16)
    before = jnp.dot(earlier, routed.astype(BF16), preferred_element_type=F32)
    lane_k = lax.broadcasted_iota(jnp.int32, (tm, TOP_K), 1)
    gates = jnp.zeros((tm, TOP_K), F32)
    idx = jnp.zeros((tm, TOP_K), F32)
    rank = jnp.zeros((tm, TOP_K), F32)
    for kk in range(TOP_K):
        here = lane_k == kk
        gates = jnp.where(here, exps[kk] / denom, gates)
        idx = jnp.where(here, idxs[kk], idx)
        rank = jnp.where(here, jnp.sum(jnp.where(sels[kk], before, 0.0), axis=1, keepdims=True), rank)
    gate_ref[0] = gates
    idx_ref[0] = idx.astype(jnp.int32)
    rank_ref[0] = rank.astype(jnp.int32)
    cnt_ref[0] = jnp.sum(routed, axis=0, keepdims=True).astype(jnp.int32)


def _mixer_out_kernel(*refs, conv, alpha, seq):
    if conv:
        bg_ref, u_ref, up_ref, un_ref, cw_ref = refs[:5]
        refs = refs[5:]
    else:
        o_ref = refs[0]
        refs = refs[1:]
    (w_ref, x_ref, mod_ref, lng_ref, lnb_ref, rwh_ref, rwl_ref, rb_ref,
     xo_ref, h_ref, gate_ref, idx_ref, rank_ref, cnt_ref) = refs
    if conv:
        i = pl.program_id(1)
        u = u_ref[0].astype(F32)
        tm = u.shape[0]
        row = lax.broadcasted_iota(jnp.int32, u.shape, 0)
        halo = up_ref.shape[1]
        before = jnp.where(i == 0, 0.0, up_ref[0, halo - 1:halo, :].astype(F32))
        after = jnp.where(i == seq // tm - 1, 0.0, un_ref[0, 0:1, :].astype(F32))
        u_prev = jnp.where(row == 0, before, pltpu.roll(u, 1, 0))
        u_next = jnp.where(row == tm - 1, after, pltpu.roll(u, tm - 1, 0))
        cw = cw_ref[...]
        y = cw[0:1] * u_prev + cw[1:2] * u + cw[2:3] * u_next
        mixed = (bg_ref[0].astype(F32) * y).astype(BF16)
    else:
        mixed = o_ref[0]
    m = mod_ref[0]
    ox = jnp.dot(mixed, w_ref[...], preferred_element_type=F32)
    xn = _layer_norm(alpha * x_ref[0] + m[2:3] * ox, lng_ref[...], lnb_ref[...])
    xo_ref[0] = xn
    h2 = xn * (1.0 + m[4:5]) + m[3:4]
    hh = h2.astype(BF16)
    hl = (h2 - hh.astype(F32)).astype(BF16)
    _store_token_tiles(h_ref, h2)
    rwh = rwh_ref[...]
    logits = (jnp.dot(hh, rwh, preferred_element_type=F32) + jnp.dot(hl, rwh, preferred_element_type=F32)
              + jnp.dot(hh, rwl_ref[...], preferred_element_type=F32) + rb_ref[...])
    _top_k_route(logits, gate_ref, idx_ref, rank_ref, cnt_ref)


def _mixer_out(mixed, w_out, x, mod, ln_g, ln_b, router_w, router_b, *, alpha, tm, conv_w=None):
    bsz, seq, d = x.shape
    n_exp = router_w.shape[1]
    nt = seq // tm
    tile = pl.BlockSpec((1, tm, d), lambda bi, i: (bi, i, 0))
    ktile = pl.BlockSpec((1, tm, TOP_K), lambda bi, i: (bi, i, 0))
    row = pl.BlockSpec((1, d), lambda bi, i: (0, 0))
    conv = conv_w is not None
    if conv:
        bg, u = mixed
        halo = 16
        r = tm // halo
        last = seq // halo - 1
        args = [bg, u, u, u, jnp.pad(conv_w.astype(F32), ((0, MOD_ROWS - conv_w.shape[0]), (0, 0)))]
        specs = [tile, tile,
                 pl.BlockSpec((1, halo, d), lambda bi, i: (bi, jnp.maximum(i * r - 1, 0), 0)),
                 pl.BlockSpec((1, halo, d), lambda bi, i: (bi, jnp.minimum((i + 1) * r, last), 0)),
                 pl.BlockSpec((MOD_ROWS, d), lambda bi, i: (0, 0))]
    else:
        args, specs = [mixed], [tile]
    rw_hi = router_w.astype(BF16)
    rw_lo = (router_w - rw_hi.astype(F32)).astype(BF16)
    args += [w_out.astype(BF16), x, mod, ln_g[None, :], ln_b[None, :], rw_hi, rw_lo, router_b[None, :]]
    specs += [pl.BlockSpec((d, d), lambda bi, i: (0, 0)), tile,
              pl.BlockSpec((1, MOD_ROWS, d), lambda bi, i: (bi, 0, 0)), row, row,
              pl.BlockSpec((d, n_exp), lambda bi, i: (0, 0)), pl.BlockSpec((d, n_exp), lambda bi, i: (0, 0)),
              pl.BlockSpec((1, n_exp), lambda bi, i: (0, 0))]
    return pl.pallas_call(
        functools.partial(_mixer_out_kernel, conv=conv, alpha=alpha, seq=seq),
        grid=(bsz, nt),
        in_specs=specs,
        out_specs=[tile, pl.BlockSpec((tm * (d // LANES), LANES), lambda bi, i: (bi * nt + i, 0)),
                   ktile, ktile, ktile,
                   pl.BlockSpec((1, 1, n_exp), lambda bi, i: (bi * nt + i, 0, 0))],
        out_shape=[jax.ShapeDtypeStruct((bsz, seq, d), F32),
                   jax.ShapeDtypeStruct((bsz * seq * (d // LANES), LANES), F32),
                   jax.ShapeDtypeStruct((bsz, seq, TOP_K), F32), jax.ShapeDtypeStruct((bsz, seq, TOP_K), jnp.int32),
                   jax.ShapeDtypeStruct((bsz, seq, TOP_K), jnp.int32),
                   jax.ShapeDtypeStruct((bsz * nt, 1, n_exp), jnp.int32)],
        compiler_params=_cparams(("parallel", "parallel")),
        name="mixer_out",
    )(*args)


def _expert_kernel(be_ref, first_ref, used_ref, x_ref, wgu_ref, bgu_ref, wdn_ref, bdn_ref, o_ref,
                   wgu_bf, wdn_bf, *, ff, fc):
    i = pl.program_id(0)

    @pl.when(first_ref[i] == 1)
    def _():
        wgu_bf[...] = wgu_ref[0, 0].astype(BF16)
        wdn_bf[...] = wdn_ref[0, 0].astype(BF16)

    @pl.when(i < used_ref[0])
    def _():
        rt = wdn_bf.shape[1] // LANES
        bm = x_ref.shape[0] // rt
        xb = _load_token_tiles(x_ref, bm, rt).astype(BF16)
        acc = jnp.zeros((bm, rt * LANES), F32)
        for c in range(ff // fc):
            lo = c * fc
            gate = jnp.dot(xb, wgu_bf[:, lo:lo + fc], preferred_element_type=F32) + bgu_ref[0, 0, :, lo:lo + fc]
            up = jnp.dot(xb, wgu_bf[:, ff + lo:ff + lo + fc], preferred_element_type=F32) \
                + bgu_ref[0, 0, :, ff + lo:ff + lo + fc]
            gate = jnp.minimum(gate, SWIGLU_LIMIT)
            up = jnp.clip(up, -SWIGLU_LIMIT, SWIGLU_LIMIT)
            act = (up + 1.0) * gate * jax.nn.sigmoid(SWIGLU_ALPHA * gate)
            acc = acc + jnp.dot(act.astype(BF16), wdn_bf[lo:lo + fc, :], preferred_element_type=F32)
        _store_token_tiles(o_ref, acc + bdn_ref[0, 0])

    @pl.when(i >= used_ref[0])
    def _():
        o_ref[...] = jnp.zeros_like(o_ref)


def _expert_ffn(xs, blk_expert, blk_first, n_used, layer, w_gu, b_gu, w_down, b_down, *, bm):
    depth, n_exp, d, ff2 = w_gu.shape
    ff = ff2 // 2
    rt = d // LANES
    wmap = lambda i, be, fi, nu: (layer, be[i], 0, 0)
    grid_spec = pltpu.PrefetchScalarGridSpec(
        num_scalar_prefetch=3,
        grid=(xs.shape[0] // (bm * rt),),
        in_specs=[pl.BlockSpec((bm * rt, LANES), lambda i, be, fi, nu: (jnp.minimum(i, nu[0] - 1), 0)),
                  pl.BlockSpec((1, 1, d, ff2), wmap), pl.BlockSpec((1, 1, 1, ff2), wmap),
                  pl.BlockSpec((1, 1, ff, d), wmap), pl.BlockSpec((1, 1, 1, d), wmap)],
        out_specs=pl.BlockSpec((bm * rt, LANES), lambda i, be, fi, nu: (i, 0)),
        scratch_shapes=[pltpu.VMEM((d, ff2), BF16), pltpu.VMEM((ff, d), BF16)],
    )
    return pl.pallas_call(
        functools.partial(_expert_kernel, ff=ff, fc=min(ff, 512)),
        grid_spec=grid_spec,
        out_shape=jax.ShapeDtypeStruct(xs.shape, F32),
        compiler_params=_cparams(("arbitrary",)),
        name="expert_ffn",
    )(blk_expert, blk_first, n_used, xs, w_gu, b_gu.reshape(depth, n_exp, 1, ff2), w_down,
      b_down.reshape(depth, n_exp, 1, d))


def _route(idx, rank, cnt_tiles, tile_rows, bm):
    t = idx.shape[0]
    a = t * TOP_K
    n_exp = cnt_tiles.shape[1]
    counts = jnp.sum(cnt_tiles, axis=0)
    padded = (counts + bm - 1) // bm * bm
    pends = jnp.cumsum(padded)
    pstarts = pends - padded
    tile_off = pstarts[None, :] + jnp.cumsum(cnt_tiles, axis=0) - cnt_tiles
    off_tok, lo = [], 0
    for rows, n_tiles in tile_rows:
        part = tile_off[lo:lo + n_tiles]
        off_tok.append(jnp.broadcast_to(part[:, None, :], (n_tiles, rows, n_exp)).reshape(n_tiles * rows, n_exp))
        lo += n_tiles
    off_tok = jnp.concatenate(off_tok, axis=0)
    chosen = idx[:, :, None] == jnp.arange(n_exp, dtype=jnp.int32)[None, None, :]
    dest = jnp.sum(jnp.where(chosen, off_tok[:, None, :], 0), axis=-1) + rank
    nblk = -(-a // bm) + n_exp
    blk_start = jnp.arange(nblk, dtype=jnp.int32) * bm
    blk_expert = jnp.minimum(jnp.sum((blk_start[:, None] >= pends[None, :]).astype(jnp.int32), axis=1),
                             n_exp - 1)
    blk_first = jnp.concatenate([jnp.ones((1,), jnp.int32),
                                 (blk_expert[1:] != blk_expert[:-1]).astype(jnp.int32)])
    n_used = (pends[-1] // bm).astype(jnp.int32).reshape(1)
    last_blk = jnp.maximum(pends // bm - 1, 0).astype(jnp.int32)
    return dest, nblk, blk_expert, blk_first, n_used, last_blk


def _dispatch_kernel(*refs, rt, n_exp, nblk, tiles):
    n_s = len(tiles)
    lb_ref, nu_ref = refs[:2]
    dest_refs = refs[2:2 + n_s]
    h_refs = refs[2 + n_s:2 + 2 * n_s]
    xs_ref, zero_buf, sem, zero_sem = refs[2 + 2 * n_s:]
    i = pl.program_id(0)
    blk_rows = zero_buf.shape[0]

    def zero_copy(blk):
        return pltpu.make_async_copy(
            zero_buf, xs_ref.at[pl.ds(pl.multiple_of(blk * blk_rows, blk_rows), blk_rows), :], zero_sem)

    @pl.when(i == 0)
    def _():
        zero_buf[...] = jnp.zeros_like(zero_buf)
        todo = [(lb_ref[0], None)]
        todo += [(lb_ref[e], lb_ref[e] != lb_ref[e - 1]) for e in range(1, n_exp)]
        todo += [(nu_ref[0] + e, nu_ref[0] + e < nblk) for e in range(n_exp)]
        for blk, cond in todo:
            if cond is None:
                zero_copy(blk).start()
            else:
                pl.when(cond)(lambda blk=blk: zero_copy(blk).start())
        for _, cond in todo:
            if cond is None:
                zero_copy(0).wait()
            else:
                pl.when(cond)(lambda: zero_copy(0).wait())

    first = 0
    for dest_ref, h_ref, (tm, nt) in zip(dest_refs, h_refs, tiles):
        @pl.when((i >= first) & (i < first + nt))
        def _(dest_ref=dest_ref, h_ref=h_ref, tm=tm):
            def issue(j, carry):
                base = j * SUBLANES
                for s in range(SUBLANES):
                    src = h_ref.at[pl.ds(pl.multiple_of((base + s) * rt, rt), rt), :]
                    for kk in range(TOP_K):
                        dst = pl.multiple_of(dest_ref[0, 0, (base + s) * TOP_K + kk], rt)
                        pltpu.make_async_copy(src, xs_ref.at[pl.ds(dst, rt), :], sem).start(priority=kk % 2)
                return carry

            lax.fori_loop(0, tm // SUBLANES, issue, 0)
            for kk in range(TOP_K):
                pltpu.make_async_copy(h_ref, xs_ref.at[pl.ds(0, tm * rt), :], sem).wait()

        first += nt


def _dispatch(streams, last_blk, n_used, nblk, *, bm, rt):
    n_exp = last_blk.shape[0]
    tiles = [(tm, h.shape[0] // (tm * rt)) for h, _, tm in streams]
    dest_specs, h_specs, dests, hs = [], [], [], []
    first = 0
    for (h, dest_rows, tm), (_, nt) in zip(streams, tiles):
        tile_of = lambda i, lb, nu, first=first, nt=nt: jnp.clip(i - first, 0, nt - 1)
        dest_specs.append(pl.BlockSpec((1, 1, tm * TOP_K), lambda i, lb, nu, f=tile_of: (f(i, lb, nu), 0, 0),
                                       memory_space=pltpu.SMEM))
        h_specs.append(pl.BlockSpec((tm * rt, LANES), lambda i, lb, nu, f=tile_of: (f(i, lb, nu), 0)))
        dests.append(dest_rows.reshape(nt, 1, tm * TOP_K))
        hs.append(h)
        first += nt
    return pl.pallas_call(
        functools.partial(_dispatch_kernel, rt=rt, n_exp=n_exp, nblk=nblk, tiles=tiles),
        grid_spec=pltpu.PrefetchScalarGridSpec(
            num_scalar_prefetch=2, grid=(first,), in_specs=dest_specs + h_specs,
            out_specs=pl.BlockSpec(memory_space=pl.ANY),
            scratch_shapes=[pltpu.VMEM((bm * rt, LANES), F32), pltpu.SemaphoreType.DMA(()),
                            pltpu.SemaphoreType.DMA(())]),
        out_shape=jax.ShapeDtypeStruct((nblk * bm * rt, LANES), F32),
        compiler_params=_cparams(("arbitrary",)),
        name="dispatch_rows",
    )(last_blk, n_used, *dests, *hs)


def _combine_kernel(dest_ref, out_ref, gate_ref, x_ref, mod_ref, lng_ref, lnb_ref, o_ref, y_buf, sem, *, alpha):
    tm, d = x_ref.shape[1:]
    rt = d // LANES

    def issue(j, carry):
        base = j * SUBLANES
        for s in range(SUBLANES):
            row = pl.multiple_of((base + s) * rt, rt)
            for kk in range(TOP_K):
                src = pl.multiple_of(dest_ref[0, 0, (base + s) * TOP_K + kk], rt)
                pltpu.make_async_copy(out_ref.at[pl.ds(src, rt), :], y_buf.at[kk, pl.ds(row, rt), :],
                                      sem).start(priority=kk % 2)
        return carry

    lax.fori_loop(0, tm // SUBLANES, issue, 0)
    for kk in range(TOP_K):
        pltpu.make_async_copy(out_ref.at[pl.ds(0, tm * rt), :], y_buf.at[kk], sem).wait()
    g = gate_ref[0]
    fx = g[:, 0:1] * _load_token_tiles(y_buf.at[0], tm, rt)
    for kk in range(1, TOP_K):
        fx = fx + g[:, kk:kk + 1] * _load_token_tiles(y_buf.at[kk], tm, rt)
    m = mod_ref[0]
    o_ref[0] = _layer_norm(alpha * x_ref[0] + m[5:6] * fx, lng_ref[...], lnb_ref[...])


def _combine(out, dest_rows, gates, x, mod, ln_g, ln_b, *, alpha, tm):
    bsz, seq, d = x.shape
    nt = seq // tm
    dest = dest_rows
    tile = pl.BlockSpec((1, tm, d), lambda bi, i: (bi, i, 0))
    row = pl.BlockSpec((1, d), lambda bi, i: (0, 0))
    return pl.pallas_call(
        functools.partial(_combine_kernel, alpha=alpha),
        grid=(bsz, nt),
        in_specs=[pl.BlockSpec((1, 1, tm * TOP_K), lambda bi, i: (bi * nt + i, 0, 0), memory_space=pltpu.SMEM),
                  pl.BlockSpec(memory_space=pl.ANY),
                  pl.BlockSpec((1, tm, TOP_K), lambda bi, i: (bi, i, 0)),
                  tile, pl.BlockSpec((1, MOD_ROWS, d), lambda bi, i: (bi, 0, 0)), row, row],
        out_specs=tile,
        out_shape=jax.ShapeDtypeStruct((bsz, seq, d), F32),
        scratch_shapes=[pltpu.VMEM((TOP_K, tm * (d // LANES), LANES), F32), pltpu.SemaphoreType.DMA(())],
        compiler_params=_cparams(("arbitrary", "arbitrary")),
        name="combine_norm",
    )(dest.reshape(bsz * nt, 1, tm * TOP_K), out, gates, x, mod, ln_g[None, :], ln_b[None, :])


def _rope_tables(seq):
    t = jnp.arange(seq, dtype=jnp.int32)
    row = (t // GRID_W).astype(F32)
    col = (t % GRID_W).astype(F32)
    n_freq = HEAD_DIM // 4
    inv_freq = ROPE_THETA ** (-jnp.arange(n_freq, dtype=F32) / n_freq)
    ar, ac = row[:, None] * inv_freq, col[:, None] * inv_freq
    cos = jnp.concatenate([jnp.cos(ar), jnp.cos(ar), jnp.cos(ac), jnp.cos(ac)], axis=1)
    sin = jnp.concatenate([-jnp.sin(ar), jnp.sin(ar), -jnp.sin(ac), jnp.sin(ac)], axis=1)
    return jnp.tile(cos, (1, 2)), jnp.tile(sin, (1, 2))


def _tile(n, want):
    t = min(n, want)
    while n % t:
        t //= 2
    return t


def kernel(x, c, ctx, c_ctx, mod_w, mod_b, ln1_g, ln1_b, ln2_g, ln2_b, win_wqkv, win_bqkv, win_sink, win_wo,
           conv_win, conv_w, conv_wout, full_wqkv, full_qnorm, full_knorm, full_wo,
           router_w, router_b, expert_wgu, expert_bgu, expert_wdown, expert_bdown):
    bsz, seq, d = x.shape
    ctx_len = ctx.shape[1]
    depth = mod_w.shape[0]
    alpha = (2.0 * depth) ** 0.25
    tables = _rope_tables(seq)
    tm_x, tm_z = _tile(seq, 512), _tile(ctx_len, 512)
    tq_win = _tile(seq, 256)
    tq_full = _tile(seq, 256)
    bm = 768

    c_rows = jnp.zeros((2 * MOD_ROWS, d), F32).at[:bsz].set(c).at[bsz].set(c_ctx)
    mods = _modulation(c_rows, mod_w, mod_b)
    z = ctx
    for i in range(depth):
        kind, j = i % N_MIXERS, i // N_MIXERS
        need_ctx = i < depth - 1
        mod_x = jnp.pad(mods[i, :bsz].reshape(bsz, 6, d), ((0, 0), (0, MOD_ROWS - 6), (0, 0)))
        mod_z = jnp.broadcast_to(jnp.pad(mods[i, bsz].reshape(1, 6, d), ((0, 0), (0, MOD_ROWS - 6), (0, 0))),
                                 (bsz, MOD_ROWS, d))
        route_args = (router_w[i], router_b[i])
        ln1 = (ln1_g[i], ln1_b[i])
        oz = None
        if kind == 0:
            qx, kx, vx = _qkv_project(x, mod_x, win_wqkv[j], win_bqkv[j], tables=tables, tm=tm_x)
            qz, kz, vz = _qkv_project(z, mod_z, win_wqkv[j], win_bqkv[j], tm=tm_z)
            ox = _window_attention(qx, kx, vx, kz, vz, win_sink[j], tq=tq_win)
            if need_ctx:
                oz = _dense_attention(qz, kz, vz, win_sink[j], tq=ctx_len)
            w_out, conv_taps = win_wo[j], None
        elif kind == 1:
            ox = _conv_in_project(x, mod_x, conv_win[j], tm=tm_x)
            if need_ctx:
                oz = _conv_in_project(z, mod_z, conv_win[j], tm=tm_z)
            w_out, conv_taps = conv_wout[j], conv_w[j]
        else:
            zero_b = jnp.zeros((full_wqkv.shape[2],), F32)
            norms = (full_qnorm[j], full_knorm[j])
            qx, kx, vx = _qkv_project(x, mod_x, full_wqkv[j], zero_b, tables=tables, norms=norms, tm=tm_x)
            qz, kz, vz = _qkv_project(z, mod_z, full_wqkv[j], zero_b, norms=norms, tm=tm_z)
            k_all = jnp.concatenate([kx, kz], axis=1)
            v_all = jnp.concatenate([vx, vz], axis=1)
            ox = _dense_attention(qx, k_all, v_all, tq=tq_full)
            if need_ctx:
                oz = _dense_attention(qz, kz, vz, tq=ctx_len)
            w_out, conv_taps = full_wo[j], None

        x, hx, gx, ix, rx, cx = _mixer_out(ox, w_out, x, mod_x, *ln1, *route_args, alpha=alpha, tm=tm_x,
                                           conv_w=conv_taps)
        n_x = bsz * seq
        idx, rank, cnt = ix.reshape(n_x, TOP_K), rx.reshape(n_x, TOP_K), cx[:, 0]
        tile_rows = [(tm_x, n_x // tm_x)]
        if need_ctx:
            z, hz, gz, iz, rz, cz = _mixer_out(oz, w_out, z, mod_z, *ln1, *route_args, alpha=alpha, tm=tm_z,
                                               conv_w=conv_taps)
            n_z = bsz * ctx_len
            idx = jnp.concatenate([idx, iz.reshape(n_z, TOP_K)], axis=0)
            rank = jnp.concatenate([rank, rz.reshape(n_z, TOP_K)], axis=0)
            cnt = jnp.concatenate([cnt, cz[:, 0]], axis=0)
            tile_rows.append((tm_z, n_z // tm_z))

        dest, nblk, blk_expert, blk_first, n_used, last_blk = _route(idx, rank, cnt, tile_rows, bm)
        rt = d // LANES
        dest_rows = dest * rt
        streams = [(hx, dest_rows[:n_x], tm_x)]
        if need_ctx:
            streams.append((hz, dest_rows[n_x:], tm_z))
        xs = _dispatch(streams, last_blk, n_used, nblk, bm=bm, rt=rt)
        out = _expert_ffn(xs, blk_expert, blk_first, n_used, i,
                          expert_wgu, expert_bgu, expert_wdown, expert_bdown, bm=bm)
        x = _combine(out, dest_rows[:n_x], gx, x, mod_x, ln2_g[i], ln2_b[i], alpha=alpha, tm=tm_x)
        if need_ctx:
            z = _combine(out, dest_rows[n_x:], gz, z, mod_z, ln2_g[i], ln2_b[i], alpha=alpha, tm=tm_z)
    return x
```

```python
import functools

import jax
import jax.numpy as jnp
from jax import lax
from jax.experimental import pallas as pl
from jax.experimental.pallas import tpu as pltpu

HEAD_DIM = 64
GROUP = 4
GRID_W = 64
WINDOW = 128
ROPE_THETA = 10000.0
TOP_K = 4
N_MIXERS = 3
SWIGLU_ALPHA = 1.702
SWIGLU_LIMIT = 7.0
LN_EPS = 1e-5
RMS_EPS = 1e-6
NEG_INF = -1e30

LANES = 128
SUBLANES = 8
MXU_COLS = 256
MOD_ROWS = 8
VMEM_LIMIT = 56 * 1024 * 1024

F32 = jnp.float32
BF16 = jnp.bfloat16


def _cparams(sem):
    return pltpu.CompilerParams(dimension_semantics=sem, vmem_limit_bytes=VMEM_LIMIT)


def _mod_kernel(c_ref, w_ref, b_ref, o_ref):
    c = c_ref[...]
    s = c * jax.nn.sigmoid(c)
    o_ref[0] = jnp.dot(s, w_ref[0], preferred_element_type=F32, precision=lax.Precision.HIGHEST) + b_ref[0]


def _modulation(c_rows, mod_w, mod_b):
    depth, d, n = mod_w.shape
    r = c_rows.shape[0]
    tn = min(n, 1536)
    return pl.pallas_call(
        _mod_kernel,
        grid=(depth, n // tn),
        in_specs=[pl.BlockSpec((r, d), lambda l, j: (0, 0)),
                  pl.BlockSpec((1, d, tn), lambda l, j: (l, 0, j)),
                  pl.BlockSpec((1, 1, tn), lambda l, j: (l, 0, j))],
        out_specs=pl.BlockSpec((1, r, tn), lambda l, j: (l, 0, j)),
        out_shape=jax.ShapeDtypeStruct((depth, r, n), F32),
        compiler_params=_cparams(("arbitrary", "arbitrary")),
        name="modulation",
    )(c_rows, mod_w, mod_b.reshape(depth, 1, n))


def _swap_pairs(t):
    lane = lax.broadcasted_iota(jnp.int32, t.shape, 1)
    return jnp.where(lane % 32 < 16, pltpu.roll(t, LANES - 16, 1), pltpu.roll(t, 16, 1))


def _head_mean_sq(t, seg_ref):
    t2 = t * t
    hi = t2.astype(BF16)
    lo = (t2 - hi.astype(F32)).astype(BF16)
    seg = seg_ref[...]
    s = jnp.dot(hi, seg, preferred_element_type=F32) + jnp.dot(lo, seg, preferred_element_type=F32)
    return s * (1.0 / HEAD_DIM)


def _qkv_kernel(*refs, nq, nk, rope, qk_norm):
    x_ref, mod_ref, w_ref, b_ref = refs[:4]
    pos = 4
    if rope:
        cos_ref, sin_ref = refs[pos:pos + 2]
        pos += 2
    if qk_norm:
        seg_ref, qg_ref, kg_ref = refs[pos:pos + 3]
        pos += 3
    q_ref, k_ref, v_ref = refs[pos:pos + 3]
    m = mod_ref[0]
    h = (x_ref[0] * (1.0 + m[1:2]) + m[0:1]).astype(BF16)
    if rope:
        cos, sin = cos_ref[...], sin_ref[...]
    for c in range((nq + 2 * nk) // MXU_COLS):
        lo = c * MXU_COLS
        wide = jnp.dot(h, w_ref[:, lo:lo + MXU_COLS], preferred_element_type=F32) + b_ref[:, lo:lo + MXU_COLS]
        if lo >= nq + nk:
            v_ref[0, :, lo - nq - nk:lo - nq - nk + MXU_COLS] = wide.astype(BF16)
            continue
        is_q = lo < nq
        if qk_norm:
            gain = qg_ref[...] if is_q else kg_ref[...]
            wide = wide * lax.rsqrt(_head_mean_sq(wide, seg_ref) + RMS_EPS) * gain
        for half in range(MXU_COLS // LANES):
            t = wide[:, half * LANES:(half + 1) * LANES]
            at = lo + half * LANES
            if rope:
                t = t * cos + _swap_pairs(t) * sin
            if is_q:
                q_ref[0, :, at:at + LANES] = (t * HEAD_DIM ** -0.5).astype(BF16)
            else:
                k_ref[0, :, at - nq:at - nq + LANES] = t.astype(BF16)


def _dup_heads(w, n_heads):
    lead = w.shape[:-1]
    w = w.reshape(lead + (n_heads, 1, HEAD_DIM))
    return jnp.broadcast_to(w, lead + (n_heads, 2, HEAD_DIM)).reshape(lead + (n_heads * LANES,))


def _pad_heads(w, n_heads, fill):
    lead = w.shape[:-1]
    w = w.reshape(lead + (n_heads, HEAD_DIM))
    pad = jnp.full(lead + (n_heads, HEAD_DIM), fill, w.dtype)
    return jnp.concatenate([w, pad], axis=-1).reshape(lead + (n_heads * LANES,))


def _qkv_project(x, mod, w, b, *, tables=None, norms=None, tm):
    bsz, seq, d = x.shape
    n_kv = (w.shape[1] - d) // (2 * HEAD_DIM)
    nq, nk = d, n_kv * LANES
    wq, wk, wv = w[:, :d], w[:, d:d + n_kv * HEAD_DIM], w[:, d + n_kv * HEAD_DIM:]
    w_ext = jnp.concatenate([wq, _dup_heads(wk, n_kv), _pad_heads(wv, n_kv, 0.0)], axis=1).astype(BF16)
    bq, bk, bv = b[:d], b[d:d + n_kv * HEAD_DIM], b[d + n_kv * HEAD_DIM:]
    b_ext = jnp.concatenate([bq, _dup_heads(bk, n_kv), _pad_heads(bv, n_kv, 1.0)])[None, :].astype(F32)
    n = nq + 2 * nk
    args = [x, mod, w_ext, b_ext]
    specs = [pl.BlockSpec((1, tm, d), lambda bi, i: (bi, i, 0)),
             pl.BlockSpec((1, MOD_ROWS, d), lambda bi, i: (bi, 0, 0)),
             pl.BlockSpec((d, n), lambda bi, i: (0, 0)),
             pl.BlockSpec((1, n), lambda bi, i: (0, 0))]
    if tables is not None:
        args += list(tables)
        specs += [pl.BlockSpec((tm, LANES), lambda bi, i: (i, 0))] * 2
    if norms is not None:
        qn, kn = norms
        col = jnp.arange(MXU_COLS) // HEAD_DIM
        seg = (col[:, None] == col[None, :]).astype(BF16)
        reps = MXU_COLS // HEAD_DIM
        args += [seg, jnp.tile(qn, reps)[None, :].astype(F32), jnp.tile(kn, reps)[None, :].astype(F32)]
        specs += [pl.BlockSpec((MXU_COLS, MXU_COLS), lambda bi, i: (0, 0)),
                  pl.BlockSpec((1, MXU_COLS), lambda bi, i: (0, 0)),
                  pl.BlockSpec((1, MXU_COLS), lambda bi, i: (0, 0))]
    return pl.pallas_call(
        functools.partial(_qkv_kernel, nq=nq, nk=nk, rope=tables is not None, qk_norm=norms is not None),
        grid=(bsz, seq // tm),
        in_specs=specs,
        out_specs=[pl.BlockSpec((1, tm, nq), lambda bi, i: (bi, i, 0)),
                   pl.BlockSpec((1, tm, nk), lambda bi, i: (bi, i, 0)),
                   pl.BlockSpec((1, tm, nk), lambda bi, i: (bi, i, 0))],
        out_shape=[jax.ShapeDtypeStruct((bsz, seq, nq), BF16),
                   jax.ShapeDtypeStruct((bsz, seq, nk), BF16),
                   jax.ShapeDtypeStruct((bsz, seq, nk), BF16)],
        compiler_params=_cparams(("parallel", "parallel")),
        name="qkv_project",
    )(*args)


def _conv_in_kernel(x_ref, mod_ref, w_ref, bg_ref, u_ref, *, d):
    m = mod_ref[0]
    h = (x_ref[0] * (1.0 + m[1:2]) + m[0:1]).astype(BF16)
    for c in range(d // MXU_COLS):
        lo = c * MXU_COLS
        bg = jnp.dot(h, w_ref[:, lo:lo + MXU_COLS], preferred_element_type=F32)
        cg = jnp.dot(h, w_ref[:, d + lo:d + lo + MXU_COLS], preferred_element_type=F32)
        xv = jnp.dot(h, w_ref[:, 2 * d + lo:2 * d + lo + MXU_COLS], preferred_element_type=F32)
        bg_ref[0, :, lo:lo + MXU_COLS] = bg.astype(BF16)
        u_ref[0, :, lo:lo + MXU_COLS] = (cg * xv).astype(BF16)


def _conv_in_project(x, mod, w_in, *, tm):
    bsz, seq, d = x.shape
    return pl.pallas_call(
        functools.partial(_conv_in_kernel, d=d),
        grid=(bsz, seq // tm),
        in_specs=[pl.BlockSpec((1, tm, d), lambda bi, i: (bi, i, 0)),
                  pl.BlockSpec((1, MOD_ROWS, d), lambda bi, i: (bi, 0, 0)),
                  pl.BlockSpec((d, 3 * d), lambda bi, i: (0, 0))],
        out_specs=[pl.BlockSpec((1, tm, d), lambda bi, i: (bi, i, 0))] * 2,
        out_shape=[jax.ShapeDtypeStruct((bsz, seq, d), BF16)] * 2,
        compiler_params=_cparams(("parallel", "parallel")),
        name="conv_in_project",
    )(x, mod, w_in.astype(BF16))


def _qk(q, k):
    return lax.dot_general(q, k, (((1,), (1,)), ((), ())), preferred_element_type=F32)


def _scores(qg, kc, bias):
    s = _qk(qg, kc)
    return s if bias is None else s + bias


def _row_max(qg, chunks, sink):
    mx = None
    for kc, _, bias in chunks:
        s = _scores(qg, kc, bias)
        for j in range(s.shape[1] // LANES):
            part = s[:, j * LANES:(j + 1) * LANES]
            mx = part if mx is None else jnp.maximum(mx, part)
    m = jnp.max(mx, axis=1, keepdims=True)
    return m if sink is None else jnp.maximum(m, sink)


def _exp_pv(qg, chunks, m):
    acc = None
    for kc, vc, bias in chunks:
        p = jnp.exp(_scores(qg, kc, bias) - m).astype(BF16)
        d = jnp.dot(p, vc, preferred_element_type=F32)
        acc = d if acc is None else acc + d
    return acc


def _attend_group(q_ref, o_ref, col0, chunks, sinks, lane):
    heads = []
    for pair in range(GROUP // 2):
        qp = q_ref[0, :, col0 + pair * LANES:col0 + (pair + 1) * LANES]
        heads.append(jnp.where(lane < HEAD_DIM, qp, jnp.zeros_like(qp)))
        heads.append(jnp.where(lane >= HEAD_DIM, qp, jnp.zeros_like(qp)))
    ms = [_row_max(qg, chunks, sink) for qg, sink in zip(heads, sinks)]
    accs = [_exp_pv(qg, chunks, m) for qg, m in zip(heads, ms)]
    for pair in range(GROUP // 2):
        a0, a1 = accs[2 * pair], accs[2 * pair + 1]
        r0, r1 = pltpu.roll(a0, HEAD_DIM, 1), pltpu.roll(a1, HEAD_DIM, 1)
        d0, d1 = r0, a1
        if sinks[0] is not None:
            d0 = d0 + jnp.exp(sinks[2 * pair] - ms[2 * pair])
            d1 = d1 + jnp.exp(sinks[2 * pair + 1] - ms[2 * pair + 1])
        o = jnp.where(lane < HEAD_DIM, a0 / d0, r1 / d1)
        o_ref[0, :, col0 + pair * LANES:col0 + (pair + 1) * LANES] = o.astype(o_ref.dtype)


def _win_attn_kernel(q_ref, kp_ref, kc_ref, kn_ref, kz_ref, vp_ref, vc_ref, vn_ref, vz_ref, sink_ref, o_ref,
                     *, tq, seq):
    i = pl.program_id(1)
    r = lax.broadcasted_iota(jnp.int32, (tq, WINDOW), 0)
    c = lax.broadcasted_iota(jnp.int32, (tq, WINDOW), 1)
    bias_prev = jnp.where((c >= r) & (i > 0), 0.0, NEG_INF)
    bias_next = jnp.where((r - c >= tq - WINDOW) & (i < seq // tq - 1), 0.0, NEG_INF)
    bias_cur = None
    if tq - 1 > WINDOW:
        rr = lax.broadcasted_iota(jnp.int32, (tq, tq), 0)
        cc = lax.broadcasted_iota(jnp.int32, (tq, tq), 1)
        bias_cur = jnp.where(jnp.abs(cc - rr) <= WINDOW, 0.0, NEG_INF)
    lane = lax.broadcasted_iota(jnp.int32, (tq, LANES), 1)
    for hk in range(kc_ref.shape[2] // LANES):
        lo = hk * LANES
        chunks = [(kp_ref[0, :, lo:lo + LANES], vp_ref[0, :, lo:lo + LANES], bias_prev),
                  (kc_ref[0, :, lo:lo + LANES], vc_ref[0, :, lo:lo + LANES], bias_cur),
                  (kn_ref[0, :, lo:lo + LANES], vn_ref[0, :, lo:lo + LANES], bias_next),
                  (kz_ref[0, :, lo:lo + LANES], vz_ref[0, :, lo:lo + LANES], None)]
        sinks = [sink_ref[hk, g] for g in range(GROUP)]
        _attend_group(q_ref, o_ref, hk * GROUP * HEAD_DIM, chunks, sinks, lane)


def _window_attention(q, k, v, kz, vz, sink, *, tq):
    bsz, seq, d = q.shape
    nk = k.shape[2]
    ctx = kz.shape[1]
    r = tq // WINDOW
    last = seq // WINDOW - 1
    cur = lambda bi, i: (bi, i, 0)
    prev = lambda bi, i: (bi, jnp.maximum(i * r - 1, 0), 0)
    nxt = lambda bi, i: (bi, jnp.minimum((i + 1) * r, last), 0)
    zmap = lambda bi, i: (bi, 0, 0)
    kv_specs = [pl.BlockSpec((1, WINDOW, nk), prev), pl.BlockSpec((1, tq, nk), cur),
                pl.BlockSpec((1, WINDOW, nk), nxt), pl.BlockSpec((1, ctx, nk), zmap)]
    return pl.pallas_call(
        functools.partial(_win_attn_kernel, tq=tq, seq=seq),
        grid=(bsz, seq // tq),
        in_specs=[pl.BlockSpec((1, tq, d), cur)] + kv_specs + kv_specs
                 + [pl.BlockSpec(memory_space=pltpu.SMEM)],
        out_specs=pl.BlockSpec((1, tq, d), cur),
        out_shape=jax.ShapeDtypeStruct((bsz, seq, d), BF16),
        compiler_params=_cparams(("parallel", "parallel")),
        name="window_attention",
    )(q, k, k, k, kz, v, v, v, vz, sink.reshape(nk // LANES, GROUP).astype(F32))


def _dense_attn_kernel(*refs, ck, has_sink):
    if has_sink:
        q_ref, k_ref, v_ref, sink_ref, o_ref = refs
    else:
        q_ref, k_ref, v_ref, o_ref = refs
    hk = pl.program_id(1)
    tq = q_ref.shape[1]
    chunks = [(k_ref[0, lo:lo + ck, :], v_ref[0, lo:lo + ck, :], None) for lo in range(0, k_ref.shape[1], ck)]
    lane = lax.broadcasted_iota(jnp.int32, (tq, LANES), 1)
    sinks = [sink_ref[hk, g] if has_sink else None for g in range(GROUP)]
    _attend_group(q_ref, o_ref, 0, chunks, sinks, lane)


def _dense_attention(q, k, v, sink=None, *, tq):
    bsz, seq, d = q.shape
    n_keys = k.shape[1]
    n_kv = k.shape[2] // LANES
    ck = 256 if n_keys % 256 == 0 else LANES
    qmap = lambda bi, h, i: (bi, i, h)
    kmap = lambda bi, h, i: (bi, 0, h)
    args = [q, k, v]
    specs = [pl.BlockSpec((1, tq, GROUP * HEAD_DIM), qmap),
             pl.BlockSpec((1, n_keys, LANES), kmap), pl.BlockSpec((1, n_keys, LANES), kmap)]
    if sink is not None:
        args.append(sink.reshape(n_kv, GROUP).astype(F32))
        specs.append(pl.BlockSpec(memory_space=pltpu.SMEM))
    return pl.pallas_call(
        functools.partial(_dense_attn_kernel, ck=ck, has_sink=sink is not None),
        grid=(bsz, n_kv, seq // tq),
        in_specs=specs,
        out_specs=pl.BlockSpec((1, tq, GROUP * HEAD_DIM), qmap),
        out_shape=jax.ShapeDtypeStruct((bsz, seq, d), BF16),
        compiler_params=_cparams(("parallel", "parallel", "parallel")),
        name="dense_attention",
    )(*args)


def _store_token_tiles(ref, val):
    rows, d = val.shape
    rt = d // LANES
    for s in range(rt):
        ref[pl.ds(s, rows, stride=rt), :] = val[:, s * LANES:(s + 1) * LANES]


def _load_token_tiles(ref, rows, rt):
    return jnp.concatenate([ref[pl.ds(s, rows, stride=rt), :] for s in range(rt)], axis=1)


def _layer_norm(r, g, b):
    mu = jnp.mean(r, axis=-1, keepdims=True)
    rc = r - mu
    var = jnp.mean(rc * rc, axis=-1, keepdims=True)
    return rc * lax.rsqrt(var + LN_EPS) * g + b


def _top_k_route(logits, gate_ref, idx_ref, rank_ref, cnt_ref):
    tm, n_exp = logits.shape
    lane_e = lax.broadcasted_iota(jnp.int32, (tm, n_exp), 1).astype(F32)
    work = logits
    sels, vals, idxs = [], [], []
    for _ in range(TOP_K):
        mk = jnp.max(work, axis=1, keepdims=True)
        ik = jnp.min(jnp.where(work == mk, lane_e, float(n_exp)), axis=1, keepdims=True)
        sel = lane_e == ik
        work = jnp.where(sel, -jnp.inf, work)
        sels.append(sel)
        vals.append(mk)
        idxs.append(ik)
    exps = [jnp.exp(v - vals[0]) for v in vals]
    denom = exps[0]
    for e in exps[1:]:
        denom = denom + e
    routed = sels[0].astype(F32)
    for sel in sels[1:]:
        routed = routed + sel.astype(F32)
    row = lax.broadcasted_iota(jnp.int32, (tm, tm), 0)
    col = lax.broadcasted_iota(jnp.int32, (tm, tm), 1)
    earlier = jnp.where(col < row, 1.0, 0.0).astype(BF16)
    before = jnp.dot(earlier, routed.astype(BF16), preferred_element_type=F32)
    lane_k = lax.broadcasted_iota(jnp.int32, (tm, TOP_K), 1)
    gates = jnp.zeros((tm, TOP_K), F32)
    idx = jnp.zeros((tm, TOP_K), F32)
    rank = jnp.zeros((tm, TOP_K), F32)
    for kk in range(TOP_K):
        here = lane_k == kk
        gates = jnp.where(here, exps[kk] / denom, gates)
        idx = jnp.where(here, idxs[kk], idx)
        rank = jnp.where(here, jnp.sum(jnp.where(sels[kk], before, 0.0), axis=1, keepdims=True), rank)
    gate_ref[0] = gates
    idx_ref[0] = idx.astype(jnp.int32)
    rank_ref[0] = rank.astype(jnp.int32)
    cnt_ref[0] = jnp.sum(routed, axis=0, keepdims=True).astype(jnp.int32)


def _mixer_out_kernel(*refs, conv, alpha, seq):
    if conv:
        bg_ref, u_ref, up_ref, un_ref, cw_ref = refs[:5]
        refs = refs[5:]
    else:
        o_ref = refs[0]
        refs = refs[1:]
    (w_ref, x_ref, mod_ref, lng_ref, lnb_ref, rwh_ref, rwl_ref, rb_ref,
     xo_ref, h_ref, gate_ref, idx_ref, rank_ref, cnt_ref) = refs
    if conv:
        i = pl.program_id(1)
        u = u_ref[0].astype(F32)
        tm = u.shape[0]
        row = lax.broadcasted_iota(jnp.int32, u.shape, 0)
        halo = up_ref.shape[1]
        before = jnp.where(i == 0, 0.0, up_ref[0, halo - 1:halo, :].astype(F32))
        after = jnp.where(i == seq // tm - 1, 0.0, un_ref[0, 0:1, :].astype(F32))
        u_prev = jnp.where(row == 0, before, pltpu.roll(u, 1, 0))
        u_next = jnp.where(row == tm - 1, after, pltpu.roll(u, tm - 1, 0))
        cw = cw_ref[...]
        y = cw[0:1] * u_prev + cw[1:2] * u + cw[2:3] * u_next
        mixed = (bg_ref[0].astype(F32) * y).astype(BF16)
    else:
        mixed = o_ref[0]
    m = mod_ref[0]
    ox = jnp.dot(mixed, w_ref[...], preferred_element_type=F32)
    xn = _layer_norm(alpha * x_ref[0] + m[2:3] * ox, lng_ref[...], lnb_ref[...])
    xo_ref[0] = xn
    h2 = xn * (1.0 + m[4:5]) + m[3:4]
    hh = h2.astype(BF16)
    hl = (h2 - hh.astype(F32)).astype(BF16)
    _store_token_tiles(h_ref, h2)
    rwh = rwh_ref[...]
    logits = (jnp.dot(hh, rwh, preferred_element_type=F32) + jnp.dot(hl, rwh, preferred_element_type=F32)
              + jnp.dot(hh, rwl_ref[...], preferred_element_type=F32) + rb_ref[...])
    _top_k_route(logits, gate_ref, idx_ref, rank_ref, cnt_ref)


def _mixer_out(mixed, w_out, x, mod, ln_g, ln_b, router_w, router_b, *, alpha, tm, conv_w=None):
    bsz, seq, d = x.shape
    n_exp = router_w.shape[1]
    nt = seq // tm
    tile = pl.BlockSpec((1, tm, d), lambda bi, i: (bi, i, 0))
    ktile = pl.BlockSpec((1, tm, TOP_K), lambda bi, i: (bi, i, 0))
    row = pl.BlockSpec((1, d), lambda bi, i: (0, 0))
    conv = conv_w is not None
    if conv:
        bg, u = mixed
        halo = 16
        r = tm // halo
        last = seq // halo - 1
        args = [bg, u, u, u, jnp.pad(conv_w.astype(F32), ((0, MOD_ROWS - conv_w.shape[0]), (0, 0)))]
        specs = [tile, tile,
                 pl.BlockSpec((1, halo, d), lambda bi, i: (bi, jnp.maximum(i * r - 1, 0), 0)),
                 pl.BlockSpec((1, halo, d), lambda bi, i: (bi, jnp.minimum((i + 1) * r, last), 0)),
                 pl.BlockSpec((MOD_ROWS, d), lambda bi, i: (0, 0))]
    else:
        args, specs = [mixed], [tile]
    rw_hi = router_w.astype(BF16)
    rw_lo = (router_w - rw_hi.astype(F32)).astype(BF16)
    args += [w_out.astype(BF16), x, mod, ln_g[None, :], ln_b[None, :], rw_hi, rw_lo, router_b[None, :]]
    specs += [pl.BlockSpec((d, d), lambda bi, i: (0, 0)), tile,
              pl.BlockSpec((1, MOD_ROWS, d), lambda bi, i: (bi, 0, 0)), row, row,
              pl.BlockSpec((d, n_exp), lambda bi, i: (0, 0)), pl.BlockSpec((d, n_exp), lambda bi, i: (0, 0)),
              pl.BlockSpec((1, n_exp), lambda bi, i: (0, 0))]
    return pl.pallas_call(
        functools.partial(_mixer_out_kernel, conv=conv, alpha=alpha, seq=seq),
        grid=(bsz, nt),
        in_specs=specs,
        out_specs=[tile, pl.BlockSpec((tm * (d // LANES), LANES), lambda bi, i: (bi * nt + i, 0)),
                   ktile, ktile, ktile,
                   pl.BlockSpec((1, 1, n_exp), lambda bi, i: (bi * nt + i, 0, 0))],
        out_shape=[jax.ShapeDtypeStruct((bsz, seq, d), F32),
                   jax.ShapeDtypeStruct((bsz * seq * (d // LANES), LANES), F32),
                   jax.ShapeDtypeStruct((bsz, seq, TOP_K), F32), jax.ShapeDtypeStruct((bsz, seq, TOP_K), jnp.int32),
                   jax.ShapeDtypeStruct((bsz, seq, TOP_K), jnp.int32),
                   jax.ShapeDtypeStruct((bsz * nt, 1, n_exp), jnp.int32)],
        compiler_params=_cparams(("parallel", "parallel")),
        name="mixer_out",
    )(*args)


def _expert_kernel(be_ref, first_ref, used_ref, x_ref, wgu_ref, bgu_ref, wdn_ref, bdn_ref, o_ref,
                   wgu_bf, wdn_bf, *, ff, fc):
    i = pl.program_id(0)

    @pl.when(first_ref[i] == 1)
    def _():
        wgu_bf[...] = wgu_ref[0, 0].astype(BF16)
        wdn_bf[...] = wdn_ref[0, 0].astype(BF16)

    @pl.when(i < used_ref[0])
    def _():
        rt = wdn_bf.shape[1] // LANES
        bm = x_ref.shape[0] // rt
        xb = _load_token_tiles(x_ref, bm, rt).astype(BF16)
        acc = jnp.zeros((bm, rt * LANES), F32)
        for c in range(ff // fc):
            lo = c * fc
            gate = jnp.dot(xb, wgu_bf[:, lo:lo + fc], preferred_element_type=F32) + bgu_ref[0, 0, :, lo:lo + fc]
            up = jnp.dot(xb, wgu_bf[:, ff + lo:ff + lo + fc], preferred_element_type=F32) \
                + bgu_ref[0, 0, :, ff + lo:ff + lo + fc]
            gate = jnp.minimum(gate, SWIGLU_LIMIT)
            up = jnp.clip(up, -SWIGLU_LIMIT, SWIGLU_LIMIT)
            act = (up + 1.0) * gate * jax.nn.sigmoid(SWIGLU_ALPHA * gate)
            acc = acc + jnp.dot(act.astype(BF16), wdn_bf[lo:lo + fc, :], preferred_element_type=F32)
        _store_token_tiles(o_ref, acc + bdn_ref[0, 0])

    @pl.when(i >= used_ref[0])
    def _():
        o_ref[...] = jnp.zeros_like(o_ref)


def _expert_ffn(xs, blk_expert, blk_first, n_used, layer, w_gu, b_gu, w_down, b_down, *, bm):
    depth, n_exp, d, ff2 = w_gu.shape
    ff = ff2 // 2
    rt = d // LANES
    wmap = lambda i, be, fi, nu: (layer, be[i], 0, 0)
    grid_spec = pltpu.PrefetchScalarGridSpec(
        num_scalar_prefetch=3,
        grid=(xs.shape[0] // (bm * rt),),
        in_specs=[pl.BlockSpec((bm * rt, LANES), lambda i, be, fi, nu: (jnp.minimum(i, nu[0] - 1), 0)),
                  pl.BlockSpec((1, 1, d, ff2), wmap), pl.BlockSpec((1, 1, 1, ff2), wmap),
                  pl.BlockSpec((1, 1, ff, d), wmap), pl.BlockSpec((1, 1, 1, d), wmap)],
        out_specs=pl.BlockSpec((bm * rt, LANES), lambda i, be, fi, nu: (i, 0)),
        scratch_shapes=[pltpu.VMEM((d, ff2), BF16), pltpu.VMEM((ff, d), BF16)],
    )
    return pl.pallas_call(
        functools.partial(_expert_kernel, ff=ff, fc=min(ff, 512)),
        grid_spec=grid_spec,
        out_shape=jax.ShapeDtypeStruct(xs.shape, F32),
        compiler_params=_cparams(("arbitrary",)),
        name="expert_ffn",
    )(blk_expert, blk_first, n_used, xs, w_gu, b_gu.reshape(depth, n_exp, 1, ff2), w_down,
      b_down.reshape(depth, n_exp, 1, d))


def _route(idx, rank, cnt_tiles, tile_rows, bm):
    t = idx.shape[0]
    a = t * TOP_K
    n_exp = cnt_tiles.shape[1]
    counts = jnp.sum(cnt_tiles, axis=0)
    padded = (counts + bm - 1) // bm * bm
    pends = jnp.cumsum(padded)
    pstarts = pends - padded
    tile_off = pstarts[None, :] + jnp.cumsum(cnt_tiles, axis=0) - cnt_tiles
    off_tok, lo = [], 0
    for rows, n_tiles in tile_rows:
        part = tile_off[lo:lo + n_tiles]
        off_tok.append(jnp.broadcast_to(part[:, None, :], (n_tiles, rows, n_exp)).reshape(n_tiles * rows, n_exp))
        lo += n_tiles
    off_tok = jnp.concatenate(off_tok, axis=0)
    chosen = idx[:, :, None] == jnp.arange(n_exp, dtype=jnp.int32)[None, None, :]
    dest = jnp.sum(jnp.where(chosen, off_tok[:, None, :], 0), axis=-1) + rank
    nblk = -(-a // bm) + n_exp
    blk_start = jnp.arange(nblk, dtype=jnp.int32) * bm
    blk_expert = jnp.minimum(jnp.sum((blk_start[:, None] >= pends[None, :]).astype(jnp.int32), axis=1),
                             n_exp - 1)
    blk_first = jnp.concatenate([jnp.ones((1,), jnp.int32),
                                 (blk_expert[1:] != blk_expert[:-1]).astype(jnp.int32)])
    n_used = (pends[-1] // bm).astype(jnp.int32).reshape(1)
    last_blk = jnp.maximum(pends // bm - 1, 0).astype(jnp.int32)
    return dest, nblk, blk_expert, blk_first, n_used, last_blk


def _dispatch_kernel(*refs, rt, n_exp, nblk, tiles):
    n_s = len(tiles)
    lb_ref, nu_ref = refs[:2]
    dest_refs = refs[2:2 + n_s]
    h_refs = refs[2 + n_s:2 + 2 * n_s]
    xs_ref, zero_buf, sem, zero_sem = refs[2 + 2 * n_s:]
    i = pl.program_id(0)
    blk_rows = zero_buf.shape[0]

    def zero_copy(blk):
        return pltpu.make_async_copy(
            zero_buf, xs_ref.at[pl.ds(pl.multiple_of(blk * blk_rows, blk_rows), blk_rows), :], zero_sem)

    @pl.when(i == 0)
    def _():
        zero_buf[...] = jnp.zeros_like(zero_buf)
        todo = [(lb_ref[0], None)]
        todo += [(lb_ref[e], lb_ref[e] != lb_ref[e - 1]) for e in range(1, n_exp)]
        todo += [(nu_ref[0] + e, nu_ref[0] + e < nblk) for e in range(n_exp)]
        for blk, cond in todo:
            if cond is None:
                zero_copy(blk).start()
            else:
                pl.when(cond)(lambda blk=blk: zero_copy(blk).start())
        for _, cond in todo:
            if cond is None:
                zero_copy(0).wait()
            else:
                pl.when(cond)(lambda: zero_copy(0).wait())

    first = 0
    for dest_ref, h_ref, (tm, nt) in zip(dest_refs, h_refs, tiles):
        @pl.when((i >= first) & (i < first + nt))
        def _(dest_ref=dest_ref, h_ref=h_ref, tm=tm):
            def issue(j, carry):
                base = j * SUBLANES
                for s in range(SUBLANES):
                    src = h_ref.at[pl.ds(pl.multiple_of((base + s) * rt, rt), rt), :]
                    for kk in range(TOP_K):
                        dst = pl.multiple_of(dest_ref[0, 0, (base + s) * TOP_K + kk], rt)
                        pltpu.make_async_copy(src, xs_ref.at[pl.ds(dst, rt), :], sem).start(priority=kk % 2)
                return carry

            lax.fori_loop(0, tm // SUBLANES, issue, 0)
            for kk in range(TOP_K):
                pltpu.make_async_copy(h_ref, xs_ref.at[pl.ds(0, tm * rt), :], sem).wait()

        first += nt


def _dispatch(streams, last_blk, n_used, nblk, *, bm, rt):
    n_exp = last_blk.shape[0]
    tiles = [(tm, h.shape[0] // (tm * rt)) for h, _, tm in streams]
    dest_specs, h_specs, dests, hs = [], [], [], []
    first = 0
    for (h, dest_rows, tm), (_, nt) in zip(streams, tiles):
        tile_of = lambda i, lb, nu, first=first, nt=nt: jnp.clip(i - first, 0, nt - 1)
        dest_specs.append(pl.BlockSpec((1, 1, tm * TOP_K), lambda i, lb, nu, f=tile_of: (f(i, lb, nu), 0, 0),
                                       memory_space=pltpu.SMEM))
        h_specs.append(pl.BlockSpec((tm * rt, LANES), lambda i, lb, nu, f=tile_of: (f(i, lb, nu), 0)))
        dests.append(dest_rows.reshape(nt, 1, tm * TOP_K))
        hs.append(h)
        first += nt
    return pl.pallas_call(
        functools.partial(_dispatch_kernel, rt=rt, n_exp=n_exp, nblk=nblk, tiles=tiles),
        grid_spec=pltpu.PrefetchScalarGridSpec(
            num_scalar_prefetch=2, grid=(first,), in_specs=dest_specs + h_specs,
            out_specs=pl.BlockSpec(memory_space=pl.ANY),
            scratch_shapes=[pltpu.VMEM((bm * rt, LANES), F32), pltpu.SemaphoreType.DMA(()),
                            pltpu.SemaphoreType.DMA(())]),
        out_shape=jax.ShapeDtypeStruct((nblk * bm * rt, LANES), F32),
        compiler_params=_cparams(("arbitrary",)),
        name="dispatch_rows",
    )(last_blk, n_used, *dests, *hs)


def _combine_kernel(dest_ref, next_dest_ref, out_ref, gate_ref, x_ref, mod_ref, lng_ref, lnb_ref, o_ref,
                    y_buf, sem, *, alpha, n_tiles):
    n = pl.program_id(0)
    tm, d = x_ref.shape[1:]
    rt = d // LANES

    def request(table_ref, slot):
        def issue(j, carry):
            base = j * SUBLANES
            for s in range(SUBLANES):
                row = pl.multiple_of((base + s) * rt, rt)
                for kk in range(TOP_K):
                    src = pl.multiple_of(table_ref[0, 0, (base + s) * TOP_K + kk], rt)
                    pltpu.make_async_copy(out_ref.at[pl.ds(src, rt), :], y_buf.at[slot, kk, pl.ds(row, rt), :],
                                          sem.at[slot]).start(priority=kk % 2)
            return carry

        lax.fori_loop(0, tm // SUBLANES, issue, 0)

    slot = n % 2

    @pl.when(n == 0)
    def _():
        request(dest_ref, 0)

    @pl.when(n + 1 < n_tiles)
    def _():
        request(next_dest_ref, 1 - slot)

    for kk in range(TOP_K):
        pltpu.make_async_copy(out_ref.at[pl.ds(0, tm * rt), :], y_buf.at[slot, kk], sem.at[slot]).wait()
    g = gate_ref[0]
    fx = g[:, 0:1] * _load_token_tiles(y_buf.at[slot, 0], tm, rt)
    for kk in range(1, TOP_K):
        fx = fx + g[:, kk:kk + 1] * _load_token_tiles(y_buf.at[slot, kk], tm, rt)
    m = mod_ref[0]
    o_ref[0] = _layer_norm(alpha * x_ref[0] + m[5:6] * fx, lng_ref[...], lnb_ref[...])


def _combine(out, dest_rows, gates, x, mod, ln_g, ln_b, *, alpha, tm):
    bsz, seq, d = x.shape
    nt = seq // tm
    n_tiles = bsz * nt
    dest = dest_rows.reshape(n_tiles, 1, tm * TOP_K)
    tile = pl.BlockSpec((1, tm, d), lambda n: (n // nt, n % nt, 0))
    row = pl.BlockSpec((1, d), lambda n: (0, 0))
    return pl.pallas_call(
        functools.partial(_combine_kernel, alpha=alpha, n_tiles=n_tiles),
        grid=(n_tiles,),
        in_specs=[pl.BlockSpec((1, 1, tm * TOP_K), lambda n: (n, 0, 0), memory_space=pltpu.SMEM),
                  pl.BlockSpec((1, 1, tm * TOP_K), lambda n: (jnp.minimum(n + 1, n_tiles - 1), 0, 0),
                               memory_space=pltpu.SMEM),
                  pl.BlockSpec(memory_space=pl.ANY),
                  pl.BlockSpec((1, tm, TOP_K), lambda n: (n // nt, n % nt, 0)),
                  tile, pl.BlockSpec((1, MOD_ROWS, d), lambda n: (n // nt, 0, 0)), row, row],
        out_specs=tile,
        out_shape=jax.ShapeDtypeStruct((bsz, seq, d), F32),
        scratch_shapes=[pltpu.VMEM((2, TOP_K, tm * (d // LANES), LANES), F32), pltpu.SemaphoreType.DMA((2,))],
        compiler_params=_cparams(("arbitrary",)),
        name="combine_norm",
    )(dest, dest, out, gates, x, mod, ln_g[None, :], ln_b[None, :])


def _rope_tables(seq):
    t = jnp.arange(seq, dtype=jnp.int32)
    row = (t // GRID_W).astype(F32)
    col = (t % GRID_W).astype(F32)
    n_freq = HEAD_DIM // 4
    inv_freq = ROPE_THETA ** (-jnp.arange(n_freq, dtype=F32) / n_freq)
    ar, ac = row[:, None] * inv_freq, col[:, None] * inv_freq
    cos = jnp.concatenate([jnp.cos(ar), jnp.cos(ar), jnp.cos(ac), jnp.cos(ac)], axis=1)
    sin = jnp.concatenate([-jnp.sin(ar), jnp.sin(ar), -jnp.sin(ac), jnp.sin(ac)], axis=1)
    return jnp.tile(cos, (1, 2)), jnp.tile(sin, (1, 2))


def _tile(n, want):
    t = min(n, want)
    while n % t:
        t //= 2
    return t


def kernel(x, c, ctx, c_ctx, mod_w, mod_b, ln1_g, ln1_b, ln2_g, ln2_b, win_wqkv, win_bqkv, win_sink, win_wo,
           conv_win, conv_w, conv_wout, full_wqkv, full_qnorm, full_knorm, full_wo,
           router_w, router_b, expert_wgu, expert_bgu, expert_wdown, expert_bdown):
    bsz, seq, d = x.shape
    ctx_len = ctx.shape[1]
    depth = mod_w.shape[0]
    alpha = (2.0 * depth) ** 0.25
    tables = _rope_tables(seq)
    tm_x, tm_z = _tile(seq, 512), _tile(ctx_len, 512)
    tq_win = _tile(seq, 256)
    tq_full = _tile(seq, 256)
    bm = 768

    c_rows = jnp.zeros((2 * MOD_ROWS, d), F32).at[:bsz].set(c).at[bsz].set(c_ctx)
    mods = _modulation(c_rows, mod_w, mod_b)
    z = ctx
    for i in range(depth):
        kind, j = i % N_MIXERS, i // N_MIXERS
        need_ctx = i < depth - 1
        mod_x = jnp.pad(mods[i, :bsz].reshape(bsz, 6, d), ((0, 0), (0, MOD_ROWS - 6), (0, 0)))
        mod_z = jnp.broadcast_to(jnp.pad(mods[i, bsz].reshape(1, 6, d), ((0, 0), (0, MOD_ROWS - 6), (0, 0))),
                                 (bsz, MOD_ROWS, d))
        route_args = (router_w[i], router_b[i])
        ln1 = (ln1_g[i], ln1_b[i])
        oz = None
        if kind == 0:
            qx, kx, vx = _qkv_project(x, mod_x, win_wqkv[j], win_bqkv[j], tables=tables, tm=tm_x)
            qz, kz, vz = _qkv_project(z, mod_z, win_wqkv[j], win_bqkv[j], tm=tm_z)
            ox = _window_attention(qx, kx, vx, kz, vz, win_sink[j], tq=tq_win)
            if need_ctx:
                oz = _dense_attention(qz, kz, vz, win_sink[j], tq=ctx_len)
            w_out, conv_taps = win_wo[j], None
        elif kind == 1:
            ox = _conv_in_project(x, mod_x, conv_win[j], tm=tm_x)
            if need_ctx:
                oz = _conv_in_project(z, mod_z, conv_win[j], tm=tm_z)
            w_out, conv_taps = conv_wout[j], conv_w[j]
        else:
            zero_b = jnp.zeros((full_wqkv.shape[2],), F32)
            norms = (full_qnorm[j], full_knorm[j])
            qx, kx, vx = _qkv_project(x, mod_x, full_wqkv[j], zero_b, tables=tables, norms=norms, tm=tm_x)
            qz, kz, vz = _qkv_project(z, mod_z, full_wqkv[j], zero_b, norms=norms, tm=tm_z)
            k_all = jnp.concatenate([kx, kz], axis=1)
            v_all = jnp.concatenate([vx, vz], axis=1)
            ox = _dense_attention(qx, k_all, v_all, tq=tq_full)
            if need_ctx:
                oz = _dense_attention(qz, kz, vz, tq=ctx_len)
            w_out, conv_taps = full_wo[j], None

        x, hx, gx, ix, rx, cx = _mixer_out(ox, w_out, x, mod_x, *ln1, *route_args, alpha=alpha, tm=tm_x,
                                           conv_w=conv_taps)
        n_x = bsz * seq
        idx, rank, cnt = ix.reshape(n_x, TOP_K), rx.reshape(n_x, TOP_K), cx[:, 0]
        tile_rows = [(tm_x, n_x // tm_x)]
        if need_ctx:
            z, hz, gz, iz, rz, cz = _mixer_out(oz, w_out, z, mod_z, *ln1, *route_args, alpha=alpha, tm=tm_z,
                                               conv_w=conv_taps)
            n_z = bsz * ctx_len
            idx = jnp.concatenate([idx, iz.reshape(n_z, TOP_K)], axis=0)
            rank = jnp.concatenate([rank, rz.reshape(n_z, TOP_K)], axis=0)
            cnt = jnp.concatenate([cnt, cz[:, 0]], axis=0)
            tile_rows.append((tm_z, n_z // tm_z))

        dest, nblk, blk_expert, blk_first, n_used, last_blk = _route(idx, rank, cnt, tile_rows, bm)
        rt = d // LANES
        dest_rows = dest * rt
        streams = [(hx, dest_rows[:n_x], tm_x)]
        if need_ctx:
            streams.append((hz, dest_rows[n_x:], tm_z))
        xs = _dispatch(streams, last_blk, n_used, nblk, bm=bm, rt=rt)
        out = _expert_ffn(xs, blk_expert, blk_first, n_used, i,
                          expert_wgu, expert_bgu, expert_wdown, expert_bdown, bm=bm)
        x = _combine(out, dest_rows[:n_x], gx, x, mod_x, ln2_g[i], ln2_b[i], alpha=alpha, tm=tm_x)
        if need_ctx:
            z = _combine(out, dest_rows[n_x:], gz, z, mod_z, ln2_g[i], ln2_b[i], alpha=alpha, tm=tm_z)
    return x
```

```python
import functools

import jax
import jax.numpy as jnp
from jax import lax
from jax.experimental import pallas as pl
from jax.experimental.pallas import tpu as pltpu

HEAD_DIM = 64
GROUP = 4
GRID_W = 64
WINDOW = 128
ROPE_THETA = 10000.0
TOP_K = 4
N_MIXERS = 3
SWIGLU_ALPHA = 1.702
SWIGLU_LIMIT = 7.0
LN_EPS = 1e-5
RMS_EPS = 1e-6
NEG_INF = -1e30

LANES = 128
SUBLANES = 8
MXU_COLS = 256
MOD_ROWS = 8
VMEM_LIMIT = 56 * 1024 * 1024

F32 = jnp.float32
BF16 = jnp.bfloat16


def _cparams(sem):
    return pltpu.CompilerParams(dimension_semantics=sem, vmem_limit_bytes=VMEM_LIMIT)


def _mod_kernel(c_ref, w_ref, b_ref, o_ref):
    c = c_ref[...]
    s = c * jax.nn.sigmoid(c)
    o_ref[0] = jnp.dot(s, w_ref[0], preferred_element_type=F32, precision=lax.Precision.HIGHEST) + b_ref[0]


def _modulation(c_rows, mod_w, mod_b):
    depth, d, n = mod_w.shape
    r = c_rows.shape[0]
    tn = min(n, 1536)
    return pl.pallas_call(
        _mod_kernel,
        grid=(depth, n // tn),
        in_specs=[pl.BlockSpec((r, d), lambda l, j: (0, 0)),
                  pl.BlockSpec((1, d, tn), lambda l, j: (l, 0, j)),
                  pl.BlockSpec((1, 1, tn), lambda l, j: (l, 0, j))],
        out_specs=pl.BlockSpec((1, r, tn), lambda l, j: (l, 0, j)),
        out_shape=jax.ShapeDtypeStruct((depth, r, n), F32),
        compiler_params=_cparams(("arbitrary", "arbitrary")),
        name="modulation",
    )(c_rows, mod_w, mod_b.reshape(depth, 1, n))


def _swap_pairs(t):
    lane = lax.broadcasted_iota(jnp.int32, t.shape, 1)
    return jnp.where(lane % 32 < 16, pltpu.roll(t, LANES - 16, 1), pltpu.roll(t, 16, 1))


def _head_mean_sq(t, seg_ref):
    t2 = t * t
    hi = t2.astype(BF16)
    lo = (t2 - hi.astype(F32)).astype(BF16)
    seg = seg_ref[...]
    s = jnp.dot(hi, seg, preferred_element_type=F32) + jnp.dot(lo, seg, preferred_element_type=F32)
    return s * (1.0 / HEAD_DIM)


def _qkv_kernel(*refs, nq, nk, rope, qk_norm):
    x_ref, mod_ref, w_ref, b_ref = refs[:4]
    pos = 4
    if rope:
        cos_ref, sin_ref = refs[pos:pos + 2]
        pos += 2
    if qk_norm:
        seg_ref, qg_ref, kg_ref = refs[pos:pos + 3]
        pos += 3
    q_ref, k_ref, v_ref = refs[pos:pos + 3]
    m = mod_ref[0]
    h = (x_ref[0] * (1.0 + m[1:2]) + m[0:1]).astype(BF16)
    if rope:
        cos, sin = cos_ref[...], sin_ref[...]
    for c in range((nq + 2 * nk) // MXU_COLS):
        lo = c * MXU_COLS
        wide = jnp.dot(h, w_ref[:, lo:lo + MXU_COLS], preferred_element_type=F32) + b_ref[:, lo:lo + MXU_COLS]
        if lo >= nq + nk:
            v_ref[0, :, lo - nq - nk:lo - nq - nk + MXU_COLS] = wide.astype(BF16)
            continue
        is_q = lo < nq
        if qk_norm:
            gain = qg_ref[...] if is_q else kg_ref[...]
            wide = wide * lax.rsqrt(_head_mean_sq(wide, seg_ref) + RMS_EPS) * gain
        for half in range(MXU_COLS // LANES):
            t = wide[:, half * LANES:(half + 1) * LANES]
            at = lo + half * LANES
            if rope:
                t = t * cos + _swap_pairs(t) * sin
            if is_q:
                q_ref[0, :, at:at + LANES] = (t * HEAD_DIM ** -0.5).astype(BF16)
            else:
                k_ref[0, :, at - nq:at - nq + LANES] = t.astype(BF16)


def _dup_heads(w, n_heads):
    lead = w.shape[:-1]
    w = w.reshape(lead + (n_heads, 1, HEAD_DIM))
    return jnp.broadcast_to(w, lead + (n_heads, 2, HEAD_DIM)).reshape(lead + (n_heads * LANES,))


def _pad_heads(w, n_heads, fill):
    lead = w.shape[:-1]
    w = w.reshape(lead + (n_heads, HEAD_DIM))
    pad = jnp.full(lead + (n_heads, HEAD_DIM), fill, w.dtype)
    return jnp.concatenate([w, pad], axis=-1).reshape(lead + (n_heads * LANES,))


def _qkv_project(x, mod, w, b, *, tables=None, norms=None, tm):
    bsz, seq, d = x.shape
    n_kv = (w.shape[1] - d) // (2 * HEAD_DIM)
    nq, nk = d, n_kv * LANES
    wq, wk, wv = w[:, :d], w[:, d:d + n_kv * HEAD_DIM], w[:, d + n_kv * HEAD_DIM:]
    w_ext = jnp.concatenate([wq, _dup_heads(wk, n_kv), _pad_heads(wv, n_kv, 0.0)], axis=1).astype(BF16)
    bq, bk, bv = b[:d], b[d:d + n_kv * HEAD_DIM], b[d + n_kv * HEAD_DIM:]
    b_ext = jnp.concatenate([bq, _dup_heads(bk, n_kv), _pad_heads(bv, n_kv, 1.0)])[None, :].astype(F32)
    n = nq + 2 * nk
    args = [x, mod, w_ext, b_ext]
    specs = [pl.BlockSpec((1, tm, d), lambda bi, i: (bi, i, 0)),
             pl.BlockSpec((1, MOD_ROWS, d), lambda bi, i: (bi, 0, 0)),
             pl.BlockSpec((d, n), lambda bi, i: (0, 0)),
             pl.BlockSpec((1, n), lambda bi, i: (0, 0))]
    if tables is not None:
        args += list(tables)
        specs += [pl.BlockSpec((tm, LANES), lambda bi, i: (i, 0))] * 2
    if norms is not None:
        qn, kn = norms
        col = jnp.arange(MXU_COLS) // HEAD_DIM
        seg = (col[:, None] == col[None, :]).astype(BF16)
        reps = MXU_COLS // HEAD_DIM
        args += [seg, jnp.tile(qn, reps)[None, :].astype(F32), jnp.tile(kn, reps)[None, :].astype(F32)]
        specs += [pl.BlockSpec((MXU_COLS, MXU_COLS), lambda bi, i: (0, 0)),
                  pl.BlockSpec((1, MXU_COLS), lambda bi, i: (0, 0)),
                  pl.BlockSpec((1, MXU_COLS), lambda bi, i: (0, 0))]
    return pl.pallas_call(
        functools.partial(_qkv_kernel, nq=nq, nk=nk, rope=tables is not None, qk_norm=norms is not None),
        grid=(bsz, seq // tm),
        in_specs=specs,
        out_specs=[pl.BlockSpec((1, tm, nq), lambda bi, i: (bi, i, 0)),
                   pl.BlockSpec((1, tm, nk), lambda bi, i: (bi, i, 0)),
                   pl.BlockSpec((1, tm, nk), lambda bi, i: (bi, i, 0))],
        out_shape=[jax.ShapeDtypeStruct((bsz, seq, nq), BF16),
                   jax.ShapeDtypeStruct((bsz, seq, nk), BF16),
                   jax.ShapeDtypeStruct((bsz, seq, nk), BF16)],
        compiler_params=_cparams(("parallel", "parallel")),
        name="qkv_project",
    )(*args)


def _conv_in_kernel(x_ref, mod_ref, w_ref, bg_ref, u_ref, *, d):
    m = mod_ref[0]
    h = (x_ref[0] * (1.0 + m[1:2]) + m[0:1]).astype(BF16)
    for c in range(d // MXU_COLS):
        lo = c * MXU_COLS
        bg = jnp.dot(h, w_ref[:, lo:lo + MXU_COLS], preferred_element_type=F32)
        cg = jnp.dot(h, w_ref[:, d + lo:d + lo + MXU_COLS], preferred_element_type=F32)
        xv = jnp.dot(h, w_ref[:, 2 * d + lo:2 * d + lo + MXU_COLS], preferred_element_type=F32)
        bg_ref[0, :, lo:lo + MXU_COLS] = bg.astype(BF16)
        u_ref[0, :, lo:lo + MXU_COLS] = (cg * xv).astype(BF16)


def _conv_in_project(x, mod, w_in, *, tm):
    bsz, seq, d = x.shape
    return pl.pallas_call(
        functools.partial(_conv_in_kernel, d=d),
        grid=(bsz, seq // tm),
        in_specs=[pl.BlockSpec((1, tm, d), lambda bi, i: (bi, i, 0)),
                  pl.BlockSpec((1, MOD_ROWS, d), lambda bi, i: (bi, 0, 0)),
                  pl.BlockSpec((d, 3 * d), lambda bi, i: (0, 0))],
        out_specs=[pl.BlockSpec((1, tm, d), lambda bi, i: (bi, i, 0))] * 2,
        out_shape=[jax.ShapeDtypeStruct((bsz, seq, d), BF16)] * 2,
        compiler_params=_cparams(("parallel", "parallel")),
        name="conv_in_project",
    )(x, mod, w_in.astype(BF16))


def _qk(q, k):
    return lax.dot_general(q, k, (((1,), (1,)), ((), ())), preferred_element_type=F32)


def _scores(qg, kc, bias):
    s = _qk(qg, kc)
    return s if bias is None else s + bias


def _row_max(qg, chunks, sink):
    mx = None
    for kc, _, bias in chunks:
        s = _scores(qg, kc, bias)
        for j in range(s.shape[1] // LANES):
            part = s[:, j * LANES:(j + 1) * LANES]
            mx = part if mx is None else jnp.maximum(mx, part)
    m = jnp.max(mx, axis=1, keepdims=True)
    return m if sink is None else jnp.maximum(m, sink)


def _exp_pv(qg, chunks, m):
    acc = None
    for kc, vc, bias in chunks:
        p = jnp.exp(_scores(qg, kc, bias) - m).astype(BF16)
        d = jnp.dot(p, vc, preferred_element_type=F32)
        acc = d if acc is None else acc + d
    return acc


def _attend_group(q_ref, o_ref, col0, chunks, sinks, lane):
    heads = []
    for pair in range(GROUP // 2):
        qp = q_ref[0, :, col0 + pair * LANES:col0 + (pair + 1) * LANES]
        heads.append(jnp.where(lane < HEAD_DIM, qp, jnp.zeros_like(qp)))
        heads.append(jnp.where(lane >= HEAD_DIM, qp, jnp.zeros_like(qp)))
    ms = [_row_max(qg, chunks, sink) for qg, sink in zip(heads, sinks)]
    accs = [_exp_pv(qg, chunks, m) for qg, m in zip(heads, ms)]
    for pair in range(GROUP // 2):
        a0, a1 = accs[2 * pair], accs[2 * pair + 1]
        r0, r1 = pltpu.roll(a0, HEAD_DIM, 1), pltpu.roll(a1, HEAD_DIM, 1)
        d0, d1 = r0, a1
        if sinks[0] is not None:
            d0 = d0 + jnp.exp(sinks[2 * pair] - ms[2 * pair])
            d1 = d1 + jnp.exp(sinks[2 * pair + 1] - ms[2 * pair + 1])
        o = jnp.where(lane < HEAD_DIM, a0 / d0, r1 / d1)
        o_ref[0, :, col0 + pair * LANES:col0 + (pair + 1) * LANES] = o.astype(o_ref.dtype)


def _win_attn_kernel(q_ref, kp_ref, kc_ref, kn_ref, kz_ref, vp_ref, vc_ref, vn_ref, vz_ref, sink_ref, o_ref,
                     *, tq, seq):
    i = pl.program_id(1)
    r = lax.broadcasted_iota(jnp.int32, (tq, WINDOW), 0)
    c = lax.broadcasted_iota(jnp.int32, (tq, WINDOW), 1)
    bias_prev = jnp.where((c >= r) & (i > 0), 0.0, NEG_INF)
    bias_next = jnp.where((r - c >= tq - WINDOW) & (i < seq // tq - 1), 0.0, NEG_INF)
    bias_cur = None
    if tq - 1 > WINDOW:
        rr = lax.broadcasted_iota(jnp.int32, (tq, tq), 0)
        cc = lax.broadcasted_iota(jnp.int32, (tq, tq), 1)
        bias_cur = jnp.where(jnp.abs(cc - rr) <= WINDOW, 0.0, NEG_INF)
    lane = lax.broadcasted_iota(jnp.int32, (tq, LANES), 1)
    for hk in range(kc_ref.shape[2] // LANES):
        lo = hk * LANES
        chunks = [(kp_ref[0, :, lo:lo + LANES], vp_ref[0, :, lo:lo + LANES], bias_prev),
                  (kc_ref[0, :, lo:lo + LANES], vc_ref[0, :, lo:lo + LANES], bias_cur),
                  (kn_ref[0, :, lo:lo + LANES], vn_ref[0, :, lo:lo + LANES], bias_next),
                  (kz_ref[0, :, lo:lo + LANES], vz_ref[0, :, lo:lo + LANES], None)]
        sinks = [sink_ref[hk, g] for g in range(GROUP)]
        _attend_group(q_ref, o_ref, hk * GROUP * HEAD_DIM, chunks, sinks, lane)


def _window_attention(q, k, v, kz, vz, sink, *, tq):
    bsz, seq, d = q.shape
    nk = k.shape[2]
    ctx = kz.shape[1]
    r = tq // WINDOW
    last = seq // WINDOW - 1
    cur = lambda bi, i: (bi, i, 0)
    prev = lambda bi, i: (bi, jnp.maximum(i * r - 1, 0), 0)
    nxt = lambda bi, i: (bi, jnp.minimum((i + 1) * r, last), 0)
    zmap = lambda bi, i: (bi, 0, 0)
    kv_specs = [pl.BlockSpec((1, WINDOW, nk), prev), pl.BlockSpec((1, tq, nk), cur),
                pl.BlockSpec((1, WINDOW, nk), nxt), pl.BlockSpec((1, ctx, nk), zmap)]
    return pl.pallas_call(
        functools.partial(_win_attn_kernel, tq=tq, seq=seq),
        grid=(bsz, seq // tq),
        in_specs=[pl.BlockSpec((1, tq, d), cur)] + kv_specs + kv_specs
                 + [pl.BlockSpec(memory_space=pltpu.SMEM)],
        out_specs=pl.BlockSpec((1, tq, d), cur),
        out_shape=jax.ShapeDtypeStruct((bsz, seq, d), BF16),
        compiler_params=_cparams(("parallel", "parallel")),
        name="window_attention",
    )(q, k, k, k, kz, v, v, v, vz, sink.reshape(nk // LANES, GROUP).astype(F32))


def _dense_attn_kernel(*refs, ck, has_sink):
    if has_sink:
        q_ref, k_ref, v_ref, sink_ref, o_ref = refs
    else:
        q_ref, k_ref, v_ref, o_ref = refs
    hk = pl.program_id(1)
    tq = q_ref.shape[1]
    chunks = [(k_ref[0, lo:lo + ck, :], v_ref[0, lo:lo + ck, :], None) for lo in range(0, k_ref.shape[1], ck)]
    lane = lax.broadcasted_iota(jnp.int32, (tq, LANES), 1)
    sinks = [sink_ref[hk, g] if has_sink else None for g in range(GROUP)]
    _attend_group(q_ref, o_ref, 0, chunks, sinks, lane)


def _dense_attention(q, k, v, sink=None, *, tq):
    bsz, seq, d = q.shape
    n_keys = k.shape[1]
    n_kv = k.shape[2] // LANES
    ck = 256 if n_keys % 256 == 0 else LANES
    qmap = lambda bi, h, i: (bi, i, h)
    kmap = lambda bi, h, i: (bi, 0, h)
    args = [q, k, v]
    specs = [pl.BlockSpec((1, tq, GROUP * HEAD_DIM), qmap),
             pl.BlockSpec((1, n_keys, LANES), kmap), pl.BlockSpec((1, n_keys, LANES), kmap)]
    if sink is not None:
        args.append(sink.reshape(n_kv, GROUP).astype(F32))
        specs.append(pl.BlockSpec(memory_space=pltpu.SMEM))
    return pl.pallas_call(
        functools.partial(_dense_attn_kernel, ck=ck, has_sink=sink is not None),
        grid=(bsz, n_kv, seq // tq),
        in_specs=specs,
        out_specs=pl.BlockSpec((1, tq, GROUP * HEAD_DIM), qmap),
        out_shape=jax.ShapeDtypeStruct((bsz, seq, d), BF16),
        compiler_params=_cparams(("parallel", "parallel", "parallel")),
        name="dense_attention",
    )(*args)


def _store_token_tiles(ref, val, first=0):
    rows, d = val.shape
    rt = d // LANES
    for s in range(rt):
        ref[pl.ds(first * rt + s, rows, stride=rt), :] = val[:, s * LANES:(s + 1) * LANES]


def _load_token_tiles(ref, rows, rt):
    return jnp.concatenate([ref[pl.ds(s, rows, stride=rt), :] for s in range(rt)], axis=1)


def _layer_norm(r, g, b):
    mu = jnp.mean(r, axis=-1, keepdims=True)
    rc = r - mu
    var = jnp.mean(rc * rc, axis=-1, keepdims=True)
    return rc * lax.rsqrt(var + LN_EPS) * g + b


def _select_experts(logits):
    rows, n_exp = logits.shape
    lane_e = lax.broadcasted_iota(jnp.int32, (rows, n_exp), 1).astype(F32)
    work = logits
    sels, vals, idxs = [], [], []
    for _ in range(TOP_K):
        mk = jnp.max(work, axis=1, keepdims=True)
        ik = jnp.min(jnp.where(work == mk, lane_e, float(n_exp)), axis=1, keepdims=True)
        sel = lane_e == ik
        work = jnp.where(sel, -jnp.inf, work)
        sels.append(sel)
        vals.append(mk)
        idxs.append(ik)
    exps = [jnp.exp(v - vals[0]) for v in vals]
    denom = exps[0]
    for e in exps[1:]:
        denom = denom + e
    routed = sels[0].astype(F32)
    for sel in sels[1:]:
        routed = routed + sel.astype(F32)
    return sels, [e / denom for e in exps], idxs, routed


def _store_route(choice, before, first, gate_ref, idx_ref, rank_ref):
    sels, gate_cols, idx_cols, _ = choice
    rows = before.shape[0]
    lane_k = lax.broadcasted_iota(jnp.int32, (rows, TOP_K), 1)
    gates = jnp.zeros((rows, TOP_K), F32)
    idx = jnp.zeros((rows, TOP_K), F32)
    rank = jnp.zeros((rows, TOP_K), F32)
    for kk in range(TOP_K):
        here = lane_k == kk
        gates = jnp.where(here, gate_cols[kk], gates)
        idx = jnp.where(here, idx_cols[kk], idx)
        rank = jnp.where(here, jnp.sum(jnp.where(sels[kk], before, 0.0), axis=1, keepdims=True), rank)
    gate_ref[0, first:first + rows, :] = gates
    idx_ref[0, first:first + rows, :] = idx.astype(jnp.int32)
    rank_ref[0, first:first + rows, :] = rank.astype(jnp.int32)


def _mixer_out_kernel(*refs, conv, alpha, seq):
    if conv:
        bg_ref, u_ref, up_ref, un_ref, cw_ref = refs[:5]
        refs = refs[5:]
    else:
        o_ref = refs[0]
        refs = refs[1:]
    (w_ref, x_ref, mod_ref, lng_ref, lnb_ref, rwh_ref, rwl_ref, rb_ref,
     xo_ref, h_ref, gate_ref, idx_ref, rank_ref, cnt_ref) = refs
    if conv:
        i = pl.program_id(1)
        u = u_ref[0].astype(F32)
        tm = u.shape[0]
        row = lax.broadcasted_iota(jnp.int32, u.shape, 0)
        halo = up_ref.shape[1]
        before = jnp.where(i == 0, 0.0, up_ref[0, halo - 1:halo, :].astype(F32))
        after = jnp.where(i == seq // tm - 1, 0.0, un_ref[0, 0:1, :].astype(F32))
        u_prev = jnp.where(row == 0, before, pltpu.roll(u, 1, 0))
        u_next = jnp.where(row == tm - 1, after, pltpu.roll(u, tm - 1, 0))
        cw = cw_ref[...]
        y = cw[0:1] * u_prev + cw[1:2] * u + cw[2:3] * u_next
        mixed = (bg_ref[0].astype(F32) * y).astype(BF16)
    else:
        mixed = o_ref[0]
    m = mod_ref[0]
    tm = mixed.shape[0]
    n_parts = 2 if tm % (2 * MXU_COLS) == 0 else 1
    rows = tm // n_parts
    rwh = rwh_ref[...]
    starts = [part * rows for part in range(n_parts)]
    oxs = [jnp.dot(mixed[r0:r0 + rows], w_ref[...], preferred_element_type=F32) for r0 in starts]
    xns = [_layer_norm(alpha * x_ref[0, r0:r0 + rows, :] + m[2:3] * ox, lng_ref[...], lnb_ref[...])
           for r0, ox in zip(starts, oxs)]
    h2s = [xn * (1.0 + m[4:5]) + m[3:4] for xn in xns]
    all_logits = []
    for r0, xn, h2 in zip(starts, xns, h2s):
        xo_ref[0, r0:r0 + rows, :] = xn
        hh = h2.astype(BF16)
        hl = (h2 - hh.astype(F32)).astype(BF16)
        _store_token_tiles(h_ref, h2, r0)
        all_logits.append(jnp.dot(hh, rwh, preferred_element_type=F32) + jnp.dot(hl, rwh, preferred_element_type=F32)
                          + jnp.dot(hh, rwl_ref[...], preferred_element_type=F32) + rb_ref[...])
    choices = [_select_experts(logits) for logits in all_logits]
    routed = jnp.concatenate([c[3] for c in choices], axis=0)
    row = lax.broadcasted_iota(jnp.int32, (tm, tm), 0)
    col = lax.broadcasted_iota(jnp.int32, (tm, tm), 1)
    earlier = jnp.where(col < row, 1.0, 0.0).astype(BF16)
    before = jnp.dot(earlier, routed.astype(BF16), preferred_element_type=F32)
    for part, choice in enumerate(choices):
        r0 = part * rows
        _store_route(choice, before[r0:r0 + rows], r0, gate_ref, idx_ref, rank_ref)
    cnt_ref[0] = jnp.sum(routed, axis=0, keepdims=True).astype(jnp.int32)


def _mixer_out(mixed, w_out, x, mod, ln_g, ln_b, router_w, router_b, *, alpha, tm, conv_w=None):
    bsz, seq, d = x.shape
    n_exp = router_w.shape[1]
    nt = seq // tm
    tile = pl.BlockSpec((1, tm, d), lambda bi, i: (bi, i, 0))
    ktile = pl.BlockSpec((1, tm, TOP_K), lambda bi, i: (bi, i, 0))
    row = pl.BlockSpec((1, d), lambda bi, i: (0, 0))
    conv = conv_w is not None
    if conv:
        bg, u = mixed
        halo = 16
        r = tm // halo
        last = seq // halo - 1
        args = [bg, u, u, u, jnp.pad(conv_w.astype(F32), ((0, MOD_ROWS - conv_w.shape[0]), (0, 0)))]
        specs = [tile, tile,
                 pl.BlockSpec((1, halo, d), lambda bi, i: (bi, jnp.maximum(i * r - 1, 0), 0)),
                 pl.BlockSpec((1, halo, d), lambda bi, i: (bi, jnp.minimum((i + 1) * r, last), 0)),
                 pl.BlockSpec((MOD_ROWS, d), lambda bi, i: (0, 0))]
    else:
        args, specs = [mixed], [tile]
    rw_hi = router_w.astype(BF16)
    rw_lo = (router_w - rw_hi.astype(F32)).astype(BF16)
    args += [w_out.astype(BF16), x, mod, ln_g[None, :], ln_b[None, :], rw_hi, rw_lo, router_b[None, :]]
    specs += [pl.BlockSpec((d, d), lambda bi, i: (0, 0)), tile,
              pl.BlockSpec((1, MOD_ROWS, d), lambda bi, i: (bi, 0, 0)), row, row,
              pl.BlockSpec((d, n_exp), lambda bi, i: (0, 0)), pl.BlockSpec((d, n_exp), lambda bi, i: (0, 0)),
              pl.BlockSpec((1, n_exp), lambda bi, i: (0, 0))]
    return pl.pallas_call(
        functools.partial(_mixer_out_kernel, conv=conv, alpha=alpha, seq=seq),
        grid=(bsz, nt),
        in_specs=specs,
        out_specs=[tile, pl.BlockSpec((tm * (d // LANES), LANES), lambda bi, i: (bi * nt + i, 0)),
                   ktile, ktile, ktile,
                   pl.BlockSpec((1, 1, n_exp), lambda bi, i: (bi * nt + i, 0, 0))],
        out_shape=[jax.ShapeDtypeStruct((bsz, seq, d), F32),
                   jax.ShapeDtypeStruct((bsz * seq * (d // LANES), LANES), F32),
                   jax.ShapeDtypeStruct((bsz, seq, TOP_K), F32), jax.ShapeDtypeStruct((bsz, seq, TOP_K), jnp.int32),
                   jax.ShapeDtypeStruct((bsz, seq, TOP_K), jnp.int32),
                   jax.ShapeDtypeStruct((bsz * nt, 1, n_exp), jnp.int32)],
        compiler_params=_cparams(("parallel", "parallel")),
        name="mixer_out",
    )(*args)


def _expert_kernel(be_ref, first_ref, used_ref, x_ref, wgu_ref, bgu_ref, wdn_ref, bdn_ref, o_ref,
                   wgu_bf, wdn_bf, *, ff, fc):
    i = pl.program_id(0)

    @pl.when(first_ref[i] == 1)
    def _():
        wgu_bf[...] = wgu_ref[0, 0].astype(BF16)
        wdn_bf[...] = wdn_ref[0, 0].astype(BF16)

    @pl.when(i < used_ref[0])
    def _():
        rt = wdn_bf.shape[1] // LANES
        bm = x_ref.shape[0] // rt
        xb = _load_token_tiles(x_ref, bm, rt).astype(BF16)
        acc = jnp.zeros((bm, rt * LANES), F32)
        for c in range(ff // fc):
            lo = c * fc
            gate = jnp.dot(xb, wgu_bf[:, lo:lo + fc], preferred_element_type=F32) + bgu_ref[0, 0, :, lo:lo + fc]
            up = jnp.dot(xb, wgu_bf[:, ff + lo:ff + lo + fc], preferred_element_type=F32) \
                + bgu_ref[0, 0, :, ff + lo:ff + lo + fc]
            gate = jnp.minimum(gate, SWIGLU_LIMIT)
            up = jnp.clip(up, -SWIGLU_LIMIT, SWIGLU_LIMIT)
            act = (up + 1.0) * gate * jax.nn.sigmoid(SWIGLU_ALPHA * gate)
            acc = acc + jnp.dot(act.astype(BF16), wdn_bf[lo:lo + fc, :], preferred_element_type=F32)
        _store_token_tiles(o_ref, acc + bdn_ref[0, 0])

    @pl.when(i >= used_ref[0])
    def _():
        o_ref[...] = jnp.zeros_like(o_ref)


def _expert_ffn(xs, blk_expert, blk_first, n_used, layer, w_gu, b_gu, w_down, b_down, *, bm):
    depth, n_exp, d, ff2 = w_gu.shape
    ff = ff2 // 2
    rt = d // LANES
    wmap = lambda i, be, fi, nu: (layer, be[i], 0, 0)
    grid_spec = pltpu.PrefetchScalarGridSpec(
        num_scalar_prefetch=3,
        grid=(xs.shape[0] // (bm * rt),),
        in_specs=[pl.BlockSpec((bm * rt, LANES), lambda i, be, fi, nu: (jnp.minimum(i, nu[0] - 1), 0)),
                  pl.BlockSpec((1, 1, d, ff2), wmap), pl.BlockSpec((1, 1, 1, ff2), wmap),
                  pl.BlockSpec((1, 1, ff, d), wmap), pl.BlockSpec((1, 1, 1, d), wmap)],
        out_specs=pl.BlockSpec((bm * rt, LANES), lambda i, be, fi, nu: (i, 0)),
        scratch_shapes=[pltpu.VMEM((d, ff2), BF16), pltpu.VMEM((ff, d), BF16)],
    )
    return pl.pallas_call(
        functools.partial(_expert_kernel, ff=ff, fc=min(ff, 512)),
        grid_spec=grid_spec,
        out_shape=jax.ShapeDtypeStruct(xs.shape, F32),
        compiler_params=_cparams(("arbitrary",)),
        name="expert_ffn",
    )(blk_expert, blk_first, n_used, xs, w_gu, b_gu.reshape(depth, n_exp, 1, ff2), w_down,
      b_down.reshape(depth, n_exp, 1, d))


def _route(idx, rank, cnt_tiles, tile_rows, bm):
    t = idx.shape[0]
    a = t * TOP_K
    n_exp = cnt_tiles.shape[1]
    counts = jnp.sum(cnt_tiles, axis=0)
    padded = (counts + bm - 1) // bm * bm
    pends = jnp.cumsum(padded)
    pstarts = pends - padded
    tile_off = pstarts[None, :] + jnp.cumsum(cnt_tiles, axis=0) - cnt_tiles
    off_tok, lo = [], 0
    for rows, n_tiles in tile_rows:
        part = tile_off[lo:lo + n_tiles]
        off_tok.append(jnp.broadcast_to(part[:, None, :], (n_tiles, rows, n_exp)).reshape(n_tiles * rows, n_exp))
        lo += n_tiles
    off_tok = jnp.concatenate(off_tok, axis=0)
    chosen = idx[:, :, None] == jnp.arange(n_exp, dtype=jnp.int32)[None, None, :]
    dest = jnp.sum(jnp.where(chosen, off_tok[:, None, :], 0), axis=-1) + rank
    nblk = -(-a // bm) + n_exp
    blk_start = jnp.arange(nblk, dtype=jnp.int32) * bm
    blk_expert = jnp.minimum(jnp.sum((blk_start[:, None] >= pends[None, :]).astype(jnp.int32), axis=1),
                             n_exp - 1)
    blk_first = jnp.concatenate([jnp.ones((1,), jnp.int32),
                                 (blk_expert[1:] != blk_expert[:-1]).astype(jnp.int32)])
    n_used = (pends[-1] // bm).astype(jnp.int32).reshape(1)
    last_blk = jnp.maximum(pends // bm - 1, 0).astype(jnp.int32)
    return dest, nblk, blk_expert, blk_first, n_used, last_blk


def _dispatch_kernel(*refs, rt, n_exp, nblk, tiles):
    n_s = len(tiles)
    lb_ref, nu_ref = refs[:2]
    dest_refs = refs[2:2 + n_s]
    h_refs = refs[2 + n_s:2 + 2 * n_s]
    xs_ref, zero_buf, sem, zero_sem = refs[2 + 2 * n_s:]
    i = pl.program_id(0)
    blk_rows = zero_buf.shape[0]

    def zero_copy(blk):
        return pltpu.make_async_copy(
            zero_buf, xs_ref.at[pl.ds(pl.multiple_of(blk * blk_rows, blk_rows), blk_rows), :], zero_sem)

    @pl.when(i == 0)
    def _():
        zero_buf[...] = jnp.zeros_like(zero_buf)
        todo = [(lb_ref[0], None)]
        todo += [(lb_ref[e], lb_ref[e] != lb_ref[e - 1]) for e in range(1, n_exp)]
        todo += [(nu_ref[0] + e, nu_ref[0] + e < nblk) for e in range(n_exp)]
        for blk, cond in todo:
            if cond is None:
                zero_copy(blk).start()
            else:
                pl.when(cond)(lambda blk=blk: zero_copy(blk).start())
        for _, cond in todo:
            if cond is None:
                zero_copy(0).wait()
            else:
                pl.when(cond)(lambda: zero_copy(0).wait())

    first = 0
    for dest_ref, h_ref, (tm, nt) in zip(dest_refs, h_refs, tiles):
        @pl.when((i >= first) & (i < first + nt))
        def _(dest_ref=dest_ref, h_ref=h_ref, tm=tm):
            def issue(j, carry):
                base = j * SUBLANES
                for s in range(SUBLANES):
                    src = h_ref.at[pl.ds(pl.multiple_of((base + s) * rt, rt), rt), :]
                    for kk in range(TOP_K):
                        dst = pl.multiple_of(dest_ref[0, 0, (base + s) * TOP_K + kk], rt)
                        pltpu.make_async_copy(src, xs_ref.at[pl.ds(dst, rt), :], sem).start(priority=kk % 2)
                return carry

            lax.fori_loop(0, tm // SUBLANES, issue, 0)
            for kk in range(TOP_K):
                pltpu.make_async_copy(h_ref, xs_ref.at[pl.ds(0, tm * rt), :], sem).wait()

        first += nt


def _dispatch(streams, last_blk, n_used, nblk, *, bm, rt):
    n_exp = last_blk.shape[0]
    tiles = [(tm, h.shape[0] // (tm * rt)) for h, _, tm in streams]
    dest_specs, h_specs, dests, hs = [], [], [], []
    first = 0
    for (h, dest_rows, tm), (_, nt) in zip(streams, tiles):
        tile_of = lambda i, lb, nu, first=first, nt=nt: jnp.clip(i - first, 0, nt - 1)
        dest_specs.append(pl.BlockSpec((1, 1, tm * TOP_K), lambda i, lb, nu, f=tile_of: (f(i, lb, nu), 0, 0),
                                       memory_space=pltpu.SMEM))
        h_specs.append(pl.BlockSpec((tm * rt, LANES), lambda i, lb, nu, f=tile_of: (f(i, lb, nu), 0)))
        dests.append(dest_rows.reshape(nt, 1, tm * TOP_K))
        hs.append(h)
        first += nt
    return pl.pallas_call(
        functools.partial(_dispatch_kernel, rt=rt, n_exp=n_exp, nblk=nblk, tiles=tiles),
        grid_spec=pltpu.PrefetchScalarGridSpec(
            num_scalar_prefetch=2, grid=(first,), in_specs=dest_specs + h_specs,
            out_specs=pl.BlockSpec(memory_space=pl.ANY),
            scratch_shapes=[pltpu.VMEM((bm * rt, LANES), F32), pltpu.SemaphoreType.DMA(()),
                            pltpu.SemaphoreType.DMA(())]),
        out_shape=jax.ShapeDtypeStruct((nblk * bm * rt, LANES), F32),
        compiler_params=_cparams(("arbitrary",)),
        name="dispatch_rows",
    )(last_blk, n_used, *dests, *hs)


def _combine_kernel(dest_ref, next_dest_ref, out_ref, gate_ref, x_ref, mod_ref, lng_ref, lnb_ref, o_ref,
                    y_buf, sem, *, alpha, n_tiles):
    n = pl.program_id(0)
    tm, d = x_ref.shape[1:]
    rt = d // LANES

    def request(table_ref, slot):
        def issue(j, carry):
            base = j * SUBLANES
            for s in range(SUBLANES):
                row = pl.multiple_of((base + s) * rt, rt)
                for kk in range(TOP_K):
                    src = pl.multiple_of(table_ref[0, 0, (base + s) * TOP_K + kk], rt)
                    pltpu.make_async_copy(out_ref.at[pl.ds(src, rt), :], y_buf.at[slot, kk, pl.ds(row, rt), :],
                                          sem.at[slot]).start(priority=kk % 2)
            return carry

        lax.fori_loop(0, tm // SUBLANES, issue, 0)

    slot = n % 2

    @pl.when(n == 0)
    def _():
        request(dest_ref, 0)

    @pl.when(n + 1 < n_tiles)
    def _():
        request(next_dest_ref, 1 - slot)

    for kk in range(TOP_K):
        pltpu.make_async_copy(out_ref.at[pl.ds(0, tm * rt), :], y_buf.at[slot, kk], sem.at[slot]).wait()
    g = gate_ref[0]
    fx = g[:, 0:1] * _load_token_tiles(y_buf.at[slot, 0], tm, rt)
    for kk in range(1, TOP_K):
        fx = fx + g[:, kk:kk + 1] * _load_token_tiles(y_buf.at[slot, kk], tm, rt)
    m = mod_ref[0]
    o_ref[0] = _layer_norm(alpha * x_ref[0] + m[5:6] * fx, lng_ref[...], lnb_ref[...])


def _combine(out, dest_rows, gates, x, mod, ln_g, ln_b, *, alpha, tm):
    bsz, seq, d = x.shape
    nt = seq // tm
    n_tiles = bsz * nt
    dest = dest_rows.reshape(n_tiles, 1, tm * TOP_K)
    tile = pl.BlockSpec((1, tm, d), lambda n: (n // nt, n % nt, 0))
    row = pl.BlockSpec((1, d), lambda n: (0, 0))
    return pl.pallas_call(
        functools.partial(_combine_kernel, alpha=alpha, n_tiles=n_tiles),
        grid=(n_tiles,),
        in_specs=[pl.BlockSpec((1, 1, tm * TOP_K), lambda n: (n, 0, 0), memory_space=pltpu.SMEM),
                  pl.BlockSpec((1, 1, tm * TOP_K), lambda n: (jnp.minimum(n + 1, n_tiles - 1), 0, 0),
                               memory_space=pltpu.SMEM),
                  pl.BlockSpec(memory_space=pl.ANY),
                  pl.BlockSpec((1, tm, TOP_K), lambda n: (n // nt, n % nt, 0)),
                  tile, pl.BlockSpec((1, MOD_ROWS, d), lambda n: (n // nt, 0, 0)), row, row],
        out_specs=tile,
        out_shape=jax.ShapeDtypeStruct((bsz, seq, d), F32),
        scratch_shapes=[pltpu.VMEM((2, TOP_K, tm * (d // LANES), LANES), F32), pltpu.SemaphoreType.DMA((2,))],
        compiler_params=_cparams(("arbitrary",)),
        name="combine_norm",
    )(dest, dest, out, gates, x, mod, ln_g[None, :], ln_b[None, :])


def _rope_tables(seq):
    t = jnp.arange(seq, dtype=jnp.int32)
    row = (t // GRID_W).astype(F32)
    col = (t % GRID_W).astype(F32)
    n_freq = HEAD_DIM // 4
    inv_freq = ROPE_THETA ** (-jnp.arange(n_freq, dtype=F32) / n_freq)
    ar, ac = row[:, None] * inv_freq, col[:, None] * inv_freq
    cos = jnp.concatenate([jnp.cos(ar), jnp.cos(ar), jnp.cos(ac), jnp.cos(ac)], axis=1)
    sin = jnp.concatenate([-jnp.sin(ar), jnp.sin(ar), -jnp.sin(ac), jnp.sin(ac)], axis=1)
    return jnp.tile(cos, (1, 2)), jnp.tile(sin, (1, 2))


def _tile(n, want):
    t = min(n, want)
    while n % t:
        t //= 2
    return t


def kernel(x, c, ctx, c_ctx, mod_w, mod_b, ln1_g, ln1_b, ln2_g, ln2_b, win_wqkv, win_bqkv, win_sink, win_wo,
           conv_win, conv_w, conv_wout, full_wqkv, full_qnorm, full_knorm, full_wo,
           router_w, router_b, expert_wgu, expert_bgu, expert_wdown, expert_bdown):
    bsz, seq, d = x.shape
    ctx_len = ctx.shape[1]
    depth = mod_w.shape[0]
    alpha = (2.0 * depth) ** 0.25
    tables = _rope_tables(seq)
    tm_x, tm_z = _tile(seq, 512), _tile(ctx_len, 512)
    tq_win = _tile(seq, 256)
    tq_full = _tile(seq, 512)
    bm = 768

    c_rows = jnp.zeros((2 * MOD_ROWS, d), F32).at[:bsz].set(c).at[bsz].set(c_ctx)
    mods = _modulation(c_rows, mod_w, mod_b)
    z = ctx
    for i in range(depth):
        kind, j = i % N_MIXERS, i // N_MIXERS
        need_ctx = i < depth - 1
        mod_x = jnp.pad(mods[i, :bsz].reshape(bsz, 6, d), ((0, 0), (0, MOD_ROWS - 6), (0, 0)))
        mod_z = jnp.broadcast_to(jnp.pad(mods[i, bsz].reshape(1, 6, d), ((0, 0), (0, MOD_ROWS - 6), (0, 0))),
                                 (bsz, MOD_ROWS, d))
        route_args = (router_w[i], router_b[i])
        ln1 = (ln1_g[i], ln1_b[i])
        oz = None
        if kind == 0:
            qx, kx, vx = _qkv_project(x, mod_x, win_wqkv[j], win_bqkv[j], tables=tables, tm=tm_x)
            qz, kz, vz = _qkv_project(z, mod_z, win_wqkv[j], win_bqkv[j], tm=tm_z)
            ox = _window_attention(qx, kx, vx, kz, vz, win_sink[j], tq=tq_win)
            if need_ctx:
                oz = _dense_attention(qz, kz, vz, win_sink[j], tq=ctx_len)
            w_out, conv_taps = win_wo[j], None
        elif kind == 1:
            ox = _conv_in_project(x, mod_x, conv_win[j], tm=tm_x)
            if need_ctx:
                oz = _conv_in_project(z, mod_z, conv_win[j], tm=tm_z)
            w_out, conv_taps = conv_wout[j], conv_w[j]
        else:
            zero_b = jnp.zeros((full_wqkv.shape[2],), F32)
            norms = (full_qnorm[j], full_knorm[j])
            qx, kx, vx = _qkv_project(x, mod_x, full_wqkv[j], zero_b, tables=tables, norms=norms, tm=tm_x)
            qz, kz, vz = _qkv_project(z, mod_z, full_wqkv[j], zero_b, norms=norms, tm=tm_z)
            k_all = jnp.concatenate([kx, kz], axis=1)
            v_all = jnp.concatenate([vx, vz], axis=1)
            ox = _dense_attention(qx, k_all, v_all, tq=tq_full)
            if need_ctx:
                oz = _dense_attention(qz, kz, vz, tq=ctx_len)
            w_out, conv_taps = full_wo[j], None

        x, hx, gx, ix, rx, cx = _mixer_out(ox, w_out, x, mod_x, *ln1, *route_args, alpha=alpha, tm=tm_x,
                                           conv_w=conv_taps)
        n_x = bsz * seq
        idx, rank, cnt = ix.reshape(n_x, TOP_K), rx.reshape(n_x, TOP_K), cx[:, 0]
        tile_rows = [(tm_x, n_x // tm_x)]
        if need_ctx:
            z, hz, gz, iz, rz, cz = _mixer_out(oz, w_out, z, mod_z, *ln1, *route_args, alpha=alpha, tm=tm_z,
                                               conv_w=conv_taps)
            n_z = bsz * ctx_len
            idx = jnp.concatenate([idx, iz.reshape(n_z, TOP_K)], axis=0)
            rank = jnp.concatenate([rank, rz.reshape(n_z, TOP_K)], axis=0)
            cnt = jnp.concatenate([cnt, cz[:, 0]], axis=0)
            tile_rows.append((tm_z, n_z // tm_z))

        dest, nblk, blk_expert, blk_first, n_used, last_blk = _route(idx, rank, cnt, tile_rows, bm)
        rt = d // LANES
        dest_rows = dest * rt
        streams = [(hx, dest_rows[:n_x], tm_x)]
        if need_ctx:
            streams.append((hz, dest_rows[n_x:], tm_z))
        xs = _dispatch(streams, last_blk, n_used, nblk, bm=bm, rt=rt)
        out = _expert_ffn(xs, blk_expert, blk_first, n_used, i,
                          expert_wgu, expert_bgu, expert_wdown, expert_bdown, bm=bm)
        x = _combine(out, dest_rows[:n_x], gx, x, mod_x, ln2_g[i], ln2_b[i], alpha=alpha, tm=tm_x)
        if need_ctx:
            z = _combine(out, dest_rows[n_x:], gz, z, mod_z, ln2_g[i], ln2_b[i], alpha=alpha, tm=tm_z)
    return x
```

```python
import functools

import jax
import jax.numpy as jnp
from jax import lax
from jax.experimental import pallas as pl
from jax.experimental.pallas import tpu as pltpu

HEAD_DIM = 64
GROUP = 4
GRID_W = 64
WINDOW = 128
ROPE_THETA = 10000.0
TOP_K = 4
N_MIXERS = 3
SWIGLU_ALPHA = 1.702
SWIGLU_LIMIT = 7.0
LN_EPS = 1e-5
RMS_EPS = 1e-6
NEG_INF = -1e30

LANES = 128
SUBLANES = 8
MXU_COLS = 256
MOD_ROWS = 8
VMEM_LIMIT = 56 * 1024 * 1024

F32 = jnp.float32
BF16 = jnp.bfloat16


def _cparams(sem):
    return pltpu.CompilerParams(dimension_semantics=sem, vmem_limit_bytes=VMEM_LIMIT)


def _mod_kernel(c_ref, w_ref, b_ref, o_ref):
    c = c_ref[...]
    s = c * jax.nn.sigmoid(c)
    o_ref[0] = jnp.dot(s, w_ref[0], preferred_element_type=F32, precision=lax.Precision.HIGHEST) + b_ref[0]


def _modulation(c_rows, mod_w, mod_b):
    depth, d, n = mod_w.shape
    r = c_rows.shape[0]
    tn = min(n, 1536)
    return pl.pallas_call(
        _mod_kernel,
        grid=(depth, n // tn),
        in_specs=[pl.BlockSpec((r, d), lambda l, j: (0, 0)),
                  pl.BlockSpec((1, d, tn), lambda l, j: (l, 0, j)),
                  pl.BlockSpec((1, 1, tn), lambda l, j: (l, 0, j))],
        out_specs=pl.BlockSpec((1, r, tn), lambda l, j: (l, 0, j)),
        out_shape=jax.ShapeDtypeStruct((depth, r, n), F32),
        compiler_params=_cparams(("arbitrary", "arbitrary")),
        name="modulation",
    )(c_rows, mod_w, mod_b.reshape(depth, 1, n))


def _swap_pairs(t):
    lane = lax.broadcasted_iota(jnp.int32, t.shape, 1)
    return jnp.where(lane % 32 < 16, pltpu.roll(t, LANES - 16, 1), pltpu.roll(t, 16, 1))


def _head_mean_sq(t, seg_ref):
    t2 = t * t
    hi = t2.astype(BF16)
    lo = (t2 - hi.astype(F32)).astype(BF16)
    seg = seg_ref[...]
    s = jnp.dot(hi, seg, preferred_element_type=F32) + jnp.dot(lo, seg, preferred_element_type=F32)
    return s * (1.0 / HEAD_DIM)


def _qkv_kernel(*refs, nq, nk, rope, qk_norm):
    x_ref, mod_ref, w_ref, b_ref = refs[:4]
    pos = 4
    if rope:
        cos_ref, sin_ref = refs[pos:pos + 2]
        pos += 2
    if qk_norm:
        seg_ref, qg_ref, kg_ref = refs[pos:pos + 3]
        pos += 3
    q_ref, k_ref, v_ref = refs[pos:pos + 3]
    m = mod_ref[0]
    h = (x_ref[0] * (1.0 + m[1:2]) + m[0:1]).astype(BF16)
    if rope:
        cos, sin = cos_ref[...], sin_ref[...]
    starts = range(0, nq + 2 * nk, MXU_COLS)
    wides = [jnp.dot(h, w_ref[:, lo:lo + MXU_COLS], preferred_element_type=F32) + b_ref[:, lo:lo + MXU_COLS]
             for lo in starts]
    if qk_norm:
        mean_sqs = [_head_mean_sq(wide, seg_ref) if lo < nq + nk else None for lo, wide in zip(starts, wides)]
        wides = [wide if ms is None else wide * lax.rsqrt(ms + RMS_EPS) * (qg_ref[...] if lo < nq else kg_ref[...])
                 for lo, wide, ms in zip(starts, wides, mean_sqs)]
    for lo, wide in zip(starts, wides):
        if lo >= nq + nk:
            v_ref[0, :, lo - nq - nk:lo - nq - nk + MXU_COLS] = wide.astype(BF16)
            continue
        is_q = lo < nq
        for half in range(MXU_COLS // LANES):
            t = wide[:, half * LANES:(half + 1) * LANES]
            at = lo + half * LANES
            if rope:
                t = t * cos + _swap_pairs(t) * sin
            if is_q:
                q_ref[0, :, at:at + LANES] = (t * HEAD_DIM ** -0.5).astype(BF16)
            else:
                k_ref[0, :, at - nq:at - nq + LANES] = t.astype(BF16)


def _dup_heads(w, n_heads):
    lead = w.shape[:-1]
    w = w.reshape(lead + (n_heads, 1, HEAD_DIM))
    return jnp.broadcast_to(w, lead + (n_heads, 2, HEAD_DIM)).reshape(lead + (n_heads * LANES,))


def _pad_heads(w, n_heads, fill):
    lead = w.shape[:-1]
    w = w.reshape(lead + (n_heads, HEAD_DIM))
    pad = jnp.full(lead + (n_heads, HEAD_DIM), fill, w.dtype)
    return jnp.concatenate([w, pad], axis=-1).reshape(lead + (n_heads * LANES,))


def _qkv_project(x, mod, w, b, *, tables=None, norms=None, tm):
    bsz, seq, d = x.shape
    n_kv = (w.shape[1] - d) // (2 * HEAD_DIM)
    nq, nk = d, n_kv * LANES
    wq, wk, wv = w[:, :d], w[:, d:d + n_kv * HEAD_DIM], w[:, d + n_kv * HEAD_DIM:]
    w_ext = jnp.concatenate([wq, _dup_heads(wk, n_kv), _pad_heads(wv, n_kv, 0.0)], axis=1).astype(BF16)
    bq, bk, bv = b[:d], b[d:d + n_kv * HEAD_DIM], b[d + n_kv * HEAD_DIM:]
    b_ext = jnp.concatenate([bq, _dup_heads(bk, n_kv), _pad_heads(bv, n_kv, 1.0)])[None, :].astype(F32)
    n = nq + 2 * nk
    args = [x, mod, w_ext, b_ext]
    specs = [pl.BlockSpec((1, tm, d), lambda bi, i: (bi, i, 0)),
             pl.BlockSpec((1, MOD_ROWS, d), lambda bi, i: (bi, 0, 0)),
             pl.BlockSpec((d, n), lambda bi, i: (0, 0)),
             pl.BlockSpec((1, n), lambda bi, i: (0, 0))]
    if tables is not None:
        args += list(tables)
        specs += [pl.BlockSpec((tm, LANES), lambda bi, i: (i, 0))] * 2
    if norms is not None:
        qn, kn = norms
        col = jnp.arange(MXU_COLS) // HEAD_DIM
        seg = (col[:, None] == col[None, :]).astype(BF16)
        reps = MXU_COLS // HEAD_DIM
        args += [seg, jnp.tile(qn, reps)[None, :].astype(F32), jnp.tile(kn, reps)[None, :].astype(F32)]
        specs += [pl.BlockSpec((MXU_COLS, MXU_COLS), lambda bi, i: (0, 0)),
                  pl.BlockSpec((1, MXU_COLS), lambda bi, i: (0, 0)),
                  pl.BlockSpec((1, MXU_COLS), lambda bi, i: (0, 0))]
    return pl.pallas_call(
        functools.partial(_qkv_kernel, nq=nq, nk=nk, rope=tables is not None, qk_norm=norms is not None),
        grid=(bsz, seq // tm),
        in_specs=specs,
        out_specs=[pl.BlockSpec((1, tm, nq), lambda bi, i: (bi, i, 0)),
                   pl.BlockSpec((1, tm, nk), lambda bi, i: (bi, i, 0)),
                   pl.BlockSpec((1, tm, nk), lambda bi, i: (bi, i, 0))],
        out_shape=[jax.ShapeDtypeStruct((bsz, seq, nq), BF16),
                   jax.ShapeDtypeStruct((bsz, seq, nk), BF16),
                   jax.ShapeDtypeStruct((bsz, seq, nk), BF16)],
        compiler_params=_cparams(("parallel", "parallel")),
        name="qkv_project",
    )(*args)


def _conv_in_kernel(x_ref, mod_ref, w_ref, bg_ref, u_ref, *, d):
    m = mod_ref[0]
    h = (x_ref[0] * (1.0 + m[1:2]) + m[0:1]).astype(BF16)
    for c in range(d // MXU_COLS):
        lo = c * MXU_COLS
        bg = jnp.dot(h, w_ref[:, lo:lo + MXU_COLS], preferred_element_type=F32)
        cg = jnp.dot(h, w_ref[:, d + lo:d + lo + MXU_COLS], preferred_element_type=F32)
        xv = jnp.dot(h, w_ref[:, 2 * d + lo:2 * d + lo + MXU_COLS], preferred_element_type=F32)
        bg_ref[0, :, lo:lo + MXU_COLS] = bg.astype(BF16)
        u_ref[0, :, lo:lo + MXU_COLS] = (cg * xv).astype(BF16)


def _conv_in_project(x, mod, w_in, *, tm):
    bsz, seq, d = x.shape
    return pl.pallas_call(
        functools.partial(_conv_in_kernel, d=d),
        grid=(bsz, seq // tm),
        in_specs=[pl.BlockSpec((1, tm, d), lambda bi, i: (bi, i, 0)),
                  pl.BlockSpec((1, MOD_ROWS, d), lambda bi, i: (bi, 0, 0)),
                  pl.BlockSpec((d, 3 * d), lambda bi, i: (0, 0))],
        out_specs=[pl.BlockSpec((1, tm, d), lambda bi, i: (bi, i, 0))] * 2,
        out_shape=[jax.ShapeDtypeStruct((bsz, seq, d), BF16)] * 2,
        compiler_params=_cparams(("parallel", "parallel")),
        name="conv_in_project",
    )(x, mod, w_in.astype(BF16))


def _qk(q, k):
    return lax.dot_general(q, k, (((1,), (1,)), ((), ())), preferred_element_type=F32)


def _scores(qg, kc, bias):
    s = _qk(qg, kc)
    return s if bias is None else s + bias


def _row_max(qg, chunks, sink):
    mx = None
    for kc, _, bias in chunks:
        s = _scores(qg, kc, bias)
        for j in range(s.shape[1] // LANES):
            part = s[:, j * LANES:(j + 1) * LANES]
            mx = part if mx is None else jnp.maximum(mx, part)
    m = jnp.max(mx, axis=1, keepdims=True)
    return m if sink is None else jnp.maximum(m, sink)


def _exp_pv(qg, chunks, m):
    acc = None
    for kc, vc, bias in chunks:
        p = jnp.exp(_scores(qg, kc, bias) - m).astype(BF16)
        d = jnp.dot(p, vc, preferred_element_type=F32)
        acc = d if acc is None else acc + d
    return acc


def _attend_groups(q_ref, o_ref, groups, lane):
    heads = []
    for col0, chunks, sinks in groups:
        for pair in range(GROUP // 2):
            qp = q_ref[0, :, col0 + pair * LANES:col0 + (pair + 1) * LANES]
            heads.append((jnp.where(lane < HEAD_DIM, qp, jnp.zeros_like(qp)), chunks, sinks[2 * pair]))
            heads.append((jnp.where(lane >= HEAD_DIM, qp, jnp.zeros_like(qp)), chunks, sinks[2 * pair + 1]))
    ms = [_row_max(qg, chunks, sink) for qg, chunks, sink in heads]
    accs = [_exp_pv(qg, chunks, m) for (qg, chunks, _), m in zip(heads, ms)]
    for gi, (col0, _, sinks) in enumerate(groups):
        for pair in range(GROUP // 2):
            e, o = gi * GROUP + 2 * pair, gi * GROUP + 2 * pair + 1
            a0, a1 = accs[e], accs[o]
            r0, r1 = pltpu.roll(a0, HEAD_DIM, 1), pltpu.roll(a1, HEAD_DIM, 1)
            d0, d1 = r0, a1
            if sinks[0] is not None:
                d0 = d0 + jnp.exp(sinks[2 * pair] - ms[e])
                d1 = d1 + jnp.exp(sinks[2 * pair + 1] - ms[o])
            out = jnp.where(lane < HEAD_DIM, a0 / d0, r1 / d1)
            o_ref[0, :, col0 + pair * LANES:col0 + (pair + 1) * LANES] = out.astype(o_ref.dtype)


def _win_attn_kernel(q_ref, kp_ref, kc_ref, kn_ref, kz_ref, vp_ref, vc_ref, vn_ref, vz_ref, sink_ref, o_ref,
                     *, tq, seq):
    i = pl.program_id(1)
    r = lax.broadcasted_iota(jnp.int32, (tq, WINDOW), 0)
    c = lax.broadcasted_iota(jnp.int32, (tq, WINDOW), 1)
    bias_prev = jnp.where((c >= r) & (i > 0), 0.0, NEG_INF)
    bias_next = jnp.where((r - c >= tq - WINDOW) & (i < seq // tq - 1), 0.0, NEG_INF)
    bias_cur = None
    if tq - 1 > WINDOW:
        rr = lax.broadcasted_iota(jnp.int32, (tq, tq), 0)
        cc = lax.broadcasted_iota(jnp.int32, (tq, tq), 1)
        bias_cur = jnp.where(jnp.abs(cc - rr) <= WINDOW, 0.0, NEG_INF)
    lane = lax.broadcasted_iota(jnp.int32, (tq, LANES), 1)
    groups = []
    for hk in range(kc_ref.shape[2] // LANES):
        lo = hk * LANES
        chunks = [(kp_ref[0, :, lo:lo + LANES], vp_ref[0, :, lo:lo + LANES], bias_prev),
                  (kc_ref[0, :, lo:lo + LANES], vc_ref[0, :, lo:lo + LANES], bias_cur),
                  (kn_ref[0, :, lo:lo + LANES], vn_ref[0, :, lo:lo + LANES], bias_next),
                  (kz_ref[0, :, lo:lo + LANES], vz_ref[0, :, lo:lo + LANES], None)]
        groups.append((hk * GROUP * HEAD_DIM, chunks, [sink_ref[hk, g] for g in range(GROUP)]))
    _attend_groups(q_ref, o_ref, groups, lane)


def _window_attention(q, k, v, kz, vz, sink, *, tq):
    bsz, seq, d = q.shape
    nk = k.shape[2]
    ctx = kz.shape[1]
    r = tq // WINDOW
    last = seq // WINDOW - 1
    cur = lambda bi, i: (bi, i, 0)
    prev = lambda bi, i: (bi, jnp.maximum(i * r - 1, 0), 0)
    nxt = lambda bi, i: (bi, jnp.minimum((i + 1) * r, last), 0)
    zmap = lambda bi, i: (bi, 0, 0)
    kv_specs = [pl.BlockSpec((1, WINDOW, nk), prev), pl.BlockSpec((1, tq, nk), cur),
                pl.BlockSpec((1, WINDOW, nk), nxt), pl.BlockSpec((1, ctx, nk), zmap)]
    return pl.pallas_call(
        functools.partial(_win_attn_kernel, tq=tq, seq=seq),
        grid=(bsz, seq // tq),
        in_specs=[pl.BlockSpec((1, tq, d), cur)] + kv_specs + kv_specs
                 + [pl.BlockSpec(memory_space=pltpu.SMEM)],
        out_specs=pl.BlockSpec((1, tq, d), cur),
        out_shape=jax.ShapeDtypeStruct((bsz, seq, d), BF16),
        compiler_params=_cparams(("parallel", "parallel")),
        name="window_attention",
    )(q, k, k, k, kz, v, v, v, vz, sink.reshape(nk // LANES, GROUP).astype(F32))


def _dense_attn_kernel(*refs, ck, has_sink):
    if has_sink:
        q_ref, k_ref, v_ref, sink_ref, o_ref = refs
    else:
        q_ref, k_ref, v_ref, o_ref = refs
    hk = pl.program_id(1)
    tq = q_ref.shape[1]
    chunks = [(k_ref[0, lo:lo + ck, :], v_ref[0, lo:lo + ck, :], None) for lo in range(0, k_ref.shape[1], ck)]
    lane = lax.broadcasted_iota(jnp.int32, (tq, LANES), 1)
    sinks = [sink_ref[hk, g] if has_sink else None for g in range(GROUP)]
    _attend_groups(q_ref, o_ref, [(0, chunks, sinks)], lane)


def _dense_attention(q, k, v, sink=None, *, tq):
    bsz, seq, d = q.shape
    n_keys = k.shape[1]
    n_kv = k.shape[2] // LANES
    ck = 256 if n_keys % 256 == 0 else LANES
    qmap = lambda bi, h, i: (bi, i, h)
    kmap = lambda bi, h, i: (bi, 0, h)
    args = [q, k, v]
    specs = [pl.BlockSpec((1, tq, GROUP * HEAD_DIM), qmap),
             pl.BlockSpec((1, n_keys, LANES), kmap), pl.BlockSpec((1, n_keys, LANES), kmap)]
    if sink is not None:
        args.append(sink.reshape(n_kv, GROUP).astype(F32))
        specs.append(pl.BlockSpec(memory_space=pltpu.SMEM))
    return pl.pallas_call(
        functools.partial(_dense_attn_kernel, ck=ck, has_sink=sink is not None),
        grid=(bsz, n_kv, seq // tq),
        in_specs=specs,
        out_specs=pl.BlockSpec((1, tq, GROUP * HEAD_DIM), qmap),
        out_shape=jax.ShapeDtypeStruct((bsz, seq, d), BF16),
        compiler_params=_cparams(("parallel", "parallel", "parallel")),
        name="dense_attention",
    )(*args)


def _store_token_tiles(ref, val, first=0):
    rows, d = val.shape
    rt = d // LANES
    for s in range(rt):
        ref[pl.ds(first * rt + s, rows, stride=rt), :] = val[:, s * LANES:(s + 1) * LANES]


def _load_token_tiles(ref, rows, rt):
    return jnp.concatenate([ref[pl.ds(s, rows, stride=rt), :] for s in range(rt)], axis=1)


def _layer_norm(r, g, b):
    mu = jnp.mean(r, axis=-1, keepdims=True)
    rc = r - mu
    var = jnp.mean(rc * rc, axis=-1, keepdims=True)
    return rc * lax.rsqrt(var + LN_EPS) * g + b


def _select_experts(logits):
    rows, n_exp = logits.shape
    lane_e = lax.broadcasted_iota(jnp.int32, (rows, n_exp), 1).astype(F32)
    work = logits
    sels, vals, idxs = [], [], []
    for _ in range(TOP_K):
        mk = jnp.max(work, axis=1, keepdims=True)
        ik = jnp.min(jnp.where(work == mk, lane_e, float(n_exp)), axis=1, keepdims=True)
        sel = lane_e == ik
        work = jnp.where(sel, -jnp.inf, work)
        sels.append(sel)
        vals.append(mk)
        idxs.append(ik)
    exps = [jnp.exp(v - vals[0]) for v in vals]
    denom = exps[0]
    for e in exps[1:]:
        denom = denom + e
    routed = sels[0].astype(F32)
    for sel in sels[1:]:
        routed = routed + sel.astype(F32)
    return sels, [e / denom for e in exps], idxs, routed


def _store_route(choice, before, first, gate_ref, idx_ref, rank_ref):
    sels, gate_cols, idx_cols, _ = choice
    rows = before.shape[0]
    lane_k = lax.broadcasted_iota(jnp.int32, (rows, TOP_K), 1)
    gates = jnp.zeros((rows, TOP_K), F32)
    idx = jnp.zeros((rows, TOP_K), F32)
    rank = jnp.zeros((rows, TOP_K), F32)
    for kk in range(TOP_K):
        here = lane_k == kk
        gates = jnp.where(here, gate_cols[kk], gates)
        idx = jnp.where(here, idx_cols[kk], idx)
        rank = jnp.where(here, jnp.sum(jnp.where(sels[kk], before, 0.0), axis=1, keepdims=True), rank)
    gate_ref[0, first:first + rows, :] = gates
    idx_ref[0, first:first + rows, :] = idx.astype(jnp.int32)
    rank_ref[0, first:first + rows, :] = rank.astype(jnp.int32)


def _mixer_out_kernel(*refs, conv, alpha, seq):
    if conv:
        bg_ref, u_ref, up_ref, un_ref, cw_ref = refs[:5]
        refs = refs[5:]
    else:
        o_ref = refs[0]
        refs = refs[1:]
    (w_ref, x_ref, mod_ref, lng_ref, lnb_ref, rwh_ref, rwl_ref, rb_ref,
     xo_ref, h_ref, gate_ref, idx_ref, rank_ref, cnt_ref) = refs
    if conv:
        i = pl.program_id(1)
        u = u_ref[0].astype(F32)
        tm = u.shape[0]
        row = lax.broadcasted_iota(jnp.int32, u.shape, 0)
        halo = up_ref.shape[1]
        before = jnp.where(i == 0, 0.0, up_ref[0, halo - 1:halo, :].astype(F32))
        after = jnp.where(i == seq // tm - 1, 0.0, un_ref[0, 0:1, :].astype(F32))
        u_prev = jnp.where(row == 0, before, pltpu.roll(u, 1, 0))
        u_next = jnp.where(row == tm - 1, after, pltpu.roll(u, tm - 1, 0))
        cw = cw_ref[...]
        y = cw[0:1] * u_prev + cw[1:2] * u + cw[2:3] * u_next
        mixed = (bg_ref[0].astype(F32) * y).astype(BF16)
    else:
        mixed = o_ref[0]
    m = mod_ref[0]
    tm = mixed.shape[0]
    n_parts = 2 if tm % (2 * MXU_COLS) == 0 else 1
    rows = tm // n_parts
    rwh = rwh_ref[...]
    starts = [part * rows for part in range(n_parts)]
    oxs = [jnp.dot(mixed[r0:r0 + rows], w_ref[...], preferred_element_type=F32) for r0 in starts]
    xns = [_layer_norm(alpha * x_ref[0, r0:r0 + rows, :] + m[2:3] * ox, lng_ref[...], lnb_ref[...])
           for r0, ox in zip(starts, oxs)]
    h2s = [xn * (1.0 + m[4:5]) + m[3:4] for xn in xns]
    all_logits = []
    for r0, xn, h2 in zip(starts, xns, h2s):
        xo_ref[0, r0:r0 + rows, :] = xn
        hh = h2.astype(BF16)
        hl = (h2 - hh.astype(F32)).astype(BF16)
        _store_token_tiles(h_ref, h2, r0)
        all_logits.append(jnp.dot(hh, rwh, preferred_element_type=F32) + jnp.dot(hl, rwh, preferred_element_type=F32)
                          + jnp.dot(hh, rwl_ref[...], preferred_element_type=F32) + rb_ref[...])
    choices = [_select_experts(logits) for logits in all_logits]
    routed = jnp.concatenate([c[3] for c in choices], axis=0)
    row = lax.broadcasted_iota(jnp.int32, (tm, tm), 0)
    col = lax.broadcasted_iota(jnp.int32, (tm, tm), 1)
    earlier = jnp.where(col < row, 1.0, 0.0).astype(BF16)
    before = jnp.dot(earlier, routed.astype(BF16), preferred_element_type=F32)
    for part, choice in enumerate(choices):
        r0 = part * rows
        _store_route(choice, before[r0:r0 + rows], r0, gate_ref, idx_ref, rank_ref)
    cnt_ref[0] = jnp.sum(routed, axis=0, keepdims=True).astype(jnp.int32)


def _mixer_out(mixed, w_out, x, mod, ln_g, ln_b, router_w, router_b, *, alpha, tm, conv_w=None):
    bsz, seq, d = x.shape
    n_exp = router_w.shape[1]
    nt = seq // tm
    tile = pl.BlockSpec((1, tm, d), lambda bi, i: (bi, i, 0))
    ktile = pl.BlockSpec((1, tm, TOP_K), lambda bi, i: (bi, i, 0))
    row = pl.BlockSpec((1, d), lambda bi, i: (0, 0))
    conv = conv_w is not None
    if conv:
        bg, u = mixed
        halo = 16
        r = tm // halo
        last = seq // halo - 1
        args = [bg, u, u, u, jnp.pad(conv_w.astype(F32), ((0, MOD_ROWS - conv_w.shape[0]), (0, 0)))]
        specs = [tile, tile,
                 pl.BlockSpec((1, halo, d), lambda bi, i: (bi, jnp.maximum(i * r - 1, 0), 0)),
                 pl.BlockSpec((1, halo, d), lambda bi, i: (bi, jnp.minimum((i + 1) * r, last), 0)),
                 pl.BlockSpec((MOD_ROWS, d), lambda bi, i: (0, 0))]
    else:
        args, specs = [mixed], [tile]
    rw_hi = router_w.astype(BF16)
    rw_lo = (router_w - rw_hi.astype(F32)).astype(BF16)
    args += [w_out.astype(BF16), x, mod, ln_g[None, :], ln_b[None, :], rw_hi, rw_lo, router_b[None, :]]
    specs += [pl.BlockSpec((d, d), lambda bi, i: (0, 0)), tile,
              pl.BlockSpec((1, MOD_ROWS, d), lambda bi, i: (bi, 0, 0)), row, row,
              pl.BlockSpec((d, n_exp), lambda bi, i: (0, 0)), pl.BlockSpec((d, n_exp), lambda bi, i: (0, 0)),
              pl.BlockSpec((1, n_exp), lambda bi, i: (0, 0))]
    return pl.pallas_call(
        functools.partial(_mixer_out_kernel, conv=conv, alpha=alpha, seq=seq),
        grid=(bsz, nt),
        in_specs=specs,
        out_specs=[tile, pl.BlockSpec((tm * (d // LANES), LANES), lambda bi, i: (bi * nt + i, 0)),
                   ktile, ktile, ktile,
                   pl.BlockSpec((1, 1, n_exp), lambda bi, i: (bi * nt + i, 0, 0))],
        out_shape=[jax.ShapeDtypeStruct((bsz, seq, d), F32),
                   jax.ShapeDtypeStruct((bsz * seq * (d // LANES), LANES), F32),
                   jax.ShapeDtypeStruct((bsz, seq, TOP_K), F32), jax.ShapeDtypeStruct((bsz, seq, TOP_K), jnp.int32),
                   jax.ShapeDtypeStruct((bsz, seq, TOP_K), jnp.int32),
                   jax.ShapeDtypeStruct((bsz * nt, 1, n_exp), jnp.int32)],
        compiler_params=_cparams(("parallel", "parallel")),
        name="mixer_out",
    )(*args)


def _expert_kernel(be_ref, first_ref, used_ref, x_ref, wgu_ref, bgu_ref, wdn_ref, bdn_ref, o_ref,
                   wgu_bf, wdn_bf, *, ff, fc):
    i = pl.program_id(0)

    @pl.when(first_ref[i] == 1)
    def _():
        wgu_bf[...] = wgu_ref[0, 0].astype(BF16)
        wdn_bf[...] = wdn_ref[0, 0].astype(BF16)

    @pl.when(i < used_ref[0])
    def _():
        rt = wdn_bf.shape[1] // LANES
        bm = x_ref.shape[0] // rt
        xb = _load_token_tiles(x_ref, bm, rt).astype(BF16)
        acc = jnp.zeros((bm, rt * LANES), F32)
        for c in range(ff // fc):
            lo = c * fc
            gate = jnp.dot(xb, wgu_bf[:, lo:lo + fc], preferred_element_type=F32) + bgu_ref[0, 0, :, lo:lo + fc]
            up = jnp.dot(xb, wgu_bf[:, ff + lo:ff + lo + fc], preferred_element_type=F32) \
                + bgu_ref[0, 0, :, ff + lo:ff + lo + fc]
            gate = jnp.minimum(gate, SWIGLU_LIMIT)
            up = jnp.clip(up, -SWIGLU_LIMIT, SWIGLU_LIMIT)
            act = (up + 1.0) * gate * jax.nn.sigmoid(SWIGLU_ALPHA * gate)
            acc = acc + jnp.dot(act.astype(BF16), wdn_bf[lo:lo + fc, :], preferred_element_type=F32)
        _store_token_tiles(o_ref, acc + bdn_ref[0, 0])

    @pl.when(i >= used_ref[0])
    def _():
        o_ref[...] = jnp.zeros_like(o_ref)


def _expert_ffn(xs, blk_expert, blk_first, n_used, layer, w_gu, b_gu, w_down, b_down, *, bm):
    depth, n_exp, d, ff2 = w_gu.shape
    ff = ff2 // 2
    rt = d // LANES
    wmap = lambda i, be, fi, nu: (layer, be[i], 0, 0)
    grid_spec = pltpu.PrefetchScalarGridSpec(
        num_scalar_prefetch=3,
        grid=(xs.shape[0] // (bm * rt),),
        in_specs=[pl.BlockSpec((bm * rt, LANES), lambda i, be, fi, nu: (jnp.minimum(i, nu[0] - 1), 0)),
                  pl.BlockSpec((1, 1, d, ff2), wmap), pl.BlockSpec((1, 1, 1, ff2), wmap),
                  pl.BlockSpec((1, 1, ff, d), wmap), pl.BlockSpec((1, 1, 1, d), wmap)],
        out_specs=pl.BlockSpec((bm * rt, LANES), lambda i, be, fi, nu: (i, 0)),
        scratch_shapes=[pltpu.VMEM((d, ff2), BF16), pltpu.VMEM((ff, d), BF16)],
    )
    return pl.pallas_call(
        functools.partial(_expert_kernel, ff=ff, fc=min(ff, 512)),
        grid_spec=grid_spec,
        out_shape=jax.ShapeDtypeStruct(xs.shape, F32),
        compiler_params=_cparams(("arbitrary",)),
        name="expert_ffn",
    )(blk_expert, blk_first, n_used, xs, w_gu, b_gu.reshape(depth, n_exp, 1, ff2), w_down,
      b_down.reshape(depth, n_exp, 1, d))


def _route(idx, rank, cnt_tiles, tile_rows, bm):
    t = idx.shape[0]
    a = t * TOP_K
    n_exp = cnt_tiles.shape[1]
    counts = jnp.sum(cnt_tiles, axis=0)
    padded = (counts + bm - 1) // bm * bm
    pends = jnp.cumsum(padded)
    pstarts = pends - padded
    tile_off = pstarts[None, :] + jnp.cumsum(cnt_tiles, axis=0) - cnt_tiles
    off_tok, lo = [], 0
    for rows, n_tiles in tile_rows:
        part = tile_off[lo:lo + n_tiles]
        off_tok.append(jnp.broadcast_to(part[:, None, :], (n_tiles, rows, n_exp)).reshape(n_tiles * rows, n_exp))
        lo += n_tiles
    off_tok = jnp.concatenate(off_tok, axis=0)
    chosen = idx[:, :, None] == jnp.arange(n_exp, dtype=jnp.int32)[None, None, :]
    dest = jnp.sum(jnp.where(chosen, off_tok[:, None, :], 0), axis=-1) + rank
    nblk = -(-a // bm) + n_exp
    blk_start = jnp.arange(nblk, dtype=jnp.int32) * bm
    blk_expert = jnp.minimum(jnp.sum((blk_start[:, None] >= pends[None, :]).astype(jnp.int32), axis=1),
                             n_exp - 1)
    blk_first = jnp.concatenate([jnp.ones((1,), jnp.int32),
                                 (blk_expert[1:] != blk_expert[:-1]).astype(jnp.int32)])
    n_used = (pends[-1] // bm).astype(jnp.int32).reshape(1)
    last_blk = jnp.maximum(pends // bm - 1, 0).astype(jnp.int32)
    return dest, nblk, blk_expert, blk_first, n_used, last_blk


def _dispatch_kernel(*refs, rt, n_exp, nblk, tiles):
    n_s = len(tiles)
    lb_ref, nu_ref = refs[:2]
    dest_refs = refs[2:2 + n_s]
    h_refs = refs[2 + n_s:2 + 2 * n_s]
    xs_ref, zero_buf, sem, zero_sem = refs[2 + 2 * n_s:]
    i = pl.program_id(0)
    blk_rows = zero_buf.shape[0]

    def zero_copy(blk):
        return pltpu.make_async_copy(
            zero_buf, xs_ref.at[pl.ds(pl.multiple_of(blk * blk_rows, blk_rows), blk_rows), :], zero_sem)

    @pl.when(i == 0)
    def _():
        zero_buf[...] = jnp.zeros_like(zero_buf)
        todo = [(lb_ref[0], None)]
        todo += [(lb_ref[e], lb_ref[e] != lb_ref[e - 1]) for e in range(1, n_exp)]
        todo += [(nu_ref[0] + e, nu_ref[0] + e < nblk) for e in range(n_exp)]
        for blk, cond in todo:
            if cond is None:
                zero_copy(blk).start()
            else:
                pl.when(cond)(lambda blk=blk: zero_copy(blk).start())
        for _, cond in todo:
            if cond is None:
                zero_copy(0).wait()
            else:
                pl.when(cond)(lambda: zero_copy(0).wait())

    first = 0
    for dest_ref, h_ref, (tm, nt) in zip(dest_refs, h_refs, tiles):
        @pl.when((i >= first) & (i < first + nt))
        def _(dest_ref=dest_ref, h_ref=h_ref, tm=tm):
            def issue(j, carry):
                base = j * SUBLANES
                for s in range(SUBLANES):
                    src = h_ref.at[pl.ds(pl.multiple_of((base + s) * rt, rt), rt), :]
                    for kk in range(TOP_K):
                        dst = pl.multiple_of(dest_ref[0, 0, (base + s) * TOP_K + kk], rt)
                        pltpu.make_async_copy(src, xs_ref.at[pl.ds(dst, rt), :], sem).start(priority=kk % 2)
                return carry

            lax.fori_loop(0, tm // SUBLANES, issue, 0)
            for kk in range(TOP_K):
                pltpu.make_async_copy(h_ref, xs_ref.at[pl.ds(0, tm * rt), :], sem).wait()

        first += nt


def _dispatch(streams, last_blk, n_used, nblk, *, bm, rt):
    n_exp = last_blk.shape[0]
    tiles = [(tm, h.shape[0] // (tm * rt)) for h, _, tm in streams]
    dest_specs, h_specs, dests, hs = [], [], [], []
    first = 0
    for (h, dest_rows, tm), (_, nt) in zip(streams, tiles):
        tile_of = lambda i, lb, nu, first=first, nt=nt: jnp.clip(i - first, 0, nt - 1)
        dest_specs.append(pl.BlockSpec((1, 1, tm * TOP_K), lambda i, lb, nu, f=tile_of: (f(i, lb, nu), 0, 0),
                                       memory_space=pltpu.SMEM))
        h_specs.append(pl.BlockSpec((tm * rt, LANES), lambda i, lb, nu, f=tile_of: (f(i, lb, nu), 0)))
        dests.append(dest_rows.reshape(nt, 1, tm * TOP_K))
        hs.append(h)
        first += nt
    return pl.pallas_call(
        functools.partial(_dispatch_kernel, rt=rt, n_exp=n_exp, nblk=nblk, tiles=tiles),
        grid_spec=pltpu.PrefetchScalarGridSpec(
            num_scalar_prefetch=2, grid=(first,), in_specs=dest_specs + h_specs,
            out_specs=pl.BlockSpec(memory_space=pl.ANY),
            scratch_shapes=[pltpu.VMEM((bm * rt, LANES), F32), pltpu.SemaphoreType.DMA(()),
                            pltpu.SemaphoreType.DMA(())]),
        out_shape=jax.ShapeDtypeStruct((nblk * bm * rt, LANES), F32),
        compiler_params=_cparams(("arbitrary",)),
        name="dispatch_rows",
    )(last_blk, n_used, *dests, *hs)


def _combine_kernel(dest_ref, next_dest_ref, out_ref, gate_ref, x_ref, mod_ref, lng_ref, lnb_ref, o_ref,
                    y_buf, sem, *, alpha, n_tiles):
    n = pl.program_id(0)
    tm, d = x_ref.shape[1:]
    rt = d // LANES

    def request(table_ref, slot):
        def issue(j, carry):
            base = j * SUBLANES
            for s in range(SUBLANES):
                row = pl.multiple_of((base + s) * rt, rt)
                for kk in range(TOP_K):
                    src = pl.multiple_of(table_ref[0, 0, (base + s) * TOP_K + kk], rt)
                    pltpu.make_async_copy(out_ref.at[pl.ds(src, rt), :], y_buf.at[slot, kk, pl.ds(row, rt), :],
                                          sem.at[slot]).start(priority=kk % 2)
            return carry

        lax.fori_loop(0, tm // SUBLANES, issue, 0)

    slot = n % 2

    @pl.when(n == 0)
    def _():
        request(dest_ref, 0)

    @pl.when(n + 1 < n_tiles)
    def _():
        request(next_dest_ref, 1 - slot)

    for kk in range(TOP_K):
        pltpu.make_async_copy(out_ref.at[pl.ds(0, tm * rt), :], y_buf.at[slot, kk], sem.at[slot]).wait()
    g = gate_ref[0]
    fx = g[:, 0:1] * _load_token_tiles(y_buf.at[slot, 0], tm, rt)
    for kk in range(1, TOP_K):
        fx = fx + g[:, kk:kk + 1] * _load_token_tiles(y_buf.at[slot, kk], tm, rt)
    m = mod_ref[0]
    o_ref[0] = _layer_norm(alpha * x_ref[0] + m[5:6] * fx, lng_ref[...], lnb_ref[...])


def _combine(out, dest_rows, gates, x, mod, ln_g, ln_b, *, alpha, tm):
    bsz, seq, d = x.shape
    nt = seq // tm
    n_tiles = bsz * nt
    dest = dest_rows.reshape(n_tiles, 1, tm * TOP_K)
    tile = pl.BlockSpec((1, tm, d), lambda n: (n // nt, n % nt, 0))
    row = pl.BlockSpec((1, d), lambda n: (0, 0))
    return pl.pallas_call(
        functools.partial(_combine_kernel, alpha=alpha, n_tiles=n_tiles),
        grid=(n_tiles,),
        in_specs=[pl.BlockSpec((1, 1, tm * TOP_K), lambda n: (n, 0, 0), memory_space=pltpu.SMEM),
                  pl.BlockSpec((1, 1, tm * TOP_K), lambda n: (jnp.minimum(n + 1, n_tiles - 1), 0, 0),
                               memory_space=pltpu.SMEM),
                  pl.BlockSpec(memory_space=pl.ANY),
                  pl.BlockSpec((1, tm, TOP_K), lambda n: (n // nt, n % nt, 0)),
                  tile, pl.BlockSpec((1, MOD_ROWS, d), lambda n: (n // nt, 0, 0)), row, row],
        out_specs=tile,
        out_shape=jax.ShapeDtypeStruct((bsz, seq, d), F32),
        scratch_shapes=[pltpu.VMEM((2, TOP_K, tm * (d // LANES), LANES), F32), pltpu.SemaphoreType.DMA((2,))],
        compiler_params=_cparams(("arbitrary",)),
        name="combine_norm",
    )(dest, dest, out, gates, x, mod, ln_g[None, :], ln_b[None, :])


def _rope_tables(seq):
    t = jnp.arange(seq, dtype=jnp.int32)
    row = (t // GRID_W).astype(F32)
    col = (t % GRID_W).astype(F32)
    n_freq = HEAD_DIM // 4
    inv_freq = ROPE_THETA ** (-jnp.arange(n_freq, dtype=F32) / n_freq)
    ar, ac = row[:, None] * inv_freq, col[:, None] * inv_freq
    cos = jnp.concatenate([jnp.cos(ar), jnp.cos(ar), jnp.cos(ac), jnp.cos(ac)], axis=1)
    sin = jnp.concatenate([-jnp.sin(ar), jnp.sin(ar), -jnp.sin(ac), jnp.sin(ac)], axis=1)
    return jnp.tile(cos, (1, 2)), jnp.tile(sin, (1, 2))


def _tile(n, want):
    t = min(n, want)
    while n % t:
        t //= 2
    return t


def kernel(x, c, ctx, c_ctx, mod_w, mod_b, ln1_g, ln1_b, ln2_g, ln2_b, win_wqkv, win_bqkv, win_sink, win_wo,
           conv_win, conv_w, conv_wout, full_wqkv, full_qnorm, full_knorm, full_wo,
           router_w, router_b, expert_wgu, expert_bgu, expert_wdown, expert_bdown):
    bsz, seq, d = x.shape
    ctx_len = ctx.shape[1]
    depth = mod_w.shape[0]
    alpha = (2.0 * depth) ** 0.25
    tables = _rope_tables(seq)
    tm_x, tm_z = _tile(seq, 512), _tile(ctx_len, 512)
    tq_win = _tile(seq, 256)
    tq_full = _tile(seq, 512)
    bm = 768

    c_rows = jnp.zeros((2 * MOD_ROWS, d), F32).at[:bsz].set(c).at[bsz].set(c_ctx)
    mods = _modulation(c_rows, mod_w, mod_b)
    z = ctx
    for i in range(depth):
        kind, j = i % N_MIXERS, i // N_MIXERS
        need_ctx = i < depth - 1
        mod_x = jnp.pad(mods[i, :bsz].reshape(bsz, 6, d), ((0, 0), (0, MOD_ROWS - 6), (0, 0)))
        mod_z = jnp.broadcast_to(jnp.pad(mods[i, bsz].reshape(1, 6, d), ((0, 0), (0, MOD_ROWS - 6), (0, 0))),
                                 (bsz, MOD_ROWS, d))
        route_args = (router_w[i], router_b[i])
        ln1 = (ln1_g[i], ln1_b[i])
        oz = None
        if kind == 0:
            qx, kx, vx = _qkv_project(x, mod_x, win_wqkv[j], win_bqkv[j], tables=tables, tm=tm_x)
            qz, kz, vz = _qkv_project(z, mod_z, win_wqkv[j], win_bqkv[j], tm=tm_z)
            ox = _window_attention(qx, kx, vx, kz, vz, win_sink[j], tq=tq_win)
            if need_ctx:
                oz = _dense_attention(qz, kz, vz, win_sink[j], tq=ctx_len)
            w_out, conv_taps = win_wo[j], None
        elif kind == 1:
            ox = _conv_in_project(x, mod_x, conv_win[j], tm=tm_x)
            if need_ctx:
                oz = _conv_in_project(z, mod_z, conv_win[j], tm=tm_z)
            w_out, conv_taps = conv_wout[j], conv_w[j]
        else:
            zero_b = jnp.zeros((full_wqkv.shape[2],), F32)
            norms = (full_qnorm[j], full_knorm[j])
            qx, kx, vx = _qkv_project(x, mod_x, full_wqkv[j], zero_b, tables=tables, norms=norms, tm=tm_x)
            qz, kz, vz = _qkv_project(z, mod_z, full_wqkv[j], zero_b, norms=norms, tm=tm_z)
            k_all = jnp.concatenate([kx, kz], axis=1)
            v_all = jnp.concatenate([vx, vz], axis=1)
            ox = _dense_attention(qx, k_all, v_all, tq=tq_full)
            if need_ctx:
                oz = _dense_attention(qz, kz, vz, tq=ctx_len)
            w_out, conv_taps = full_wo[j], None

        x, hx, gx, ix, rx, cx = _mixer_out(ox, w_out, x, mod_x, *ln1, *route_args, alpha=alpha, tm=tm_x,
                                           conv_w=conv_taps)
        n_x = bsz * seq
        idx, rank, cnt = ix.reshape(n_x, TOP_K), rx.reshape(n_x, TOP_K), cx[:, 0]
        tile_rows = [(tm_x, n_x // tm_x)]
        if need_ctx:
            z, hz, gz, iz, rz, cz = _mixer_out(oz, w_out, z, mod_z, *ln1, *route_args, alpha=alpha, tm=tm_z,
                                               conv_w=conv_taps)
            n_z = bsz * ctx_len
            idx = jnp.concatenate([idx, iz.reshape(n_z, TOP_K)], axis=0)
            rank = jnp.concatenate([rank, rz.reshape(n_z, TOP_K)], axis=0)
            cnt = jnp.concatenate([cnt, cz[:, 0]], axis=0)
            tile_rows.append((tm_z, n_z // tm_z))

        dest, nblk, blk_expert, blk_first, n_used, last_blk = _route(idx, rank, cnt, tile_rows, bm)
        rt = d // LANES
        dest_rows = dest * rt
        streams = [(hx, dest_rows[:n_x], tm_x)]
        if need_ctx:
            streams.append((hz, dest_rows[n_x:], tm_z))
        xs = _dispatch(streams, last_blk, n_used, nblk, bm=bm, rt=rt)
        out = _expert_ffn(xs, blk_expert, blk_first, n_used, i,
                          expert_wgu, expert_bgu, expert_wdown, expert_bdown, bm=bm)
        x = _combine(out, dest_rows[:n_x], gx, x, mod_x, ln2_g[i], ln2_b[i], alpha=alpha, tm=tm_x)
        if need_ctx:
            z = _combine(out, dest_rows[n_x:], gz, z, mod_z, ln2_g[i], ln2_b[i], alpha=alpha, tm=tm_z)
    return x
```

```python
import functools

import jax
import jax.numpy as jnp
from jax import lax
from jax.experimental import pallas as pl
from jax.experimental.pallas import tpu as pltpu

HEAD_DIM = 64
GROUP = 4
GRID_W = 64
WINDOW = 128
ROPE_THETA = 10000.0
TOP_K = 4
N_MIXERS = 3
SWIGLU_ALPHA = 1.702
SWIGLU_LIMIT = 7.0
LN_EPS = 1e-5
RMS_EPS = 1e-6
NEG_INF = -1e30

LANES = 128
SUBLANES = 8
BF16_SUBLANES = 16
MXU_COLS = 256
MOD_ROWS = 8
VMEM_LIMIT = 56 * 1024 * 1024

F32 = jnp.float32
BF16 = jnp.bfloat16


def _cparams(sem):
    return pltpu.CompilerParams(dimension_semantics=sem, vmem_limit_bytes=VMEM_LIMIT)


def _mod_kernel(c_ref, w_ref, b_ref, o_ref):
    c = c_ref[...]
    s = c * jax.nn.sigmoid(c)
    o_ref[0] = jnp.dot(s, w_ref[0], preferred_element_type=F32, precision=lax.Precision.HIGHEST) + b_ref[0]


def _modulation(c_rows, mod_w, mod_b):
    depth, d, n = mod_w.shape
    r = c_rows.shape[0]
    tn = min(n, 1536)
    return pl.pallas_call(
        _mod_kernel,
        grid=(depth, n // tn),
        in_specs=[pl.BlockSpec((r, d), lambda l, j: (0, 0)),
                  pl.BlockSpec((1, d, tn), lambda l, j: (l, 0, j)),
                  pl.BlockSpec((1, 1, tn), lambda l, j: (l, 0, j))],
        out_specs=pl.BlockSpec((1, r, tn), lambda l, j: (l, 0, j)),
        out_shape=jax.ShapeDtypeStruct((depth, r, n), F32),
        compiler_params=_cparams(("arbitrary", "arbitrary")),
        name="modulation",
    )(c_rows, mod_w, mod_b.reshape(depth, 1, n))


def _swap_pairs(t):
    lane = lax.broadcasted_iota(jnp.int32, t.shape, 1)
    return jnp.where(lane % 32 < 16, pltpu.roll(t, LANES - 16, 1), pltpu.roll(t, 16, 1))


def _head_mean_sq(t, seg_ref):
    t2 = t * t
    hi = t2.astype(BF16)
    lo = (t2 - hi.astype(F32)).astype(BF16)
    seg = seg_ref[...]
    s = jnp.dot(hi, seg, preferred_element_type=F32) + jnp.dot(lo, seg, preferred_element_type=F32)
    return s * (1.0 / HEAD_DIM)


def _qkv_kernel(*refs, nq, nk, rope, qk_norm):
    x_ref, mod_ref, w_ref, b_ref = refs[:4]
    pos = 4
    if rope:
        cos_ref, sin_ref = refs[pos:pos + 2]
        pos += 2
    if qk_norm:
        seg_ref, qg_ref, kg_ref = refs[pos:pos + 3]
        pos += 3
    q_ref, k_ref, v_ref = refs[pos:pos + 3]
    m = mod_ref[0]
    h = (x_ref[0] * (1.0 + m[1:2]) + m[0:1]).astype(BF16)
    if rope:
        cos, sin = cos_ref[...], sin_ref[...]
    starts = range(0, nq + 2 * nk, MXU_COLS)
    wides = [jnp.dot(h, w_ref[:, lo:lo + MXU_COLS], preferred_element_type=F32) + b_ref[:, lo:lo + MXU_COLS]
             for lo in starts]
    if qk_norm:
        mean_sqs = [_head_mean_sq(wide, seg_ref) if lo < nq + nk else None for lo, wide in zip(starts, wides)]
        wides = [wide if ms is None else wide * lax.rsqrt(ms + RMS_EPS) * (qg_ref[...] if lo < nq else kg_ref[...])
                 for lo, wide, ms in zip(starts, wides, mean_sqs)]
    for lo, wide in zip(starts, wides):
        if lo >= nq + nk:
            v_ref[0, :, lo - nq - nk:lo - nq - nk + MXU_COLS] = wide.astype(BF16)
            continue
        is_q = lo < nq
        for half in range(MXU_COLS // LANES):
            t = wide[:, half * LANES:(half + 1) * LANES]
            at = lo + half * LANES
            if rope:
                t = t * cos + _swap_pairs(t) * sin
            if is_q:
                q_ref[0, :, at:at + LANES] = (t * HEAD_DIM ** -0.5).astype(BF16)
            else:
                k_ref[0, :, at - nq:at - nq + LANES] = t.astype(BF16)


def _dup_heads(w, n_heads):
    lead = w.shape[:-1]
    w = w.reshape(lead + (n_heads, 1, HEAD_DIM))
    return jnp.broadcast_to(w, lead + (n_heads, 2, HEAD_DIM)).reshape(lead + (n_heads * LANES,))


def _pad_heads(w, n_heads, fill):
    lead = w.shape[:-1]
    w = w.reshape(lead + (n_heads, HEAD_DIM))
    pad = jnp.full(lead + (n_heads, HEAD_DIM), fill, w.dtype)
    return jnp.concatenate([w, pad], axis=-1).reshape(lead + (n_heads * LANES,))


def _qkv_project(x, mod, w, b, *, tables=None, norms=None, tm):
    bsz, seq, d = x.shape
    n_kv = (w.shape[1] - d) // (2 * HEAD_DIM)
    nq, nk = d, n_kv * LANES
    wq, wk, wv = w[:, :d], w[:, d:d + n_kv * HEAD_DIM], w[:, d + n_kv * HEAD_DIM:]
    w_ext = jnp.concatenate([wq, _dup_heads(wk, n_kv), _pad_heads(wv, n_kv, 0.0)], axis=1).astype(BF16)
    bq, bk, bv = b[:d], b[d:d + n_kv * HEAD_DIM], b[d + n_kv * HEAD_DIM:]
    b_ext = jnp.concatenate([bq, _dup_heads(bk, n_kv), _pad_heads(bv, n_kv, 1.0)])[None, :].astype(F32)
    n = nq + 2 * nk
    args = [x, mod, w_ext, b_ext]
    specs = [pl.BlockSpec((1, tm, d), lambda bi, i: (bi, i, 0)),
             pl.BlockSpec((1, MOD_ROWS, d), lambda bi, i: (bi, 0, 0)),
             pl.BlockSpec((d, n), lambda bi, i: (0, 0)),
             pl.BlockSpec((1, n), lambda bi, i: (0, 0))]
    if tables is not None:
        args += list(tables)
        specs += [pl.BlockSpec((tm, LANES), lambda bi, i: (i, 0))] * 2
    if norms is not None:
        qn, kn = norms
        col = jnp.arange(MXU_COLS) // HEAD_DIM
        seg = (col[:, None] == col[None, :]).astype(BF16)
        reps = MXU_COLS // HEAD_DIM
        args += [seg, jnp.tile(qn, reps)[None, :].astype(F32), jnp.tile(kn, reps)[None, :].astype(F32)]
        specs += [pl.BlockSpec((MXU_COLS, MXU_COLS), lambda bi, i: (0, 0)),
                  pl.BlockSpec((1, MXU_COLS), lambda bi, i: (0, 0)),
                  pl.BlockSpec((1, MXU_COLS), lambda bi, i: (0, 0))]
    return pl.pallas_call(
        functools.partial(_qkv_kernel, nq=nq, nk=nk, rope=tables is not None, qk_norm=norms is not None),
        grid=(bsz, seq // tm),
        in_specs=specs,
        out_specs=[pl.BlockSpec((1, tm, nq), lambda bi, i: (bi, i, 0)),
                   pl.BlockSpec((1, tm, nk), lambda bi, i: (bi, i, 0)),
                   pl.BlockSpec((1, tm, nk), lambda bi, i: (bi, i, 0))],
        out_shape=[jax.ShapeDtypeStruct((bsz, seq, nq), BF16),
                   jax.ShapeDtypeStruct((bsz, seq, nk), BF16),
                   jax.ShapeDtypeStruct((bsz, seq, nk), BF16)],
        compiler_params=_cparams(("parallel", "parallel")),
        name="qkv_project",
    )(*args)


def _conv_in_kernel(x_ref, mod_ref, w_ref, bg_ref, u_ref, *, d):
    m = mod_ref[0]
    h = (x_ref[0] * (1.0 + m[1:2]) + m[0:1]).astype(BF16)
    for c in range(d // MXU_COLS):
        lo = c * MXU_COLS
        bg = jnp.dot(h, w_ref[:, lo:lo + MXU_COLS], preferred_element_type=F32)
        cg = jnp.dot(h, w_ref[:, d + lo:d + lo + MXU_COLS], preferred_element_type=F32)
        xv = jnp.dot(h, w_ref[:, 2 * d + lo:2 * d + lo + MXU_COLS], preferred_element_type=F32)
        bg_ref[0, :, lo:lo + MXU_COLS] = bg.astype(BF16)
        u_ref[0, :, lo:lo + MXU_COLS] = (cg * xv).astype(BF16)


def _conv_in_project(x, mod, w_in, *, tm):
    bsz, seq, d = x.shape
    return pl.pallas_call(
        functools.partial(_conv_in_kernel, d=d),
        grid=(bsz, seq // tm),
        in_specs=[pl.BlockSpec((1, tm, d), lambda bi, i: (bi, i, 0)),
                  pl.BlockSpec((1, MOD_ROWS, d), lambda bi, i: (bi, 0, 0)),
                  pl.BlockSpec((d, 3 * d), lambda bi, i: (0, 0))],
        out_specs=[pl.BlockSpec((1, tm, d), lambda bi, i: (bi, i, 0))] * 2,
        out_shape=[jax.ShapeDtypeStruct((bsz, seq, d), BF16)] * 2,
        compiler_params=_cparams(("parallel", "parallel")),
        name="conv_in_project",
    )(x, mod, w_in.astype(BF16))


def _qk(q, k):
    return lax.dot_general(q, k, (((1,), (1,)), ((), ())), preferred_element_type=F32)


def _scores(qg, kc, bias):
    s = _qk(qg, kc)
    return s if bias is None else s + bias


def _row_max(qg, chunks, sink):
    mx = None
    for kc, _, bias in chunks:
        s = _scores(qg, kc, bias)
        for j in range(s.shape[1] // LANES):
            part = s[:, j * LANES:(j + 1) * LANES]
            mx = part if mx is None else jnp.maximum(mx, part)
    m = jnp.max(mx, axis=1, keepdims=True)
    return m if sink is None else jnp.maximum(m, sink)


def _exp_pv(qg, chunks, m):
    acc = None
    for kc, vc, bias in chunks:
        p = jnp.exp(_scores(qg, kc, bias) - m).astype(BF16)
        d = jnp.dot(p, vc, preferred_element_type=F32)
        acc = d if acc is None else acc + d
    return acc


def _attend_groups(q_ref, o_ref, groups, lane):
    heads = []
    for col0, chunks, sinks in groups:
        for pair in range(GROUP // 2):
            qp = q_ref[0, :, col0 + pair * LANES:col0 + (pair + 1) * LANES]
            heads.append((jnp.where(lane < HEAD_DIM, qp, jnp.zeros_like(qp)), chunks, sinks[2 * pair]))
            heads.append((jnp.where(lane >= HEAD_DIM, qp, jnp.zeros_like(qp)), chunks, sinks[2 * pair + 1]))
    ms = [_row_max(qg, chunks, sink) for qg, chunks, sink in heads]
    accs = [_exp_pv(qg, chunks, m) for (qg, chunks, _), m in zip(heads, ms)]
    for gi, (col0, _, sinks) in enumerate(groups):
        for pair in range(GROUP // 2):
            e, o = gi * GROUP + 2 * pair, gi * GROUP + 2 * pair + 1
            a0, a1 = accs[e], accs[o]
            r0, r1 = pltpu.roll(a0, HEAD_DIM, 1), pltpu.roll(a1, HEAD_DIM, 1)
            d0, d1 = r0, a1
            if sinks[0] is not None:
                d0 = d0 + jnp.exp(sinks[2 * pair] - ms[e])
                d1 = d1 + jnp.exp(sinks[2 * pair + 1] - ms[o])
            out = jnp.where(lane < HEAD_DIM, a0 / d0, r1 / d1)
            o_ref[0, :, col0 + pair * LANES:col0 + (pair + 1) * LANES] = out.astype(o_ref.dtype)


def _win_attn_kernel(q_ref, kp_ref, kc_ref, kn_ref, kz_ref, vp_ref, vc_ref, vn_ref, vz_ref, sink_ref, o_ref,
                     *, tq, seq):
    i = pl.program_id(1)
    r = lax.broadcasted_iota(jnp.int32, (tq, WINDOW), 0)
    c = lax.broadcasted_iota(jnp.int32, (tq, WINDOW), 1)
    bias_prev = jnp.where((c >= r) & (i > 0), 0.0, NEG_INF)
    bias_next = jnp.where((r - c >= tq - WINDOW) & (i < seq // tq - 1), 0.0, NEG_INF)
    bias_cur = None
    if tq - 1 > WINDOW:
        rr = lax.broadcasted_iota(jnp.int32, (tq, tq), 0)
        cc = lax.broadcasted_iota(jnp.int32, (tq, tq), 1)
        bias_cur = jnp.where(jnp.abs(cc - rr) <= WINDOW, 0.0, NEG_INF)
    lane = lax.broadcasted_iota(jnp.int32, (tq, LANES), 1)
    groups = []
    for hk in range(kc_ref.shape[2] // LANES):
        lo = hk * LANES
        chunks = [(kp_ref[0, :, lo:lo + LANES], vp_ref[0, :, lo:lo + LANES], bias_prev),
                  (kc_ref[0, :, lo:lo + LANES], vc_ref[0, :, lo:lo + LANES], bias_cur),
                  (kn_ref[0, :, lo:lo + LANES], vn_ref[0, :, lo:lo + LANES], bias_next),
                  (kz_ref[0, :, lo:lo + LANES], vz_ref[0, :, lo:lo + LANES], None)]
        groups.append((hk * GROUP * HEAD_DIM, chunks, [sink_ref[hk, g] for g in range(GROUP)]))
    _attend_groups(q_ref, o_ref, groups, lane)


def _window_attention(q, k, v, kz, vz, sink, *, tq):
    bsz, seq, d = q.shape
    nk = k.shape[2]
    ctx = kz.shape[1]
    r = tq // WINDOW
    last = seq // WINDOW - 1
    cur = lambda bi, i: (bi, i, 0)
    prev = lambda bi, i: (bi, jnp.maximum(i * r - 1, 0), 0)
    nxt = lambda bi, i: (bi, jnp.minimum((i + 1) * r, last), 0)
    zmap = lambda bi, i: (bi, 0, 0)
    kv_specs = [pl.BlockSpec((1, WINDOW, nk), prev), pl.BlockSpec((1, tq, nk), cur),
                pl.BlockSpec((1, WINDOW, nk), nxt), pl.BlockSpec((1, ctx, nk), zmap)]
    return pl.pallas_call(
        functools.partial(_win_attn_kernel, tq=tq, seq=seq),
        grid=(bsz, seq // tq),
        in_specs=[pl.BlockSpec((1, tq, d), cur)] + kv_specs + kv_specs
                 + [pl.BlockSpec(memory_space=pltpu.SMEM)],
        out_specs=pl.BlockSpec((1, tq, d), cur),
        out_shape=jax.ShapeDtypeStruct((bsz, seq, d), BF16),
        compiler_params=_cparams(("parallel", "parallel")),
        name="window_attention",
    )(q, k, k, k, kz, v, v, v, vz, sink.reshape(nk // LANES, GROUP).astype(F32))


def _dense_attn_kernel(*refs, ck, has_sink):
    if has_sink:
        q_ref, k_ref, v_ref, sink_ref, o_ref = refs
    else:
        q_ref, k_ref, v_ref, o_ref = refs
    hk = pl.program_id(1)
    tq = q_ref.shape[1]
    chunks = [(k_ref[0, lo:lo + ck, :], v_ref[0, lo:lo + ck, :], None) for lo in range(0, k_ref.shape[1], ck)]
    lane = lax.broadcasted_iota(jnp.int32, (tq, LANES), 1)
    sinks = [sink_ref[hk, g] if has_sink else None for g in range(GROUP)]
    _attend_groups(q_ref, o_ref, [(0, chunks, sinks)], lane)


def _dense_attention(q, k, v, sink=None, *, tq):
    bsz, seq, d = q.shape
    n_keys = k.shape[1]
    n_kv = k.shape[2] // LANES
    ck = 256 if n_keys % 256 == 0 else LANES
    qmap = lambda bi, h, i: (bi, i, h)
    kmap = lambda bi, h, i: (bi, 0, h)
    args = [q, k, v]
    specs = [pl.BlockSpec((1, tq, GROUP * HEAD_DIM), qmap),
             pl.BlockSpec((1, n_keys, LANES), kmap), pl.BlockSpec((1, n_keys, LANES), kmap)]
    if sink is not None:
        args.append(sink.reshape(n_kv, GROUP).astype(F32))
        specs.append(pl.BlockSpec(memory_space=pltpu.SMEM))
    return pl.pallas_call(
        functools.partial(_dense_attn_kernel, ck=ck, has_sink=sink is not None),
        grid=(bsz, n_kv, seq // tq),
        in_specs=specs,
        out_specs=pl.BlockSpec((1, tq, GROUP * HEAD_DIM), qmap),
        out_shape=jax.ShapeDtypeStruct((bsz, seq, d), BF16),
        compiler_params=_cparams(("parallel", "parallel", "parallel")),
        name="dense_attention",
    )(*args)


def _store_token_tiles(ref, val, first=0):
    rows, d = val.shape
    rt = d // LANES
    for s in range(rt):
        ref[pl.ds(first * rt + s, rows, stride=rt), :] = val[:, s * LANES:(s + 1) * LANES]


def _load_token_tiles(ref, rows, rt):
    return jnp.concatenate([ref[pl.ds(s, rows, stride=rt), :] for s in range(rt)], axis=1)


def _layer_norm(r, g, b):
    mu = jnp.mean(r, axis=-1, keepdims=True)
    rc = r - mu
    var = jnp.mean(rc * rc, axis=-1, keepdims=True)
    return rc * lax.rsqrt(var + LN_EPS) * g + b


def _select_experts(logits):
    rows, n_exp = logits.shape
    lane_e = lax.broadcasted_iota(jnp.int32, (rows, n_exp), 1).astype(F32)
    work = logits
    sels, vals, idxs = [], [], []
    for _ in range(TOP_K):
        mk = jnp.max(work, axis=1, keepdims=True)
        ik = jnp.min(jnp.where(work == mk, lane_e, float(n_exp)), axis=1, keepdims=True)
        sel = lane_e == ik
        work = jnp.where(sel, -jnp.inf, work)
        sels.append(sel)
        vals.append(mk)
        idxs.append(ik)
    exps = [jnp.exp(v - vals[0]) for v in vals]
    denom = exps[0]
    for e in exps[1:]:
        denom = denom + e
    routed = sels[0].astype(F32)
    for sel in sels[1:]:
        routed = routed + sel.astype(F32)
    return sels, [e / denom for e in exps], idxs, routed


def _store_route(choice, before, first, gate_ref, idx_ref, rank_ref):
    sels, gate_cols, idx_cols, _ = choice
    rows = before.shape[0]
    lane_k = lax.broadcasted_iota(jnp.int32, (rows, TOP_K), 1)
    gates = jnp.zeros((rows, TOP_K), F32)
    idx = jnp.zeros((rows, TOP_K), F32)
    rank = jnp.zeros((rows, TOP_K), F32)
    for kk in range(TOP_K):
        here = lane_k == kk
        gates = jnp.where(here, gate_cols[kk], gates)
        idx = jnp.where(here, idx_cols[kk], idx)
        rank = jnp.where(here, jnp.sum(jnp.where(sels[kk], before, 0.0), axis=1, keepdims=True), rank)
    gate_ref[0, first:first + rows, :] = gates
    idx_ref[0, first:first + rows, :] = idx.astype(jnp.int32)
    rank_ref[0, first:first + rows, :] = rank.astype(jnp.int32)


def _mixer_out_kernel(*refs, conv, alpha, seq):
    if conv:
        bg_ref, u_ref, up_ref, un_ref, cw_ref = refs[:5]
        refs = refs[5:]
    else:
        o_ref = refs[0]
        refs = refs[1:]
    (w_ref, x_ref, mod_ref, lng_ref, lnb_ref, rwh_ref, rwl_ref, rb_ref,
     xo_ref, h_ref, gate_ref, idx_ref, rank_ref, cnt_ref) = refs
    if conv:
        i = pl.program_id(1)
        u = u_ref[0].astype(F32)
        tm = u.shape[0]
        row = lax.broadcasted_iota(jnp.int32, u.shape, 0)
        halo = up_ref.shape[1]
        before = jnp.where(i == 0, 0.0, up_ref[0, halo - 1:halo, :].astype(F32))
        after = jnp.where(i == seq // tm - 1, 0.0, un_ref[0, 0:1, :].astype(F32))
        u_prev = jnp.where(row == 0, before, pltpu.roll(u, 1, 0))
        u_next = jnp.where(row == tm - 1, after, pltpu.roll(u, tm - 1, 0))
        cw = cw_ref[...]
        y = cw[0:1] * u_prev + cw[1:2] * u + cw[2:3] * u_next
        mixed = (bg_ref[0].astype(F32) * y).astype(BF16)
    else:
        mixed = o_ref[0]
    m = mod_ref[0]
    tm = mixed.shape[0]
    n_parts = 2 if tm % (2 * MXU_COLS) == 0 else 1
    rows = tm // n_parts
    rwh = rwh_ref[...]
    starts = [part * rows for part in range(n_parts)]
    oxs = [jnp.dot(mixed[r0:r0 + rows], w_ref[...], preferred_element_type=F32) for r0 in starts]
    xns = [_layer_norm(alpha * x_ref[0, r0:r0 + rows, :] + m[2:3] * ox, lng_ref[...], lnb_ref[...])
           for r0, ox in zip(starts, oxs)]
    h2s = [xn * (1.0 + m[4:5]) + m[3:4] for xn in xns]
    all_logits = []
    for r0, xn, h2 in zip(starts, xns, h2s):
        xo_ref[0, r0:r0 + rows, :] = xn
        hh = h2.astype(BF16)
        hl = (h2 - hh.astype(F32)).astype(BF16)
        _store_token_tiles(h_ref, h2, r0)
        all_logits.append(jnp.dot(hh, rwh, preferred_element_type=F32) + jnp.dot(hl, rwh, preferred_element_type=F32)
                          + jnp.dot(hh, rwl_ref[...], preferred_element_type=F32) + rb_ref[...])
    choices = [_select_experts(logits) for logits in all_logits]
    routed = jnp.concatenate([c[3] for c in choices], axis=0)
    row = lax.broadcasted_iota(jnp.int32, (tm, tm), 0)
    col = lax.broadcasted_iota(jnp.int32, (tm, tm), 1)
    earlier = jnp.where(col < row, 1.0, 0.0).astype(BF16)
    before = jnp.dot(earlier, routed.astype(BF16), preferred_element_type=F32)
    for part, choice in enumerate(choices):
        r0 = part * rows
        _store_route(choice, before[r0:r0 + rows], r0, gate_ref, idx_ref, rank_ref)
    cnt_ref[0] = jnp.sum(routed, axis=0, keepdims=True).astype(jnp.int32)


def _mixer_out(mixed, w_out, x, mod, ln_g, ln_b, router_w, router_b, *, alpha, tm, conv_w=None):
    bsz, seq, d = x.shape
    n_exp = router_w.shape[1]
    nt = seq // tm
    tile = pl.BlockSpec((1, tm, d), lambda bi, i: (bi, i, 0))
    ktile = pl.BlockSpec((1, tm, TOP_K), lambda bi, i: (bi, i, 0))
    row = pl.BlockSpec((1, d), lambda bi, i: (0, 0))
    conv = conv_w is not None
    if conv:
        bg, u = mixed
        halo = BF16_SUBLANES
        r = tm // halo
        last = seq // halo - 1
        args = [bg, u, u, u, jnp.pad(conv_w.astype(F32), ((0, MOD_ROWS - conv_w.shape[0]), (0, 0)))]
        specs = [tile, tile,
                 pl.BlockSpec((1, halo, d), lambda bi, i: (bi, jnp.maximum(i * r - 1, 0), 0)),
                 pl.BlockSpec((1, halo, d), lambda bi, i: (bi, jnp.minimum((i + 1) * r, last), 0)),
                 pl.BlockSpec((MOD_ROWS, d), lambda bi, i: (0, 0))]
    else:
        args, specs = [mixed], [tile]
    rw_hi = router_w.astype(BF16)
    rw_lo = (router_w - rw_hi.astype(F32)).astype(BF16)
    args += [w_out.astype(BF16), x, mod, ln_g[None, :], ln_b[None, :], rw_hi, rw_lo, router_b[None, :]]
    specs += [pl.BlockSpec((d, d), lambda bi, i: (0, 0)), tile,
              pl.BlockSpec((1, MOD_ROWS, d), lambda bi, i: (bi, 0, 0)), row, row,
              pl.BlockSpec((d, n_exp), lambda bi, i: (0, 0)), pl.BlockSpec((d, n_exp), lambda bi, i: (0, 0)),
              pl.BlockSpec((1, n_exp), lambda bi, i: (0, 0))]
    return pl.pallas_call(
        functools.partial(_mixer_out_kernel, conv=conv, alpha=alpha, seq=seq),
        grid=(bsz, nt),
        in_specs=specs,
        out_specs=[tile, pl.BlockSpec((tm * (d // LANES), LANES), lambda bi, i: (bi * nt + i, 0)),
                   ktile, ktile, ktile,
                   pl.BlockSpec((1, 1, n_exp), lambda bi, i: (bi * nt + i, 0, 0))],
        out_shape=[jax.ShapeDtypeStruct((bsz, seq, d), F32),
                   jax.ShapeDtypeStruct((bsz * seq * (d // LANES), LANES), F32),
                   jax.ShapeDtypeStruct((bsz, seq, TOP_K), F32), jax.ShapeDtypeStruct((bsz, seq, TOP_K), jnp.int32),
                   jax.ShapeDtypeStruct((bsz, seq, TOP_K), jnp.int32),
                   jax.ShapeDtypeStruct((bsz * nt, 1, n_exp), jnp.int32)],
        compiler_params=_cparams(("parallel", "parallel")),
        name="mixer_out",
    )(*args)


def _expert_kernel(be_ref, first_ref, used_ref, x_ref, wgu_ref, bgu_ref, wdn_ref, bdn_ref, o_ref,
                   wgu_bf, wdn_bf, *, ff, fc):
    i = pl.program_id(0)

    @pl.when(first_ref[i] == 1)
    def _():
        wgu_bf[...] = wgu_ref[0, 0].astype(BF16)
        wdn_bf[...] = wdn_ref[0, 0].astype(BF16)

    @pl.when(i < used_ref[0])
    def _():
        rt = wdn_bf.shape[1] // LANES
        bm = x_ref.shape[0] // rt
        xb = _load_token_tiles(x_ref, bm, rt).astype(BF16)
        acc = jnp.zeros((bm, rt * LANES), F32)
        for c in range(ff // fc):
            lo = c * fc
            gate = jnp.dot(xb, wgu_bf[:, lo:lo + fc], preferred_element_type=F32) + bgu_ref[0, 0, :, lo:lo + fc]
            up = jnp.dot(xb, wgu_bf[:, ff + lo:ff + lo + fc], preferred_element_type=F32) \
                + bgu_ref[0, 0, :, ff + lo:ff + lo + fc]
            gate = jnp.minimum(gate, SWIGLU_LIMIT)
            up = jnp.clip(up, -SWIGLU_LIMIT, SWIGLU_LIMIT)
            act = (up + 1.0) * gate * jax.nn.sigmoid(SWIGLU_ALPHA * gate)
            acc = acc + jnp.dot(act.astype(BF16), wdn_bf[lo:lo + fc, :], preferred_element_type=F32)
        _store_token_tiles(o_ref, acc + bdn_ref[0, 0])

    @pl.when(i >= used_ref[0])
    def _():
        o_ref[...] = jnp.zeros_like(o_ref)


def _expert_ffn(xs, blk_expert, blk_first, n_used, layer, w_gu, b_gu, w_down, b_down, *, bm):
    depth, n_exp, d, ff2 = w_gu.shape
    ff = ff2 // 2
    rt = d // LANES
    wmap = lambda i, be, fi, nu: (layer, be[i], 0, 0)
    grid_spec = pltpu.PrefetchScalarGridSpec(
        num_scalar_prefetch=3,
        grid=(xs.shape[0] // (bm * rt),),
        in_specs=[pl.BlockSpec((bm * rt, LANES), lambda i, be, fi, nu: (jnp.minimum(i, nu[0] - 1), 0)),
                  pl.BlockSpec((1, 1, d, ff2), wmap), pl.BlockSpec((1, 1, 1, ff2), wmap),
                  pl.BlockSpec((1, 1, ff, d), wmap), pl.BlockSpec((1, 1, 1, d), wmap)],
        out_specs=pl.BlockSpec((bm * rt, LANES), lambda i, be, fi, nu: (i, 0)),
        scratch_shapes=[pltpu.VMEM((d, ff2), BF16), pltpu.VMEM((ff, d), BF16)],
    )
    return pl.pallas_call(
        functools.partial(_expert_kernel, ff=ff, fc=min(ff, 512)),
        grid_spec=grid_spec,
        out_shape=jax.ShapeDtypeStruct(xs.shape, F32),
        compiler_params=_cparams(("arbitrary",)),
        name="expert_ffn",
    )(blk_expert, blk_first, n_used, xs, w_gu, b_gu.reshape(depth, n_exp, 1, ff2), w_down,
      b_down.reshape(depth, n_exp, 1, d))


def _route(idx, rank, cnt_tiles, tile_rows, bm):
    t = idx.shape[0]
    a = t * TOP_K
    n_exp = cnt_tiles.shape[1]
    counts = jnp.sum(cnt_tiles, axis=0)
    padded = (counts + bm - 1) // bm * bm
    pends = jnp.cumsum(padded)
    pstarts = pends - padded
    tile_off = pstarts[None, :] + jnp.cumsum(cnt_tiles, axis=0) - cnt_tiles
    off_tok, lo = [], 0
    for rows, n_tiles in tile_rows:
        part = tile_off[lo:lo + n_tiles]
        off_tok.append(jnp.broadcast_to(part[:, None, :], (n_tiles, rows, n_exp)).reshape(n_tiles * rows, n_exp))
        lo += n_tiles
    off_tok = jnp.concatenate(off_tok, axis=0)
    chosen = idx[:, :, None] == jnp.arange(n_exp, dtype=jnp.int32)[None, None, :]
    dest = jnp.sum(jnp.where(chosen, off_tok[:, None, :], 0), axis=-1) + rank
    nblk = -(-a // bm) + n_exp
    blk_start = jnp.arange(nblk, dtype=jnp.int32) * bm
    blk_expert = jnp.minimum(jnp.sum((blk_start[:, None] >= pends[None, :]).astype(jnp.int32), axis=1),
                             n_exp - 1)
    blk_first = jnp.concatenate([jnp.ones((1,), jnp.int32),
                                 (blk_expert[1:] != blk_expert[:-1]).astype(jnp.int32)])
    n_used = (pends[-1] // bm).astype(jnp.int32).reshape(1)
    last_blk = jnp.maximum(pends // bm - 1, 0).astype(jnp.int32)
    return dest, nblk, blk_expert, blk_first, n_used, last_blk


def _dispatch_kernel(*refs, rt, n_exp, nblk, tiles):
    n_s = len(tiles)
    lb_ref, nu_ref = refs[:2]
    dest_refs = refs[2:2 + n_s]
    h_refs = refs[2 + n_s:2 + 2 * n_s]
    xs_ref, zero_buf, sem, zero_sem = refs[2 + 2 * n_s:]
    i = pl.program_id(0)
    blk_rows = zero_buf.shape[0]

    def zero_copy(blk):
        return pltpu.make_async_copy(
            zero_buf, xs_ref.at[pl.ds(pl.multiple_of(blk * blk_rows, blk_rows), blk_rows), :], zero_sem)

    @pl.when(i == 0)
    def _():
        zero_buf[...] = jnp.zeros_like(zero_buf)
        todo = [(lb_ref[0], None)]
        todo += [(lb_ref[e], lb_ref[e] != lb_ref[e - 1]) for e in range(1, n_exp)]
        todo += [(nu_ref[0] + e, nu_ref[0] + e < nblk) for e in range(n_exp)]
        for blk, cond in todo:
            if cond is None:
                zero_copy(blk).start()
            else:
                pl.when(cond)(lambda blk=blk: zero_copy(blk).start())
        for _, cond in todo:
            if cond is None:
                zero_copy(0).wait()
            else:
                pl.when(cond)(lambda: zero_copy(0).wait())

    first = 0
    for dest_ref, h_ref, (tm, nt) in zip(dest_refs, h_refs, tiles):
        @pl.when((i >= first) & (i < first + nt))
        def _(dest_ref=dest_ref, h_ref=h_ref, tm=tm):
            def issue(j, carry):
                base = j * SUBLANES
                for s in range(SUBLANES):
                    src = h_ref.at[pl.ds(pl.multiple_of((base + s) * rt, rt), rt), :]
                    for kk in range(TOP_K):
                        dst = pl.multiple_of(dest_ref[0, 0, (base + s) * TOP_K + kk], rt)
                        pltpu.make_async_copy(src, xs_ref.at[pl.ds(dst, rt), :], sem).start(priority=kk % 2)
                return carry

            lax.fori_loop(0, tm // SUBLANES, issue, 0)
            for kk in range(TOP_K):
                pltpu.make_async_copy(h_ref, xs_ref.at[pl.ds(0, tm * rt), :], sem).wait()

        first += nt


def _dispatch(streams, last_blk, n_used, nblk, *, bm, rt):
    n_exp = last_blk.shape[0]
    tiles = [(tm, h.shape[0] // (tm * rt)) for h, _, tm in streams]
    dest_specs, h_specs, dests, hs = [], [], [], []
    first = 0
    for (h, dest_rows, tm), (_, nt) in zip(streams, tiles):
        tile_of = lambda i, lb, nu, first=first, nt=nt: jnp.clip(i - first, 0, nt - 1)
        dest_specs.append(pl.BlockSpec((1, 1, tm * TOP_K), lambda i, lb, nu, f=tile_of: (f(i, lb, nu), 0, 0),
                                       memory_space=pltpu.SMEM))
        h_specs.append(pl.BlockSpec((tm * rt, LANES), lambda i, lb, nu, f=tile_of: (f(i, lb, nu), 0)))
        dests.append(dest_rows.reshape(nt, 1, tm * TOP_K))
        hs.append(h)
        first += nt
    return pl.pallas_call(
        functools.partial(_dispatch_kernel, rt=rt, n_exp=n_exp, nblk=nblk, tiles=tiles),
        grid_spec=pltpu.PrefetchScalarGridSpec(
            num_scalar_prefetch=2, grid=(first,), in_specs=dest_specs + h_specs,
            out_specs=pl.BlockSpec(memory_space=pl.ANY),
            scratch_shapes=[pltpu.VMEM((bm * rt, LANES), F32), pltpu.SemaphoreType.DMA(()),
                            pltpu.SemaphoreType.DMA(())]),
        out_shape=jax.ShapeDtypeStruct((nblk * bm * rt, LANES), F32),
        compiler_params=_cparams(("arbitrary",)),
        name="dispatch_rows",
    )(last_blk, n_used, *dests, *hs)


def _combine_kernel(dest_ref, next_dest_ref, out_ref, gate_ref, x_ref, mod_ref, lng_ref, lnb_ref, o_ref,
                    y_buf, sem, *, alpha, n_tiles):
    n = pl.program_id(0)
    tm, d = x_ref.shape[1:]
    rt = d // LANES

    def request(table_ref, slot):
        def issue(j, carry):
            base = j * SUBLANES
            for s in range(SUBLANES):
                row = pl.multiple_of((base + s) * rt, rt)
                for kk in range(TOP_K):
                    src = pl.multiple_of(table_ref[0, 0, (base + s) * TOP_K + kk], rt)
                    pltpu.make_async_copy(out_ref.at[pl.ds(src, rt), :], y_buf.at[slot, kk, pl.ds(row, rt), :],
                                          sem.at[slot]).start(priority=kk % 2)
            return carry

        lax.fori_loop(0, tm // SUBLANES, issue, 0)

    slot = n % 2

    @pl.when(n == 0)
    def _():
        request(dest_ref, 0)

    @pl.when(n + 1 < n_tiles)
    def _():
        request(next_dest_ref, 1 - slot)

    for kk in range(TOP_K):
        pltpu.make_async_copy(out_ref.at[pl.ds(0, tm * rt), :], y_buf.at[slot, kk], sem.at[slot]).wait()
    g = gate_ref[0]
    fx = g[:, 0:1] * _load_token_tiles(y_buf.at[slot, 0], tm, rt)
    for kk in range(1, TOP_K):
        fx = fx + g[:, kk:kk + 1] * _load_token_tiles(y_buf.at[slot, kk], tm, rt)
    m = mod_ref[0]
    o_ref[0] = _layer_norm(alpha * x_ref[0] + m[5:6] * fx, lng_ref[...], lnb_ref[...])


def _combine(out, dest_rows, gates, x, mod, ln_g, ln_b, *, alpha, tm):
    bsz, seq, d = x.shape
    nt = seq // tm
    n_tiles = bsz * nt
    dest = dest_rows.reshape(n_tiles, 1, tm * TOP_K)
    tile = pl.BlockSpec((1, tm, d), lambda n: (n // nt, n % nt, 0))
    row = pl.BlockSpec((1, d), lambda n: (0, 0))
    return pl.pallas_call(
        functools.partial(_combine_kernel, alpha=alpha, n_tiles=n_tiles),
        grid=(n_tiles,),
        in_specs=[pl.BlockSpec((1, 1, tm * TOP_K), lambda n: (n, 0, 0), memory_space=pltpu.SMEM),
                  pl.BlockSpec((1, 1, tm * TOP_K), lambda n: (jnp.minimum(n + 1, n_tiles - 1), 0, 0),
                               memory_space=pltpu.SMEM),
                  pl.BlockSpec(memory_space=pl.ANY),
                  pl.BlockSpec((1, tm, TOP_K), lambda n: (n // nt, n % nt, 0)),
                  tile, pl.BlockSpec((1, MOD_ROWS, d), lambda n: (n // nt, 0, 0)), row, row],
        out_specs=tile,
        out_shape=jax.ShapeDtypeStruct((bsz, seq, d), F32),
        scratch_shapes=[pltpu.VMEM((2, TOP_K, tm * (d // LANES), LANES), F32), pltpu.SemaphoreType.DMA((2,))],
        compiler_params=_cparams(("arbitrary",)),
        name="combine_norm",
    )(dest, dest, out, gates, x, mod, ln_g[None, :], ln_b[None, :])


def _rope_tables(seq):
    t = jnp.arange(seq, dtype=jnp.int32)
    row = (t // GRID_W).astype(F32)
    col = (t % GRID_W).astype(F32)
    n_freq = HEAD_DIM // 4
    inv_freq = ROPE_THETA ** (-jnp.arange(n_freq, dtype=F32) / n_freq)
    ar, ac = row[:, None] * inv_freq, col[:, None] * inv_freq
    cos = jnp.concatenate([jnp.cos(ar), jnp.cos(ar), jnp.cos(ac), jnp.cos(ac)], axis=1)
    sin = jnp.concatenate([-jnp.sin(ar), jnp.sin(ar), -jnp.sin(ac), jnp.sin(ac)], axis=1)
    return jnp.tile(cos, (1, 2)), jnp.tile(sin, (1, 2))


def _tile(n, want):
    t = min(n, want)
    while n % t:
        t //= 2
    return t


def kernel(x, c, ctx, c_ctx, mod_w, mod_b, ln1_g, ln1_b, ln2_g, ln2_b, win_wqkv, win_bqkv, win_sink, win_wo,
           conv_win, conv_w, conv_wout, full_wqkv, full_qnorm, full_knorm, full_wo,
           router_w, router_b, expert_wgu, expert_bgu, expert_wdown, expert_bdown):
    bsz, seq, d = x.shape
    ctx_len = ctx.shape[1]
    depth = mod_w.shape[0]
    alpha = (2.0 * depth) ** 0.25
    tables = _rope_tables(seq)
    tm_x, tm_z = _tile(seq, 512), _tile(ctx_len, 512)
    tq_win = _tile(seq, 256)
    tq_full = _tile(seq, 512)
    bm = 768

    c_rows = jnp.zeros((2 * MOD_ROWS, d), F32).at[:bsz].set(c).at[bsz].set(c_ctx)
    mods = _modulation(c_rows, mod_w, mod_b)
    z = ctx
    for i in range(depth):
        kind, j = i % N_MIXERS, i // N_MIXERS
        need_ctx = i < depth - 1
        mod_x = jnp.pad(mods[i, :bsz].reshape(bsz, 6, d), ((0, 0), (0, MOD_ROWS - 6), (0, 0)))
        mod_z = jnp.broadcast_to(jnp.pad(mods[i, bsz].reshape(1, 6, d), ((0, 0), (0, MOD_ROWS - 6), (0, 0))),
                                 (bsz, MOD_ROWS, d))
        route_args = (router_w[i], router_b[i])
        ln1 = (ln1_g[i], ln1_b[i])
        oz = None
        if kind == 0:
            qx, kx, vx = _qkv_project(x, mod_x, win_wqkv[j], win_bqkv[j], tables=tables, tm=tm_x)
            qz, kz, vz = _qkv_project(z, mod_z, win_wqkv[j], win_bqkv[j], tm=tm_z)
            ox = _window_attention(qx, kx, vx, kz, vz, win_sink[j], tq=tq_win)
            if need_ctx:
                oz = _dense_attention(qz, kz, vz, win_sink[j], tq=ctx_len)
            w_out, conv_taps = win_wo[j], None
        elif kind == 1:
            ox = _conv_in_project(x, mod_x, conv_win[j], tm=tm_x)
            if need_ctx:
                oz = _conv_in_project(z, mod_z, conv_win[j], tm=tm_z)
            w_out, conv_taps = conv_wout[j], conv_w[j]
        else:
            zero_b = jnp.zeros((full_wqkv.shape[2],), F32)
            norms = (full_qnorm[j], full_knorm[j])
            qx, kx, vx = _qkv_project(x, mod_x, full_wqkv[j], zero_b, tables=tables, norms=norms, tm=tm_x)
            qz, kz, vz = _qkv_project(z, mod_z, full_wqkv[j], zero_b, norms=norms, tm=tm_z)
            k_all = jnp.concatenate([kx, kz], axis=1)
            v_all = jnp.concatenate([vx, vz], axis=1)
            ox = _dense_attention(qx, k_all, v_all, tq=tq_full)
            if need_ctx:
                oz = _dense_attention(qz, kz, vz, tq=ctx_len)
            w_out, conv_taps = full_wo[j], None

        x, hx, gx, ix, rx, cx = _mixer_out(ox, w_out, x, mod_x, *ln1, *route_args, alpha=alpha, tm=tm_x,
                                           conv_w=conv_taps)
        n_x = bsz * seq
        idx, rank, cnt = ix.reshape(n_x, TOP_K), rx.reshape(n_x, TOP_K), cx[:, 0]
        tile_rows = [(tm_x, n_x // tm_x)]
        if need_ctx:
            z, hz, gz, iz, rz, cz = _mixer_out(oz, w_out, z, mod_z, *ln1, *route_args, alpha=alpha, tm=tm_z,
                                               conv_w=conv_taps)
            n_z = bsz * ctx_len
            idx = jnp.concatenate([idx, iz.reshape(n_z, TOP_K)], axis=0)
            rank = jnp.concatenate([rank, rz.reshape(n_z, TOP_K)], axis=0)
            cnt = jnp.concatenate([cnt, cz[:, 0]], axis=0)
            tile_rows.append((tm_z, n_z // tm_z))

        dest, nblk, blk_expert, blk_first, n_used, last_blk = _route(idx, rank, cnt, tile_rows, bm)
        rt = d // LANES
        dest_rows = dest * rt
        streams = [(hx, dest_rows[:n_x], tm_x)]
        if need_ctx:
            streams.append((hz, dest_rows[n_x:], tm_z))
        xs = _dispatch(streams, last_blk, n_used, nblk, bm=bm, rt=rt)
        out = _expert_ffn(xs, blk_expert, blk_first, n_used, i,
                          expert_wgu, expert_bgu, expert_wdown, expert_bdown, bm=bm)
        x = _combine(out, dest_rows[:n_x], gx, x, mod_x, ln2_g[i], ln2_b[i], alpha=alpha, tm=tm_x)
        if need_ctx:
            z = _combine(out, dest_rows[n_x:], gz, z, mod_z, ln2_g[i], ln2_b[i], alpha=alpha, tm=tm_z)
    return x
```

```python
import functools

import jax
import jax.numpy as jnp
from jax import lax
from jax.experimental import pallas as pl
from jax.experimental.pallas import tpu as pltpu

HEAD_DIM = 64
GROUP = 4
GRID_W = 64
WINDOW = 128
ROPE_THETA = 10000.0
TOP_K = 4
N_MIXERS = 3
SWIGLU_ALPHA = 1.702
SWIGLU_LIMIT = 7.0
LN_EPS = 1e-5
RMS_EPS = 1e-6
NEG_INF = -1e30

LANES = 128
SUBLANES = 8
BF16_SUBLANES = 16
MXU_COLS = 256
MOD_ROWS = 8
VMEM_LIMIT = 56 * 1024 * 1024

F32 = jnp.float32
BF16 = jnp.bfloat16


def _cparams(sem):
    return pltpu.CompilerParams(dimension_semantics=sem, vmem_limit_bytes=VMEM_LIMIT)


def _mod_kernel(c_ref, w_ref, b_ref, o_ref):
    c = c_ref[...]
    s = c * jax.nn.sigmoid(c)
    o_ref[0] = jnp.dot(s, w_ref[0], preferred_element_type=F32, precision=lax.Precision.HIGHEST) + b_ref[0]


def _modulation(c_rows, mod_w, mod_b):
    depth, d, n = mod_w.shape
    r = c_rows.shape[0]
    tn = min(n, 1536)
    return pl.pallas_call(
        _mod_kernel,
        grid=(depth, n // tn),
        in_specs=[pl.BlockSpec((r, d), lambda l, j: (0, 0)),
                  pl.BlockSpec((1, d, tn), lambda l, j: (l, 0, j)),
                  pl.BlockSpec((1, 1, tn), lambda l, j: (l, 0, j))],
        out_specs=pl.BlockSpec((1, r, tn), lambda l, j: (l, 0, j)),
        out_shape=jax.ShapeDtypeStruct((depth, r, n), F32),
        compiler_params=_cparams(("arbitrary", "arbitrary")),
        name="modulation",
    )(c_rows, mod_w, mod_b.reshape(depth, 1, n))


def _swap_pairs(t):
    lane = lax.broadcasted_iota(jnp.int32, t.shape, 1)
    return jnp.where(lane % 32 < 16, pltpu.roll(t, LANES - 16, 1), pltpu.roll(t, 16, 1))


def _head_mean_sq(t, seg_ref):
    t2 = t * t
    hi = t2.astype(BF16)
    lo = (t2 - hi.astype(F32)).astype(BF16)
    seg = seg_ref[...]
    s = jnp.dot(hi, seg, preferred_element_type=F32) + jnp.dot(lo, seg, preferred_element_type=F32)
    return s * (1.0 / HEAD_DIM)


def _qkv_kernel(*refs, nq, nk, rope, qk_norm):
    x_ref, mod_ref, w_ref, b_ref = refs[:4]
    pos = 4
    if rope:
        cos_ref, sin_ref = refs[pos:pos + 2]
        pos += 2
    if qk_norm:
        seg_ref, qg_ref, kg_ref = refs[pos:pos + 3]
        pos += 3
    q_ref, k_ref, v_ref = refs[pos:pos + 3]
    m = mod_ref[0]
    h = (x_ref[0] * (1.0 + m[1:2]) + m[0:1]).astype(BF16)
    if rope:
        cos, sin = cos_ref[...], sin_ref[...]
    starts = range(0, nq + 2 * nk, MXU_COLS)
    wides = [jnp.dot(h, w_ref[:, lo:lo + MXU_COLS], preferred_element_type=F32) + b_ref[:, lo:lo + MXU_COLS]
             for lo in starts]
    if qk_norm:
        mean_sqs = [_head_mean_sq(wide, seg_ref) if lo < nq + nk else None for lo, wide in zip(starts, wides)]
        wides = [wide if ms is None else wide * lax.rsqrt(ms + RMS_EPS) * (qg_ref[...] if lo < nq else kg_ref[...])
                 for lo, wide, ms in zip(starts, wides, mean_sqs)]
    for lo, wide in zip(starts, wides):
        if lo >= nq + nk:
            v_ref[0, :, lo - nq - nk:lo - nq - nk + MXU_COLS] = wide.astype(BF16)
            continue
        is_q = lo < nq
        for half in range(MXU_COLS // LANES):
            t = wide[:, half * LANES:(half + 1) * LANES]
            at = lo + half * LANES
            if rope:
                t = t * cos + _swap_pairs(t) * sin
            if is_q:
                q_ref[0, :, at:at + LANES] = (t * HEAD_DIM ** -0.5).astype(BF16)
            else:
                k_ref[0, :, at - nq:at - nq + LANES] = t.astype(BF16)


def _dup_heads(w, n_heads):
    lead = w.shape[:-1]
    w = w.reshape(lead + (n_heads, 1, HEAD_DIM))
    return jnp.broadcast_to(w, lead + (n_heads, 2, HEAD_DIM)).reshape(lead + (n_heads * LANES,))


def _pad_heads(w, n_heads, fill):
    lead = w.shape[:-1]
    w = w.reshape(lead + (n_heads, HEAD_DIM))
    pad = jnp.full(lead + (n_heads, HEAD_DIM), fill, w.dtype)
    return jnp.concatenate([w, pad], axis=-1).reshape(lead + (n_heads * LANES,))


def _qkv_project(x, mod, w, b, *, tables=None, norms=None, tm):
    bsz, seq, d = x.shape
    n_kv = (w.shape[1] - d) // (2 * HEAD_DIM)
    nq, nk = d, n_kv * LANES
    wq, wk, wv = w[:, :d], w[:, d:d + n_kv * HEAD_DIM], w[:, d + n_kv * HEAD_DIM:]
    w_ext = jnp.concatenate([wq, _dup_heads(wk, n_kv), _pad_heads(wv, n_kv, 0.0)], axis=1).astype(BF16)
    bq, bk, bv = b[:d], b[d:d + n_kv * HEAD_DIM], b[d + n_kv * HEAD_DIM:]
    b_ext = jnp.concatenate([bq, _dup_heads(bk, n_kv), _pad_heads(bv, n_kv, 1.0)])[None, :].astype(F32)
    n = nq + 2 * nk
    args = [x, mod, w_ext, b_ext]
    specs = [pl.BlockSpec((1, tm, d), lambda bi, i: (bi, i, 0)),
             pl.BlockSpec((1, MOD_ROWS, d), lambda bi, i: (bi, 0, 0)),
             pl.BlockSpec((d, n), lambda bi, i: (0, 0)),
             pl.BlockSpec((1, n), lambda bi, i: (0, 0))]
    if tables is not None:
        args += list(tables)
        specs += [pl.BlockSpec((tm, LANES), lambda bi, i: (i, 0))] * 2
    if norms is not None:
        qn, kn = norms
        col = jnp.arange(MXU_COLS) // HEAD_DIM
        seg = (col[:, None] == col[None, :]).astype(BF16)
        reps = MXU_COLS // HEAD_DIM
        args += [seg, jnp.tile(qn, reps)[None, :].astype(F32), jnp.tile(kn, reps)[None, :].astype(F32)]
        specs += [pl.BlockSpec((MXU_COLS, MXU_COLS), lambda bi, i: (0, 0)),
                  pl.BlockSpec((1, MXU_COLS), lambda bi, i: (0, 0)),
                  pl.BlockSpec((1, MXU_COLS), lambda bi, i: (0, 0))]
    return pl.pallas_call(
        functools.partial(_qkv_kernel, nq=nq, nk=nk, rope=tables is not None, qk_norm=norms is not None),
        grid=(bsz, seq // tm),
        in_specs=specs,
        out_specs=[pl.BlockSpec((1, tm, nq), lambda bi, i: (bi, i, 0)),
                   pl.BlockSpec((1, tm, nk), lambda bi, i: (bi, i, 0)),
                   pl.BlockSpec((1, tm, nk), lambda bi, i: (bi, i, 0))],
        out_shape=[jax.ShapeDtypeStruct((bsz, seq, nq), BF16),
                   jax.ShapeDtypeStruct((bsz, seq, nk), BF16),
                   jax.ShapeDtypeStruct((bsz, seq, nk), BF16)],
        compiler_params=_cparams(("parallel", "parallel")),
        name="qkv_project",
    )(*args)


def _conv_in_kernel(x_ref, mod_ref, w_ref, bg_ref, u_ref, *, d):
    m = mod_ref[0]
    h = (x_ref[0] * (1.0 + m[1:2]) + m[0:1]).astype(BF16)
    for c in range(d // MXU_COLS):
        lo = c * MXU_COLS
        bg = jnp.dot(h, w_ref[:, lo:lo + MXU_COLS], preferred_element_type=F32)
        cg = jnp.dot(h, w_ref[:, d + lo:d + lo + MXU_COLS], preferred_element_type=F32)
        xv = jnp.dot(h, w_ref[:, 2 * d + lo:2 * d + lo + MXU_COLS], preferred_element_type=F32)
        bg_ref[0, :, lo:lo + MXU_COLS] = bg.astype(BF16)
        u_ref[0, :, lo:lo + MXU_COLS] = (cg * xv).astype(BF16)


def _conv_in_project(x, mod, w_in, *, tm):
    bsz, seq, d = x.shape
    return pl.pallas_call(
        functools.partial(_conv_in_kernel, d=d),
        grid=(bsz, seq // tm),
        in_specs=[pl.BlockSpec((1, tm, d), lambda bi, i: (bi, i, 0)),
                  pl.BlockSpec((1, MOD_ROWS, d), lambda bi, i: (bi, 0, 0)),
                  pl.BlockSpec((d, 3 * d), lambda bi, i: (0, 0))],
        out_specs=[pl.BlockSpec((1, tm, d), lambda bi, i: (bi, i, 0))] * 2,
        out_shape=[jax.ShapeDtypeStruct((bsz, seq, d), BF16)] * 2,
        compiler_params=_cparams(("parallel", "parallel")),
        name="conv_in_project",
    )(x, mod, w_in.astype(BF16))


def _qk(q, k):
    return lax.dot_general(q, k, (((1,), (1,)), ((), ())), preferred_element_type=F32)


def _scores(qg, kc, bias):
    s = _qk(qg, kc)
    return s if bias is None else s + bias


def _row_max(qg, chunks, sink):
    mx = None
    for kc, _, bias in chunks:
        s = _scores(qg, kc, bias)
        for j in range(s.shape[1] // LANES):
            part = s[:, j * LANES:(j + 1) * LANES]
            mx = part if mx is None else jnp.maximum(mx, part)
    m = jnp.max(mx, axis=1, keepdims=True)
    return m if sink is None else jnp.maximum(m, sink)


def _exp_pv(qg, chunks, m):
    acc = None
    for kc, vc, bias in chunks:
        p = jnp.exp(_scores(qg, kc, bias) - m).astype(BF16)
        d = jnp.dot(p, vc, preferred_element_type=F32)
        acc = d if acc is None else acc + d
    return acc


def _attend_groups(q_ref, o_ref, groups, lane):
    heads = []
    for col0, chunks, sinks in groups:
        for pair in range(GROUP // 2):
            qp = q_ref[0, :, col0 + pair * LANES:col0 + (pair + 1) * LANES]
            heads.append((jnp.where(lane < HEAD_DIM, qp, jnp.zeros_like(qp)), chunks, sinks[2 * pair]))
            heads.append((jnp.where(lane >= HEAD_DIM, qp, jnp.zeros_like(qp)), chunks, sinks[2 * pair + 1]))
    ms = [_row_max(qg, chunks, sink) for qg, chunks, sink in heads]
    accs = [_exp_pv(qg, chunks, m) for (qg, chunks, _), m in zip(heads, ms)]
    for gi, (col0, _, sinks) in enumerate(groups):
        for pair in range(GROUP // 2):
            e, o = gi * GROUP + 2 * pair, gi * GROUP + 2 * pair + 1
            a0, a1 = accs[e], accs[o]
            r0, r1 = pltpu.roll(a0, HEAD_DIM, 1), pltpu.roll(a1, HEAD_DIM, 1)
            d0, d1 = r0, a1
            if sinks[0] is not None:
                d0 = d0 + jnp.exp(sinks[2 * pair] - ms[e])
                d1 = d1 + jnp.exp(sinks[2 * pair + 1] - ms[o])
            out = jnp.where(lane < HEAD_DIM, a0 / d0, r1 / d1)
            o_ref[0, :, col0 + pair * LANES:col0 + (pair + 1) * LANES] = out.astype(o_ref.dtype)


def _win_attn_kernel(q_ref, kp_ref, kc_ref, kn_ref, kz_ref, vp_ref, vc_ref, vn_ref, vz_ref, sink_ref, o_ref,
                     *, tq, seq):
    i = pl.program_id(1)
    r = lax.broadcasted_iota(jnp.int32, (tq, WINDOW), 0)
    c = lax.broadcasted_iota(jnp.int32, (tq, WINDOW), 1)
    bias_prev = jnp.where((c >= r) & (i > 0), 0.0, NEG_INF)
    bias_next = jnp.where((r - c >= tq - WINDOW) & (i < seq // tq - 1), 0.0, NEG_INF)
    bias_cur = None
    if tq - 1 > WINDOW:
        rr = lax.broadcasted_iota(jnp.int32, (tq, tq), 0)
        cc = lax.broadcasted_iota(jnp.int32, (tq, tq), 1)
        bias_cur = jnp.where(jnp.abs(cc - rr) <= WINDOW, 0.0, NEG_INF)
    lane = lax.broadcasted_iota(jnp.int32, (tq, LANES), 1)
    groups = []
    for hk in range(kc_ref.shape[2] // LANES):
        lo = hk * LANES
        chunks = [(kp_ref[0, :, lo:lo + LANES], vp_ref[0, :, lo:lo + LANES], bias_prev),
                  (kc_ref[0, :, lo:lo + LANES], vc_ref[0, :, lo:lo + LANES], bias_cur),
                  (kn_ref[0, :, lo:lo + LANES], vn_ref[0, :, lo:lo + LANES], bias_next),
                  (kz_ref[0, :, lo:lo + LANES], vz_ref[0, :, lo:lo + LANES], None)]
        groups.append((hk * GROUP * HEAD_DIM, chunks, [sink_ref[hk, g] for g in range(GROUP)]))
    _attend_groups(q_ref, o_ref, groups, lane)


def _window_attention(q, k, v, kz, vz, sink, *, tq):
    bsz, seq, d = q.shape
    nk = k.shape[2]
    ctx = kz.shape[1]
    r = tq // WINDOW
    last = seq // WINDOW - 1
    cur = lambda bi, i: (bi, i, 0)
    prev = lambda bi, i: (bi, jnp.maximum(i * r - 1, 0), 0)
    nxt = lambda bi, i: (bi, jnp.minimum((i + 1) * r, last), 0)
    zmap = lambda bi, i: (bi, 0, 0)
    kv_specs = [pl.BlockSpec((1, WINDOW, nk), prev), pl.BlockSpec((1, tq, nk), cur),
                pl.BlockSpec((1, WINDOW, nk), nxt), pl.BlockSpec((1, ctx, nk), zmap)]
    return pl.pallas_call(
        functools.partial(_win_attn_kernel, tq=tq, seq=seq),
        grid=(bsz, seq // tq),
        in_specs=[pl.BlockSpec((1, tq, d), cur)] + kv_specs + kv_specs
                 + [pl.BlockSpec(memory_space=pltpu.SMEM)],
        out_specs=pl.BlockSpec((1, tq, d), cur),
        out_shape=jax.ShapeDtypeStruct((bsz, seq, d), BF16),
        compiler_params=_cparams(("parallel", "parallel")),
        name="window_attention",
    )(q, k, k, k, kz, v, v, v, vz, sink.reshape(nk // LANES, GROUP).astype(F32))


def _dense_attn_kernel(*refs, ck, has_sink):
    if has_sink:
        q_ref, k_ref, v_ref, sink_ref, o_ref = refs
    else:
        q_ref, k_ref, v_ref, o_ref = refs
    hk = pl.program_id(1)
    tq = q_ref.shape[1]
    chunks = [(k_ref[0, lo:lo + ck, :], v_ref[0, lo:lo + ck, :], None) for lo in range(0, k_ref.shape[1], ck)]
    lane = lax.broadcasted_iota(jnp.int32, (tq, LANES), 1)
    sinks = [sink_ref[hk, g] if has_sink else None for g in range(GROUP)]
    _attend_groups(q_ref, o_ref, [(0, chunks, sinks)], lane)


def _dense_attention(q, k, v, sink=None, *, tq):
    bsz, seq, d = q.shape
    n_keys = k.shape[1]
    n_kv = k.shape[2] // LANES
    ck = 256 if n_keys % 256 == 0 else LANES
    qmap = lambda bi, h, i: (bi, i, h)
    kmap = lambda bi, h, i: (bi, 0, h)
    args = [q, k, v]
    specs = [pl.BlockSpec((1, tq, GROUP * HEAD_DIM), qmap),
             pl.BlockSpec((1, n_keys, LANES), kmap), pl.BlockSpec((1, n_keys, LANES), kmap)]
    if sink is not None:
        args.append(sink.reshape(n_kv, GROUP).astype(F32))
        specs.append(pl.BlockSpec(memory_space=pltpu.SMEM))
    return pl.pallas_call(
        functools.partial(_dense_attn_kernel, ck=ck, has_sink=sink is not None),
        grid=(bsz, n_kv, seq // tq),
        in_specs=specs,
        out_specs=pl.BlockSpec((1, tq, GROUP * HEAD_DIM), qmap),
        out_shape=jax.ShapeDtypeStruct((bsz, seq, d), BF16),
        compiler_params=_cparams(("parallel", "parallel", "parallel")),
        name="dense_attention",
    )(*args)


def _store_token_tiles(ref, val, first=0):
    rows, d = val.shape
    rt = d // LANES
    for s in range(rt):
        ref[pl.ds(first * rt + s, rows, stride=rt), :] = val[:, s * LANES:(s + 1) * LANES]


def _load_token_tiles(ref, rows, rt):
    return jnp.concatenate([ref[pl.ds(s, rows, stride=rt), :] for s in range(rt)], axis=1)


def _layer_norm(r, g, b):
    mu = jnp.mean(r, axis=-1, keepdims=True)
    rc = r - mu
    var = jnp.mean(rc * rc, axis=-1, keepdims=True)
    return rc * lax.rsqrt(var + LN_EPS) * g + b


def _select_experts(logits):
    rows, n_exp = logits.shape
    lane_e = lax.broadcasted_iota(jnp.int32, (rows, n_exp), 1).astype(F32)
    work = logits
    sels, vals, idxs = [], [], []
    for _ in range(TOP_K):
        mk = jnp.max(work, axis=1, keepdims=True)
        ik = jnp.min(jnp.where(work == mk, lane_e, float(n_exp)), axis=1, keepdims=True)
        sel = lane_e == ik
        work = jnp.where(sel, -jnp.inf, work)
        sels.append(sel)
        vals.append(mk)
        idxs.append(ik)
    exps = [jnp.exp(v - vals[0]) for v in vals]
    denom = exps[0]
    for e in exps[1:]:
        denom = denom + e
    routed = sels[0].astype(F32)
    for sel in sels[1:]:
        routed = routed + sel.astype(F32)
    return sels, [e / denom for e in exps], idxs, routed


def _store_route(choice, before, first, gate_ref, idx_ref, rank_ref):
    sels, gate_cols, idx_cols, _ = choice
    rows = before.shape[0]
    lane_k = lax.broadcasted_iota(jnp.int32, (rows, TOP_K), 1)
    gates = jnp.zeros((rows, TOP_K), F32)
    idx = jnp.zeros((rows, TOP_K), F32)
    rank = jnp.zeros((rows, TOP_K), F32)
    for kk in range(TOP_K):
        here = lane_k == kk
        gates = jnp.where(here, gate_cols[kk], gates)
        idx = jnp.where(here, idx_cols[kk], idx)
        rank = jnp.where(here, jnp.sum(jnp.where(sels[kk], before, 0.0), axis=1, keepdims=True), rank)
    gate_ref[0, first:first + rows, :] = gates
    idx_ref[0, first:first + rows, :] = idx.astype(jnp.int32)
    rank_ref[0, first:first + rows, :] = rank.astype(jnp.int32)


def _mixer_out_kernel(*refs, conv, alpha, seq):
    if conv:
        bg_ref, u_ref, up_ref, un_ref, cw_ref = refs[:5]
        refs = refs[5:]
    else:
        o_ref = refs[0]
        refs = refs[1:]
    (w_ref, x_ref, mod_ref, lng_ref, lnb_ref, rwh_ref, rwl_ref, rb_ref,
     xo_ref, h_ref, gate_ref, idx_ref, rank_ref, cnt_ref) = refs
    if conv:
        i = pl.program_id(1)
        u = u_ref[0].astype(F32)
        tm = u.shape[0]
        row = lax.broadcasted_iota(jnp.int32, u.shape, 0)
        halo = up_ref.shape[1]
        before = jnp.where(i == 0, 0.0, up_ref[0, halo - 1:halo, :].astype(F32))
        after = jnp.where(i == seq // tm - 1, 0.0, un_ref[0, 0:1, :].astype(F32))
        u_prev = jnp.where(row == 0, before, pltpu.roll(u, 1, 0))
        u_next = jnp.where(row == tm - 1, after, pltpu.roll(u, tm - 1, 0))
        cw = cw_ref[...]
        y = cw[0:1] * u_prev + cw[1:2] * u + cw[2:3] * u_next
        mixed = (bg_ref[0].astype(F32) * y).astype(BF16)
    else:
        mixed = o_ref[0]
    m = mod_ref[0]
    tm = mixed.shape[0]
    n_parts = 2 if tm % (2 * MXU_COLS) == 0 else 1
    rows = tm // n_parts
    rwh = rwh_ref[...]
    starts = [part * rows for part in range(n_parts)]
    oxs = [jnp.dot(mixed[r0:r0 + rows], w_ref[...], preferred_element_type=F32) for r0 in starts]
    xns = [_layer_norm(alpha * x_ref[0, r0:r0 + rows, :] + m[2:3] * ox, lng_ref[...], lnb_ref[...])
           for r0, ox in zip(starts, oxs)]
    h2s = [xn * (1.0 + m[4:5]) + m[3:4] for xn in xns]
    all_logits = []
    for r0, xn, h2 in zip(starts, xns, h2s):
        xo_ref[0, r0:r0 + rows, :] = xn
        hh = h2.astype(BF16)
        hl = (h2 - hh.astype(F32)).astype(BF16)
        _store_token_tiles(h_ref, h2, r0)
        all_logits.append(jnp.dot(hh, rwh, preferred_element_type=F32) + jnp.dot(hl, rwh, preferred_element_type=F32)
                          + jnp.dot(hh, rwl_ref[...], preferred_element_type=F32) + rb_ref[...])
    choices = [_select_experts(logits) for logits in all_logits]
    routed = jnp.concatenate([c[3] for c in choices], axis=0)
    row = lax.broadcasted_iota(jnp.int32, (tm, tm), 0)
    col = lax.broadcasted_iota(jnp.int32, (tm, tm), 1)
    earlier = jnp.where(col < row, 1.0, 0.0).astype(BF16)
    before = jnp.dot(earlier, routed.astype(BF16), preferred_element_type=F32)
    for part, choice in enumerate(choices):
        r0 = part * rows
        _store_route(choice, before[r0:r0 + rows], r0, gate_ref, idx_ref, rank_ref)
    cnt_ref[0] = jnp.sum(routed, axis=0, keepdims=True).astype(jnp.int32)


def _mixer_out(mixed, w_out, x, mod, ln_g, ln_b, router_w, router_b, *, alpha, tm, conv_w=None):
    bsz, seq, d = x.shape
    n_exp = router_w.shape[1]
    nt = seq // tm
    tile = pl.BlockSpec((1, tm, d), lambda bi, i: (bi, i, 0))
    ktile = pl.BlockSpec((1, tm, TOP_K), lambda bi, i: (bi, i, 0))
    row = pl.BlockSpec((1, d), lambda bi, i: (0, 0))
    conv = conv_w is not None
    if conv:
        bg, u = mixed
        halo = BF16_SUBLANES
        r = tm // halo
        last = seq // halo - 1
        args = [bg, u, u, u, jnp.pad(conv_w.astype(F32), ((0, MOD_ROWS - conv_w.shape[0]), (0, 0)))]
        specs = [tile, tile,
                 pl.BlockSpec((1, halo, d), lambda bi, i: (bi, jnp.maximum(i * r - 1, 0), 0)),
                 pl.BlockSpec((1, halo, d), lambda bi, i: (bi, jnp.minimum((i + 1) * r, last), 0)),
                 pl.BlockSpec((MOD_ROWS, d), lambda bi, i: (0, 0))]
    else:
        args, specs = [mixed], [tile]
    rw_hi = router_w.astype(BF16)
    rw_lo = (router_w - rw_hi.astype(F32)).astype(BF16)
    args += [w_out.astype(BF16), x, mod, ln_g[None, :], ln_b[None, :], rw_hi, rw_lo, router_b[None, :]]
    specs += [pl.BlockSpec((d, d), lambda bi, i: (0, 0)), tile,
              pl.BlockSpec((1, MOD_ROWS, d), lambda bi, i: (bi, 0, 0)), row, row,
              pl.BlockSpec((d, n_exp), lambda bi, i: (0, 0)), pl.BlockSpec((d, n_exp), lambda bi, i: (0, 0)),
              pl.BlockSpec((1, n_exp), lambda bi, i: (0, 0))]
    return pl.pallas_call(
        functools.partial(_mixer_out_kernel, conv=conv, alpha=alpha, seq=seq),
        grid=(bsz, nt),
        in_specs=specs,
        out_specs=[tile, pl.BlockSpec((tm * (d // LANES), LANES), lambda bi, i: (bi * nt + i, 0)),
                   ktile, ktile, ktile,
                   pl.BlockSpec((1, 1, n_exp), lambda bi, i: (bi * nt + i, 0, 0))],
        out_shape=[jax.ShapeDtypeStruct((bsz, seq, d), F32),
                   jax.ShapeDtypeStruct((bsz * seq * (d // LANES), LANES), F32),
                   jax.ShapeDtypeStruct((bsz, seq, TOP_K), F32), jax.ShapeDtypeStruct((bsz, seq, TOP_K), jnp.int32),
                   jax.ShapeDtypeStruct((bsz, seq, TOP_K), jnp.int32),
                   jax.ShapeDtypeStruct((bsz * nt, 1, n_exp), jnp.int32)],
        compiler_params=_cparams(("parallel", "parallel")),
        name="mixer_out",
    )(*args)


def _expert_kernel(be_ref, first_ref, used_ref, x_ref, wgu_ref, bgu_ref, wdn_ref, bdn_ref, o_ref,
                   wgu_bf, wdn_bf, *, ff, fc):
    i = pl.program_id(0)

    @pl.when(first_ref[i] == 1)
    def _():
        wgu_bf[...] = wgu_ref[0, 0].astype(BF16)
        wdn_bf[...] = wdn_ref[0, 0].astype(BF16)

    @pl.when(i < used_ref[0])
    def _():
        rt = wdn_bf.shape[1] // LANES
        bm = x_ref.shape[0] // rt
        xb = _load_token_tiles(x_ref, bm, rt).astype(BF16)
        acc = jnp.zeros((bm, rt * LANES), F32)
        for c in range(ff // fc):
            lo = c * fc
            gate = jnp.dot(xb, wgu_bf[:, lo:lo + fc], preferred_element_type=F32) + bgu_ref[0, 0, :, lo:lo + fc]
            up = jnp.dot(xb, wgu_bf[:, ff + lo:ff + lo + fc], preferred_element_type=F32) \
                + bgu_ref[0, 0, :, ff + lo:ff + lo + fc]
            gate = jnp.minimum(gate, SWIGLU_LIMIT)
            up = jnp.clip(up, -SWIGLU_LIMIT, SWIGLU_LIMIT)
            act = (up + 1.0) * gate * jax.nn.sigmoid(SWIGLU_ALPHA * gate)
            acc = acc + jnp.dot(act.astype(BF16), wdn_bf[lo:lo + fc, :], preferred_element_type=F32)
        _store_token_tiles(o_ref, acc + bdn_ref[0, 0])

    @pl.when(i >= used_ref[0])
    def _():
        o_ref[...] = jnp.zeros_like(o_ref)


def _expert_ffn(xs, blk_expert, blk_first, n_used, layer, w_gu, b_gu, w_down, b_down, *, bm):
    depth, n_exp, d, ff2 = w_gu.shape
    ff = ff2 // 2
    rt = d // LANES
    wmap = lambda i, be, fi, nu: (layer, be[i], 0, 0)
    grid_spec = pltpu.PrefetchScalarGridSpec(
        num_scalar_prefetch=3,
        grid=(xs.shape[0] // (bm * rt),),
        in_specs=[pl.BlockSpec((bm * rt, LANES), lambda i, be, fi, nu: (jnp.minimum(i, nu[0] - 1), 0)),
                  pl.BlockSpec((1, 1, d, ff2), wmap), pl.BlockSpec((1, 1, 1, ff2), wmap),
                  pl.BlockSpec((1, 1, ff, d), wmap), pl.BlockSpec((1, 1, 1, d), wmap)],
        out_specs=pl.BlockSpec((bm * rt, LANES), lambda i, be, fi, nu: (i, 0)),
        scratch_shapes=[pltpu.VMEM((d, ff2), BF16), pltpu.VMEM((ff, d), BF16)],
    )
    return pl.pallas_call(
        functools.partial(_expert_kernel, ff=ff, fc=min(ff, 512)),
        grid_spec=grid_spec,
        out_shape=jax.ShapeDtypeStruct(xs.shape, F32),
        compiler_params=_cparams(("arbitrary",)),
        name="expert_ffn",
    )(blk_expert, blk_first, n_used, xs, w_gu, b_gu.reshape(depth, n_exp, 1, ff2), w_down,
      b_down.reshape(depth, n_exp, 1, d))


def _route(idx, rank, cnt_tiles, tile_rows, bm):
    t = idx.shape[0]
    a = t * TOP_K
    n_exp = cnt_tiles.shape[1]
    counts = jnp.sum(cnt_tiles, axis=0)
    padded = (counts + bm - 1) // bm * bm
    pends = jnp.cumsum(padded)
    pstarts = pends - padded
    tile_off = pstarts[None, :] + jnp.cumsum(cnt_tiles, axis=0) - cnt_tiles
    off_tok, lo = [], 0
    for rows, n_tiles in tile_rows:
        part = tile_off[lo:lo + n_tiles]
        off_tok.append(jnp.broadcast_to(part[:, None, :], (n_tiles, rows, n_exp)).reshape(n_tiles * rows, n_exp))
        lo += n_tiles
    off_tok = jnp.concatenate(off_tok, axis=0)
    chosen = idx[:, :, None] == jnp.arange(n_exp, dtype=jnp.int32)[None, None, :]
    dest = jnp.sum(jnp.where(chosen, off_tok[:, None, :], 0), axis=-1) + rank
    nblk = -(-a // bm) + n_exp
    blk_start = jnp.arange(nblk, dtype=jnp.int32) * bm
    blk_expert = jnp.minimum(jnp.sum((blk_start[:, None] >= pends[None, :]).astype(jnp.int32), axis=1),
                             n_exp - 1)
    blk_first = jnp.concatenate([jnp.ones((1,), jnp.int32),
                                 (blk_expert[1:] != blk_expert[:-1]).astype(jnp.int32)])
    n_used = (pends[-1] // bm).astype(jnp.int32).reshape(1)
    last_blk = jnp.maximum(pends // bm - 1, 0).astype(jnp.int32)
    return dest, nblk, blk_expert, blk_first, n_used, last_blk


def _dispatch_kernel(*refs, rt, n_exp, nblk, tiles):
    n_s = len(tiles)
    lb_ref, nu_ref = refs[:2]
    dest_refs = refs[2:2 + n_s]
    h_refs = refs[2 + n_s:2 + 2 * n_s]
    raw_refs = refs[2 + 2 * n_s:2 + 3 * n_s]
    xs_ref, zero_buf, sem, zero_sem = refs[2 + 3 * n_s:]
    i = pl.program_id(0)
    blk_rows = zero_buf.shape[0]

    def zero_copy(blk):
        return pltpu.make_async_copy(
            zero_buf, xs_ref.at[pl.ds(pl.multiple_of(blk * blk_rows, blk_rows), blk_rows), :], zero_sem)

    @pl.when(i == 0)
    def _():
        zero_buf[...] = jnp.zeros_like(zero_buf)
        todo = [(lb_ref[0], None)]
        todo += [(lb_ref[e], lb_ref[e] != lb_ref[e - 1]) for e in range(1, n_exp)]
        todo += [(nu_ref[0] + e, nu_ref[0] + e < nblk) for e in range(n_exp)]
        for blk, cond in todo:
            if cond is None:
                zero_copy(blk).start()
            else:
                pl.when(cond)(lambda blk=blk: zero_copy(blk).start())
        for _, cond in todo:
            if cond is None:
                zero_copy(0).wait()
            else:
                pl.when(cond)(lambda: zero_copy(0).wait())

    first = 0
    for dest_ref, h_ref, raw_ref, (tm, nt) in zip(dest_refs, h_refs, raw_refs, tiles):
        @pl.when((i >= first) & (i < first + nt))
        def _(dest_ref=dest_ref, h_ref=h_ref, raw_ref=raw_ref, tm=tm, first=first):
            tile_row = (i - first) * tm

            def issue(j, carry):
                base = j * SUBLANES
                for s in range(SUBLANES):
                    src = h_ref.at[pl.ds(pl.multiple_of((base + s) * rt, rt), rt), :]
                    far = raw_ref.at[pl.ds(pl.multiple_of((tile_row + base + s) * rt, rt), rt), :]
                    for kk in range(TOP_K):
                        dst = xs_ref.at[pl.ds(pl.multiple_of(dest_ref[0, 0, (base + s) * TOP_K + kk], rt), rt), :]
                        if kk < TOP_K // 2:
                            pltpu.make_async_copy(src, dst, sem).start(priority=kk % 2)
                        else:
                            pltpu.make_async_copy(far, dst, sem).start()
                return carry

            lax.fori_loop(0, tm // SUBLANES, issue, 0)
            for kk in range(TOP_K):
                pltpu.make_async_copy(h_ref, xs_ref.at[pl.ds(0, tm * rt), :], sem).wait()

        first += nt


def _dispatch(streams, last_blk, n_used, nblk, *, bm, rt):
    n_exp = last_blk.shape[0]
    tiles = [(tm, h.shape[0] // (tm * rt)) for h, _, tm in streams]
    dest_specs, h_specs, dests, hs = [], [], [], []
    first = 0
    for (h, dest_rows, tm), (_, nt) in zip(streams, tiles):
        tile_of = lambda i, lb, nu, first=first, nt=nt: jnp.clip(i - first, 0, nt - 1)
        dest_specs.append(pl.BlockSpec((1, 1, tm * TOP_K), lambda i, lb, nu, f=tile_of: (f(i, lb, nu), 0, 0),
                                       memory_space=pltpu.SMEM))
        h_specs.append(pl.BlockSpec((tm * rt, LANES), lambda i, lb, nu, f=tile_of: (f(i, lb, nu), 0)))
        dests.append(dest_rows.reshape(nt, 1, tm * TOP_K))
        hs.append(h)
        first += nt
    return pl.pallas_call(
        functools.partial(_dispatch_kernel, rt=rt, n_exp=n_exp, nblk=nblk, tiles=tiles),
        grid_spec=pltpu.PrefetchScalarGridSpec(
            num_scalar_prefetch=2, grid=(first,),
            in_specs=dest_specs + h_specs + [pl.BlockSpec(memory_space=pl.ANY)] * len(hs),
            out_specs=pl.BlockSpec(memory_space=pl.ANY),
            scratch_shapes=[pltpu.VMEM((bm * rt, LANES), F32), pltpu.SemaphoreType.DMA(()),
                            pltpu.SemaphoreType.DMA(())]),
        out_shape=jax.ShapeDtypeStruct((nblk * bm * rt, LANES), F32),
        compiler_params=_cparams(("arbitrary",)),
        name="dispatch_rows",
    )(last_blk, n_used, *dests, *hs, *hs)


def _combine_kernel(dest_ref, next_dest_ref, out_ref, gate_ref, x_ref, mod_ref, lng_ref, lnb_ref, o_ref,
                    y_buf, sem, *, alpha, n_tiles):
    n = pl.program_id(0)
    tm, d = x_ref.shape[1:]
    rt = d // LANES

    def request(table_ref, slot):
        def issue(j, carry):
            base = j * SUBLANES
            for s in range(SUBLANES):
                row = pl.multiple_of((base + s) * rt, rt)
                for kk in range(TOP_K):
                    src = pl.multiple_of(table_ref[0, 0, (base + s) * TOP_K + kk], rt)
                    pltpu.make_async_copy(out_ref.at[pl.ds(src, rt), :], y_buf.at[slot, kk, pl.ds(row, rt), :],
                                          sem.at[slot]).start(priority=kk % 2)
            return carry

        lax.fori_loop(0, tm // SUBLANES, issue, 0)

    slot = n % 2

    @pl.when(n == 0)
    def _():
        request(dest_ref, 0)

    @pl.when(n + 1 < n_tiles)
    def _():
        request(next_dest_ref, 1 - slot)

    for kk in range(TOP_K):
        pltpu.make_async_copy(out_ref.at[pl.ds(0, tm * rt), :], y_buf.at[slot, kk], sem.at[slot]).wait()
    g = gate_ref[0]
    fx = g[:, 0:1] * _load_token_tiles(y_buf.at[slot, 0], tm, rt)
    for kk in range(1, TOP_K):
        fx = fx + g[:, kk:kk + 1] * _load_token_tiles(y_buf.at[slot, kk], tm, rt)
    m = mod_ref[0]
    o_ref[0] = _layer_norm(alpha * x_ref[0] + m[5:6] * fx, lng_ref[...], lnb_ref[...])


def _combine(out, dest_rows, gates, x, mod, ln_g, ln_b, *, alpha, tm):
    bsz, seq, d = x.shape
    nt = seq // tm
    n_tiles = bsz * nt
    dest = dest_rows.reshape(n_tiles, 1, tm * TOP_K)
    tile = pl.BlockSpec((1, tm, d), lambda n: (n // nt, n % nt, 0))
    row = pl.BlockSpec((1, d), lambda n: (0, 0))
    return pl.pallas_call(
        functools.partial(_combine_kernel, alpha=alpha, n_tiles=n_tiles),
        grid=(n_tiles,),
        in_specs=[pl.BlockSpec((1, 1, tm * TOP_K), lambda n: (n, 0, 0), memory_space=pltpu.SMEM),
                  pl.BlockSpec((1, 1, tm * TOP_K), lambda n: (jnp.minimum(n + 1, n_tiles - 1), 0, 0),
                               memory_space=pltpu.SMEM),
                  pl.BlockSpec(memory_space=pl.ANY),
                  pl.BlockSpec((1, tm, TOP_K), lambda n: (n // nt, n % nt, 0)),
                  tile, pl.BlockSpec((1, MOD_ROWS, d), lambda n: (n // nt, 0, 0)), row, row],
        out_specs=tile,
        out_shape=jax.ShapeDtypeStruct((bsz, seq, d), F32),
        scratch_shapes=[pltpu.VMEM((2, TOP_K, tm * (d // LANES), LANES), F32), pltpu.SemaphoreType.DMA((2,))],
        compiler_params=_cparams(("arbitrary",)),
        name="combine_norm",
    )(dest, dest, out, gates, x, mod, ln_g[None, :], ln_b[None, :])


def _rope_tables(seq):
    t = jnp.arange(seq, dtype=jnp.int32)
    row = (t // GRID_W).astype(F32)
    col = (t % GRID_W).astype(F32)
    n_freq = HEAD_DIM // 4
    inv_freq = ROPE_THETA ** (-jnp.arange(n_freq, dtype=F32) / n_freq)
    ar, ac = row[:, None] * inv_freq, col[:, None] * inv_freq
    cos = jnp.concatenate([jnp.cos(ar), jnp.cos(ar), jnp.cos(ac), jnp.cos(ac)], axis=1)
    sin = jnp.concatenate([-jnp.sin(ar), jnp.sin(ar), -jnp.sin(ac), jnp.sin(ac)], axis=1)
    return jnp.tile(cos, (1, 2)), jnp.tile(sin, (1, 2))


def _tile(n, want):
    t = min(n, want)
    while n % t:
        t //= 2
    return t


def kernel(x, c, ctx, c_ctx, mod_w, mod_b, ln1_g, ln1_b, ln2_g, ln2_b, win_wqkv, win_bqkv, win_sink, win_wo,
           conv_win, conv_w, conv_wout, full_wqkv, full_qnorm, full_knorm, full_wo,
           router_w, router_b, expert_wgu, expert_bgu, expert_wdown, expert_bdown):
    bsz, seq, d = x.shape
    ctx_len = ctx.shape[1]
    depth = mod_w.shape[0]
    alpha = (2.0 * depth) ** 0.25
    tables = _rope_tables(seq)
    tm_x, tm_z = _tile(seq, 512), _tile(ctx_len, 512)
    tq_win = _tile(seq, 256)
    tq_full = _tile(seq, 512)
    bm = 768

    c_rows = jnp.zeros((2 * MOD_ROWS, d), F32).at[:bsz].set(c).at[bsz].set(c_ctx)
    mods = _modulation(c_rows, mod_w, mod_b)
    z = ctx
    for i in range(depth):
        kind, j = i % N_MIXERS, i // N_MIXERS
        need_ctx = i < depth - 1
        mod_x = jnp.pad(mods[i, :bsz].reshape(bsz, 6, d), ((0, 0), (0, MOD_ROWS - 6), (0, 0)))
        mod_z = jnp.broadcast_to(jnp.pad(mods[i, bsz].reshape(1, 6, d), ((0, 0), (0, MOD_ROWS - 6), (0, 0))),
                                 (bsz, MOD_ROWS, d))
        route_args = (router_w[i], router_b[i])
        ln1 = (ln1_g[i], ln1_b[i])
        oz = None
        if kind == 0:
            qx, kx, vx = _qkv_project(x, mod_x, win_wqkv[j], win_bqkv[j], tables=tables, tm=tm_x)
            qz, kz, vz = _qkv_project(z, mod_z, win_wqkv[j], win_bqkv[j], tm=tm_z)
            ox = _window_attention(qx, kx, vx, kz, vz, win_sink[j], tq=tq_win)
            if need_ctx:
                oz = _dense_attention(qz, kz, vz, win_sink[j], tq=ctx_len)
            w_out, conv_taps = win_wo[j], None
        elif kind == 1:
            ox = _conv_in_project(x, mod_x, conv_win[j], tm=tm_x)
            if need_ctx:
                oz = _conv_in_project(z, mod_z, conv_win[j], tm=tm_z)
            w_out, conv_taps = conv_wout[j], conv_w[j]
        else:
            zero_b = jnp.zeros((full_wqkv.shape[2],), F32)
            norms = (full_qnorm[j], full_knorm[j])
            qx, kx, vx = _qkv_project(x, mod_x, full_wqkv[j], zero_b, tables=tables, norms=norms, tm=tm_x)
            qz, kz, vz = _qkv_project(z, mod_z, full_wqkv[j], zero_b, norms=norms, tm=tm_z)
            k_all = jnp.concatenate([kx, kz], axis=1)
            v_all = jnp.concatenate([vx, vz], axis=1)
            ox = _dense_attention(qx, k_all, v_all, tq=tq_full)
            if need_ctx:
                oz = _dense_attention(qz, kz, vz, tq=ctx_len)
            w_out, conv_taps = full_wo[j], None

        x, hx, gx, ix, rx, cx = _mixer_out(ox, w_out, x, mod_x, *ln1, *route_args, alpha=alpha, tm=tm_x,
                                           conv_w=conv_taps)
        n_x = bsz * seq
        idx, rank, cnt = ix.reshape(n_x, TOP_K), rx.reshape(n_x, TOP_K), cx[:, 0]
        tile_rows = [(tm_x, n_x // tm_x)]
        if need_ctx:
            z, hz, gz, iz, rz, cz = _mixer_out(oz, w_out, z, mod_z, *ln1, *route_args, alpha=alpha, tm=tm_z,
                                               conv_w=conv_taps)
            n_z = bsz * ctx_len
            idx = jnp.concatenate([idx, iz.reshape(n_z, TOP_K)], axis=0)
            rank = jnp.concatenate([rank, rz.reshape(n_z, TOP_K)], axis=0)
            cnt = jnp.concatenate([cnt, cz[:, 0]], axis=0)
            tile_rows.append((tm_z, n_z // tm_z))

        dest, nblk, blk_expert, blk_first, n_used, last_blk = _route(idx, rank, cnt, tile_rows, bm)
        rt = d // LANES
        dest_rows = dest * rt
        streams = [(hx, dest_rows[:n_x], tm_x)]
        if need_ctx:
            streams.append((hz, dest_rows[n_x:], tm_z))
        xs = _dispatch(streams, last_blk, n_used, nblk, bm=bm, rt=rt)
        out = _expert_ffn(xs, blk_expert, blk_first, n_used, i,
                          expert_wgu, expert_bgu, expert_wdown, expert_bdown, bm=bm)
        x = _combine(out, dest_rows[:n_x], gx, x, mod_x, ln2_g[i], ln2_b[i], alpha=alpha, tm=tm_x)
        if need_ctx:
            z = _combine(out, dest_rows[n_x:], gz, z, mod_z, ln2_g[i], ln2_b[i], alpha=alpha, tm=tm_z)
    return x
```

```python
import functools

import jax
import jax.numpy as jnp
from jax import lax
from jax.experimental import pallas as pl
from jax.experimental.pallas import tpu as pltpu

HEAD_DIM = 64
GROUP = 4
GRID_W = 64
WINDOW = 128
ROPE_THETA = 10000.0
TOP_K = 4
N_MIXERS = 3
SWIGLU_ALPHA = 1.702
SWIGLU_LIMIT = 7.0
LN_EPS = 1e-5
RMS_EPS = 1e-6
NEG_INF = -1e30

LANES = 128
SUBLANES = 8
BF16_SUBLANES = 16
MXU_COLS = 256
MOD_ROWS = 8
VMEM_LIMIT = 56 * 1024 * 1024

F32 = jnp.float32
BF16 = jnp.bfloat16


def _cparams(sem):
    return pltpu.CompilerParams(dimension_semantics=sem, vmem_limit_bytes=VMEM_LIMIT)


def _mod_kernel(c_ref, w_ref, b_ref, o_ref):
    c = c_ref[...]
    s = c * jax.nn.sigmoid(c)
    o_ref[0] = jnp.dot(s, w_ref[0], preferred_element_type=F32, precision=lax.Precision.HIGHEST) + b_ref[0]


def _modulation(c_rows, mod_w, mod_b):
    depth, d, n = mod_w.shape
    r = c_rows.shape[0]
    tn = min(n, 1536)
    return pl.pallas_call(
        _mod_kernel,
        grid=(depth, n // tn),
        in_specs=[pl.BlockSpec((r, d), lambda l, j: (0, 0)),
                  pl.BlockSpec((1, d, tn), lambda l, j: (l, 0, j)),
                  pl.BlockSpec((1, 1, tn), lambda l, j: (l, 0, j))],
        out_specs=pl.BlockSpec((1, r, tn), lambda l, j: (l, 0, j)),
        out_shape=jax.ShapeDtypeStruct((depth, r, n), F32),
        compiler_params=_cparams(("arbitrary", "arbitrary")),
        name="modulation",
    )(c_rows, mod_w, mod_b.reshape(depth, 1, n))


def _swap_pairs(t):
    lane = lax.broadcasted_iota(jnp.int32, t.shape, 1)
    return jnp.where(lane % 32 < 16, pltpu.roll(t, LANES - 16, 1), pltpu.roll(t, 16, 1))


def _head_mean_sq(t, seg_ref):
    t2 = t * t
    hi = t2.astype(BF16)
    lo = (t2 - hi.astype(F32)).astype(BF16)
    seg = seg_ref[...]
    s = jnp.dot(hi, seg, preferred_element_type=F32) + jnp.dot(lo, seg, preferred_element_type=F32)
    return s * (1.0 / HEAD_DIM)


def _qkv_kernel(*refs, nq, nk, rope, qk_norm):
    x_ref, mod_ref, w_ref, b_ref = refs[:4]
    pos = 4
    if rope:
        cos_ref, sin_ref = refs[pos:pos + 2]
        pos += 2
    if qk_norm:
        seg_ref, qg_ref, kg_ref = refs[pos:pos + 3]
        pos += 3
    q_ref, k_ref, v_ref = refs[pos:pos + 3]
    m = mod_ref[0]
    h = (x_ref[0] * (1.0 + m[1:2]) + m[0:1]).astype(BF16)
    if rope:
        cos, sin = cos_ref[...], sin_ref[...]
    starts = range(0, nq + 2 * nk, MXU_COLS)
    wides = [jnp.dot(h, w_ref[:, lo:lo + MXU_COLS], preferred_element_type=F32) + b_ref[:, lo:lo + MXU_COLS]
             for lo in starts]
    if qk_norm:
        mean_sqs = [_head_mean_sq(wide, seg_ref) if lo < nq + nk else None for lo, wide in zip(starts, wides)]
        wides = [wide if ms is None else wide * lax.rsqrt(ms + RMS_EPS) * (qg_ref[...] if lo < nq else kg_ref[...])
                 for lo, wide, ms in zip(starts, wides, mean_sqs)]
    for lo, wide in zip(starts, wides):
        if lo >= nq + nk:
            v_ref[0, :, lo - nq - nk:lo - nq - nk + MXU_COLS] = wide.astype(BF16)
            continue
        is_q = lo < nq
        for half in range(MXU_COLS // LANES):
            t = wide[:, half * LANES:(half + 1) * LANES]
            at = lo + half * LANES
            if rope:
                t = t * cos + _swap_pairs(t) * sin
            if is_q:
                q_ref[0, :, at:at + LANES] = (t * HEAD_DIM ** -0.5).astype(BF16)
            else:
                k_ref[0, :, at - nq:at - nq + LANES] = t.astype(BF16)


def _dup_heads(w, n_heads):
    lead = w.shape[:-1]
    w = w.reshape(lead + (n_heads, 1, HEAD_DIM))
    return jnp.broadcast_to(w, lead + (n_heads, 2, HEAD_DIM)).reshape(lead + (n_heads * LANES,))


def _pad_heads(w, n_heads, fill):
    lead = w.shape[:-1]
    w = w.reshape(lead + (n_heads, HEAD_DIM))
    pad = jnp.full(lead + (n_heads, HEAD_DIM), fill, w.dtype)
    return jnp.concatenate([w, pad], axis=-1).reshape(lead + (n_heads * LANES,))


def _qkv_project(x, mod, w, b, *, tables=None, norms=None, tm):
    bsz, seq, d = x.shape
    n_kv = (w.shape[1] - d) // (2 * HEAD_DIM)
    nq, nk = d, n_kv * LANES
    wq, wk, wv = w[:, :d], w[:, d:d + n_kv * HEAD_DIM], w[:, d + n_kv * HEAD_DIM:]
    w_ext = jnp.concatenate([wq, _dup_heads(wk, n_kv), _pad_heads(wv, n_kv, 0.0)], axis=1).astype(BF16)
    bq, bk, bv = b[:d], b[d:d + n_kv * HEAD_DIM], b[d + n_kv * HEAD_DIM:]
    b_ext = jnp.concatenate([bq, _dup_heads(bk, n_kv), _pad_heads(bv, n_kv, 1.0)])[None, :].astype(F32)
    n = nq + 2 * nk
    args = [x, mod, w_ext, b_ext]
    specs = [pl.BlockSpec((1, tm, d), lambda bi, i: (bi, i, 0)),
             pl.BlockSpec((1, MOD_ROWS, d), lambda bi, i: (bi, 0, 0)),
             pl.BlockSpec((d, n), lambda bi, i: (0, 0)),
             pl.BlockSpec((1, n), lambda bi, i: (0, 0))]
    if tables is not None:
        args += list(tables)
        specs += [pl.BlockSpec((tm, LANES), lambda bi, i: (i, 0))] * 2
    if norms is not None:
        qn, kn = norms
        col = jnp.arange(MXU_COLS) // HEAD_DIM
        seg = (col[:, None] == col[None, :]).astype(BF16)
        reps = MXU_COLS // HEAD_DIM
        args += [seg, jnp.tile(qn, reps)[None, :].astype(F32), jnp.tile(kn, reps)[None, :].astype(F32)]
        specs += [pl.BlockSpec((MXU_COLS, MXU_COLS), lambda bi, i: (0, 0)),
                  pl.BlockSpec((1, MXU_COLS), lambda bi, i: (0, 0)),
                  pl.BlockSpec((1, MXU_COLS), lambda bi, i: (0, 0))]
    return pl.pallas_call(
        functools.partial(_qkv_kernel, nq=nq, nk=nk, rope=tables is not None, qk_norm=norms is not None),
        grid=(bsz, seq // tm),
        in_specs=specs,
        out_specs=[pl.BlockSpec((1, tm, nq), lambda bi, i: (bi, i, 0)),
                   pl.BlockSpec((1, tm, nk), lambda bi, i: (bi, i, 0)),
                   pl.BlockSpec((1, tm, nk), lambda bi, i: (bi, i, 0))],
        out_shape=[jax.ShapeDtypeStruct((bsz, seq, nq), BF16),
                   jax.ShapeDtypeStruct((bsz, seq, nk), BF16),
                   jax.ShapeDtypeStruct((bsz, seq, nk), BF16)],
        compiler_params=_cparams(("parallel", "parallel")),
        name="qkv_project",
    )(*args)


def _conv_in_kernel(x_ref, mod_ref, w_ref, bg_ref, u_ref, *, d):
    m = mod_ref[0]
    h = (x_ref[0] * (1.0 + m[1:2]) + m[0:1]).astype(BF16)
    for c in range(d // MXU_COLS):
        lo = c * MXU_COLS
        bg = jnp.dot(h, w_ref[:, lo:lo + MXU_COLS], preferred_element_type=F32)
        cg = jnp.dot(h, w_ref[:, d + lo:d + lo + MXU_COLS], preferred_element_type=F32)
        xv = jnp.dot(h, w_ref[:, 2 * d + lo:2 * d + lo + MXU_COLS], preferred_element_type=F32)
        bg_ref[0, :, lo:lo + MXU_COLS] = bg.astype(BF16)
        u_ref[0, :, lo:lo + MXU_COLS] = (cg * xv).astype(BF16)


def _conv_in_project(x, mod, w_in, *, tm):
    bsz, seq, d = x.shape
    return pl.pallas_call(
        functools.partial(_conv_in_kernel, d=d),
        grid=(bsz, seq // tm),
        in_specs=[pl.BlockSpec((1, tm, d), lambda bi, i: (bi, i, 0)),
                  pl.BlockSpec((1, MOD_ROWS, d), lambda bi, i: (bi, 0, 0)),
                  pl.BlockSpec((d, 3 * d), lambda bi, i: (0, 0))],
        out_specs=[pl.BlockSpec((1, tm, d), lambda bi, i: (bi, i, 0))] * 2,
        out_shape=[jax.ShapeDtypeStruct((bsz, seq, d), BF16)] * 2,
        compiler_params=_cparams(("parallel", "parallel")),
        name="conv_in_project",
    )(x, mod, w_in.astype(BF16))


def _qk(q, k):
    return lax.dot_general(q, k, (((1,), (1,)), ((), ())), preferred_element_type=F32)


def _scores(qg, kc, bias):
    s = _qk(qg, kc)
    return s if bias is None else s + bias


def _row_max(qg, chunks, sink):
    mx = None
    for kc, _, bias in chunks:
        s = _scores(qg, kc, bias)
        for j in range(s.shape[1] // LANES):
            part = s[:, j * LANES:(j + 1) * LANES]
            mx = part if mx is None else jnp.maximum(mx, part)
    m = jnp.max(mx, axis=1, keepdims=True)
    return m if sink is None else jnp.maximum(m, sink)


def _exp_pv(qg, chunks, m):
    acc = None
    for kc, vc, bias in chunks:
        p = jnp.exp(_scores(qg, kc, bias) - m).astype(BF16)
        d = jnp.dot(p, vc, preferred_element_type=F32)
        acc = d if acc is None else acc + d
    return acc


def _attend_groups(q_ref, o_ref, groups, lane):
    heads = []
    for col0, chunks, sinks in groups:
        for pair in range(GROUP // 2):
            qp = q_ref[0, :, col0 + pair * LANES:col0 + (pair + 1) * LANES]
            heads.append((jnp.where(lane < HEAD_DIM, qp, jnp.zeros_like(qp)), chunks, sinks[2 * pair]))
            heads.append((jnp.where(lane >= HEAD_DIM, qp, jnp.zeros_like(qp)), chunks, sinks[2 * pair + 1]))
    ms = [_row_max(qg, chunks, sink) for qg, chunks, sink in heads]
    accs = [_exp_pv(qg, chunks, m) for (qg, chunks, _), m in zip(heads, ms)]
    for gi, (col0, _, sinks) in enumerate(groups):
        for pair in range(GROUP // 2):
            e, o = gi * GROUP + 2 * pair, gi * GROUP + 2 * pair + 1
            a0, a1 = accs[e], accs[o]
            r0, r1 = pltpu.roll(a0, HEAD_DIM, 1), pltpu.roll(a1, HEAD_DIM, 1)
            d0, d1 = r0, a1
            if sinks[0] is not None:
                d0 = d0 + jnp.exp(sinks[2 * pair] - ms[e])
                d1 = d1 + jnp.exp(sinks[2 * pair + 1] - ms[o])
            out = jnp.where(lane < HEAD_DIM, a0 / d0, r1 / d1)
            o_ref[0, :, col0 + pair * LANES:col0 + (pair + 1) * LANES] = out.astype(o_ref.dtype)


def _win_attn_kernel(q_ref, kp_ref, kc_ref, kn_ref, kz_ref, vp_ref, vc_ref, vn_ref, vz_ref, sink_ref, o_ref,
                     *, tq, seq):
    i = pl.program_id(1)
    r = lax.broadcasted_iota(jnp.int32, (tq, WINDOW), 0)
    c = lax.broadcasted_iota(jnp.int32, (tq, WINDOW), 1)
    bias_prev = jnp.where((c >= r) & (i > 0), 0.0, NEG_INF)
    bias_next = jnp.where((r - c >= tq - WINDOW) & (i < seq // tq - 1), 0.0, NEG_INF)
    bias_cur = None
    if tq - 1 > WINDOW:
        rr = lax.broadcasted_iota(jnp.int32, (tq, tq), 0)
        cc = lax.broadcasted_iota(jnp.int32, (tq, tq), 1)
        bias_cur = jnp.where(jnp.abs(cc - rr) <= WINDOW, 0.0, NEG_INF)
    lane = lax.broadcasted_iota(jnp.int32, (tq, LANES), 1)
    groups = []
    for hk in range(kc_ref.shape[2] // LANES):
        lo = hk * LANES
        chunks = [(kp_ref[0, :, lo:lo + LANES], vp_ref[0, :, lo:lo + LANES], bias_prev),
                  (kc_ref[0, :, lo:lo + LANES], vc_ref[0, :, lo:lo + LANES], bias_cur),
                  (kn_ref[0, :, lo:lo + LANES], vn_ref[0, :, lo:lo + LANES], bias_next),
                  (kz_ref[0, :, lo:lo + LANES], vz_ref[0, :, lo:lo + LANES], None)]
        groups.append((hk * GROUP * HEAD_DIM, chunks, [sink_ref[hk, g] for g in range(GROUP)]))
    _attend_groups(q_ref, o_ref, groups, lane)


def _window_attention(q, k, v, kz, vz, sink, *, tq):
    bsz, seq, d = q.shape
    nk = k.shape[2]
    ctx = kz.shape[1]
    r = tq // WINDOW
    last = seq // WINDOW - 1
    cur = lambda bi, i: (bi, i, 0)
    prev = lambda bi, i: (bi, jnp.maximum(i * r - 1, 0), 0)
    nxt = lambda bi, i: (bi, jnp.minimum((i + 1) * r, last), 0)
    zmap = lambda bi, i: (bi, 0, 0)
    kv_specs = [pl.BlockSpec((1, WINDOW, nk), prev), pl.BlockSpec((1, tq, nk), cur),
                pl.BlockSpec((1, WINDOW, nk), nxt), pl.BlockSpec((1, ctx, nk), zmap)]
    return pl.pallas_call(
        functools.partial(_win_attn_kernel, tq=tq, seq=seq),
        grid=(bsz, seq // tq),
        in_specs=[pl.BlockSpec((1, tq, d), cur)] + kv_specs + kv_specs
                 + [pl.BlockSpec(memory_space=pltpu.SMEM)],
        out_specs=pl.BlockSpec((1, tq, d), cur),
        out_shape=jax.ShapeDtypeStruct((bsz, seq, d), BF16),
        compiler_params=_cparams(("parallel", "parallel")),
        name="window_attention",
    )(q, k, k, k, kz, v, v, v, vz, sink.reshape(nk // LANES, GROUP).astype(F32))


def _dense_attn_kernel(*refs, ck, has_sink):
    if has_sink:
        q_ref, k_ref, v_ref, sink_ref, o_ref = refs
    else:
        q_ref, k_ref, v_ref, o_ref = refs
    hk = pl.program_id(1)
    tq = q_ref.shape[1]
    chunks = [(k_ref[0, lo:lo + ck, :], v_ref[0, lo:lo + ck, :], None) for lo in range(0, k_ref.shape[1], ck)]
    lane = lax.broadcasted_iota(jnp.int32, (tq, LANES), 1)
    sinks = [sink_ref[hk, g] if has_sink else None for g in range(GROUP)]
    _attend_groups(q_ref, o_ref, [(0, chunks, sinks)], lane)


def _dense_attention(q, k, v, sink=None, *, tq):
    bsz, seq, d = q.shape
    n_keys = k.shape[1]
    n_kv = k.shape[2] // LANES
    ck = 256 if n_keys % 256 == 0 else LANES
    qmap = lambda bi, h, i: (bi, i, h)
    kmap = lambda bi, h, i: (bi, 0, h)
    args = [q, k, v]
    specs = [pl.BlockSpec((1, tq, GROUP * HEAD_DIM), qmap),
             pl.BlockSpec((1, n_keys, LANES), kmap), pl.BlockSpec((1, n_keys, LANES), kmap)]
    if sink is not None:
        args.append(sink.reshape(n_kv, GROUP).astype(F32))
        specs.append(pl.BlockSpec(memory_space=pltpu.SMEM))
    return pl.pallas_call(
        functools.partial(_dense_attn_kernel, ck=ck, has_sink=sink is not None),
        grid=(bsz, n_kv, seq // tq),
        in_specs=specs,
        out_specs=pl.BlockSpec((1, tq, GROUP * HEAD_DIM), qmap),
        out_shape=jax.ShapeDtypeStruct((bsz, seq, d), BF16),
        compiler_params=_cparams(("parallel", "parallel", "parallel")),
        name="dense_attention",
    )(*args)


def _store_token_tiles(ref, val, first=0):
    rows, d = val.shape
    rt = d // LANES
    for s in range(rt):
        ref[pl.ds(first * rt + s, rows, stride=rt), :] = val[:, s * LANES:(s + 1) * LANES]


def _load_token_tiles(ref, rows, rt):
    return jnp.concatenate([ref[pl.ds(s, rows, stride=rt), :] for s in range(rt)], axis=1)


def _layer_norm(r, g, b):
    mu = jnp.mean(r, axis=-1, keepdims=True)
    rc = r - mu
    var = jnp.mean(rc * rc, axis=-1, keepdims=True)
    return rc * lax.rsqrt(var + LN_EPS) * g + b


def _select_experts(logits):
    rows, n_exp = logits.shape
    lane_e = lax.broadcasted_iota(jnp.int32, (rows, n_exp), 1).astype(F32)
    work = logits
    sels, vals, idxs = [], [], []
    for _ in range(TOP_K):
        mk = jnp.max(work, axis=1, keepdims=True)
        ik = jnp.min(jnp.where(work == mk, lane_e, float(n_exp)), axis=1, keepdims=True)
        sel = lane_e == ik
        work = jnp.where(sel, -jnp.inf, work)
        sels.append(sel)
        vals.append(mk)
        idxs.append(ik)
    exps = [jnp.exp(v - vals[0]) for v in vals]
    denom = exps[0]
    for e in exps[1:]:
        denom = denom + e
    routed = sels[0].astype(F32)
    for sel in sels[1:]:
        routed = routed + sel.astype(F32)
    return sels, [e / denom for e in exps], idxs, routed


def _store_route(choice, before, first, gate_ref, idx_ref, rank_ref):
    sels, gate_cols, idx_cols, _ = choice
    rows = before.shape[0]
    lane_k = lax.broadcasted_iota(jnp.int32, (rows, TOP_K), 1)
    gates = jnp.zeros((rows, TOP_K), F32)
    idx = jnp.zeros((rows, TOP_K), F32)
    rank = jnp.zeros((rows, TOP_K), F32)
    for kk in range(TOP_K):
        here = lane_k == kk
        gates = jnp.where(here, gate_cols[kk], gates)
        idx = jnp.where(here, idx_cols[kk], idx)
        rank = jnp.where(here, jnp.sum(jnp.where(sels[kk], before, 0.0), axis=1, keepdims=True), rank)
    gate_ref[0, first:first + rows, :] = gates
    idx_ref[0, first:first + rows, :] = idx.astype(jnp.int32)
    rank_ref[0, first:first + rows, :] = rank.astype(jnp.int32)


def _mixer_out_kernel(*refs, conv, alpha, seq):
    if conv:
        bg_ref, u_ref, up_ref, un_ref, cw_ref = refs[:5]
        refs = refs[5:]
    else:
        o_ref = refs[0]
        refs = refs[1:]
    (w_ref, x_ref, mod_ref, lng_ref, lnb_ref, rwh_ref, rwl_ref, rb_ref,
     xo_ref, h_ref, gate_ref, idx_ref, rank_ref, cnt_ref) = refs
    if conv:
        i = pl.program_id(1)
        u = u_ref[0].astype(F32)
        tm = u.shape[0]
        row = lax.broadcasted_iota(jnp.int32, u.shape, 0)
        halo = up_ref.shape[1]
        before = jnp.where(i == 0, 0.0, up_ref[0, halo - 1:halo, :].astype(F32))
        after = jnp.where(i == seq // tm - 1, 0.0, un_ref[0, 0:1, :].astype(F32))
        u_prev = jnp.where(row == 0, before, pltpu.roll(u, 1, 0))
        u_next = jnp.where(row == tm - 1, after, pltpu.roll(u, tm - 1, 0))
        cw = cw_ref[...]
        y = cw[0:1] * u_prev + cw[1:2] * u + cw[2:3] * u_next
        mixed = (bg_ref[0].astype(F32) * y).astype(BF16)
    else:
        mixed = o_ref[0]
    m = mod_ref[0]
    tm = mixed.shape[0]
    n_parts = 2 if tm % (2 * MXU_COLS) == 0 else 1
    rows = tm // n_parts
    rwh = rwh_ref[...]
    starts = [part * rows for part in range(n_parts)]
    oxs = [jnp.dot(mixed[r0:r0 + rows], w_ref[...], preferred_element_type=F32) for r0 in starts]
    xns = [_layer_norm(alpha * x_ref[0, r0:r0 + rows, :] + m[2:3] * ox, lng_ref[...], lnb_ref[...])
           for r0, ox in zip(starts, oxs)]
    h2s = [xn * (1.0 + m[4:5]) + m[3:4] for xn in xns]
    all_logits = []
    for r0, xn, h2 in zip(starts, xns, h2s):
        xo_ref[0, r0:r0 + rows, :] = xn
        hh = h2.astype(BF16)
        hl = (h2 - hh.astype(F32)).astype(BF16)
        _store_token_tiles(h_ref, h2, r0)
        all_logits.append(jnp.dot(hh, rwh, preferred_element_type=F32) + jnp.dot(hl, rwh, preferred_element_type=F32)
                          + jnp.dot(hh, rwl_ref[...], preferred_element_type=F32) + rb_ref[...])
    choices = [_select_experts(logits) for logits in all_logits]
    routed = jnp.concatenate([c[3] for c in choices], axis=0)
    row = lax.broadcasted_iota(jnp.int32, (tm, tm), 0)
    col = lax.broadcasted_iota(jnp.int32, (tm, tm), 1)
    earlier = jnp.where(col < row, 1.0, 0.0).astype(BF16)
    before = jnp.dot(earlier, routed.astype(BF16), preferred_element_type=F32)
    for part, choice in enumerate(choices):
        r0 = part * rows
        _store_route(choice, before[r0:r0 + rows], r0, gate_ref, idx_ref, rank_ref)
    cnt_ref[0] = jnp.sum(routed, axis=0, keepdims=True).astype(jnp.int32)


def _mixer_out(mixed, w_out, x, mod, ln_g, ln_b, router_w, router_b, *, alpha, tm, conv_w=None):
    bsz, seq, d = x.shape
    n_exp = router_w.shape[1]
    nt = seq // tm
    tile = pl.BlockSpec((1, tm, d), lambda bi, i: (bi, i, 0))
    ktile = pl.BlockSpec((1, tm, TOP_K), lambda bi, i: (bi, i, 0))
    row = pl.BlockSpec((1, d), lambda bi, i: (0, 0))
    conv = conv_w is not None
    if conv:
        bg, u = mixed
        halo = BF16_SUBLANES
        r = tm // halo
        last = seq // halo - 1
        args = [bg, u, u, u, jnp.pad(conv_w.astype(F32), ((0, MOD_ROWS - conv_w.shape[0]), (0, 0)))]
        specs = [tile, tile,
                 pl.BlockSpec((1, halo, d), lambda bi, i: (bi, jnp.maximum(i * r - 1, 0), 0)),
                 pl.BlockSpec((1, halo, d), lambda bi, i: (bi, jnp.minimum((i + 1) * r, last), 0)),
                 pl.BlockSpec((MOD_ROWS, d), lambda bi, i: (0, 0))]
    else:
        args, specs = [mixed], [tile]
    rw_hi = router_w.astype(BF16)
    rw_lo = (router_w - rw_hi.astype(F32)).astype(BF16)
    args += [w_out.astype(BF16), x, mod, ln_g[None, :], ln_b[None, :], rw_hi, rw_lo, router_b[None, :]]
    specs += [pl.BlockSpec((d, d), lambda bi, i: (0, 0)), tile,
              pl.BlockSpec((1, MOD_ROWS, d), lambda bi, i: (bi, 0, 0)), row, row,
              pl.BlockSpec((d, n_exp), lambda bi, i: (0, 0)), pl.BlockSpec((d, n_exp), lambda bi, i: (0, 0)),
              pl.BlockSpec((1, n_exp), lambda bi, i: (0, 0))]
    return pl.pallas_call(
        functools.partial(_mixer_out_kernel, conv=conv, alpha=alpha, seq=seq),
        grid=(bsz, nt),
        in_specs=specs,
        out_specs=[tile, pl.BlockSpec((tm * (d // LANES), LANES), lambda bi, i: (bi * nt + i, 0)),
                   ktile, ktile, ktile,
                   pl.BlockSpec((1, 1, n_exp), lambda bi, i: (bi * nt + i, 0, 0))],
        out_shape=[jax.ShapeDtypeStruct((bsz, seq, d), F32),
                   jax.ShapeDtypeStruct((bsz * seq * (d // LANES), LANES), F32),
                   jax.ShapeDtypeStruct((bsz, seq, TOP_K), F32), jax.ShapeDtypeStruct((bsz, seq, TOP_K), jnp.int32),
                   jax.ShapeDtypeStruct((bsz, seq, TOP_K), jnp.int32),
                   jax.ShapeDtypeStruct((bsz * nt, 1, n_exp), jnp.int32)],
        compiler_params=_cparams(("parallel", "parallel")),
        name="mixer_out",
    )(*args)


def _expert_kernel(be_ref, first_ref, used_ref, x_ref, wgu_ref, bgu_ref, wdn_ref, bdn_ref, o_ref,
                   wgu_bf, wdn_bf, *, ff, fc):
    i = pl.program_id(0)

    @pl.when(first_ref[i] == 1)
    def _():
        wgu_bf[...] = wgu_ref[0, 0].astype(BF16)
        wdn_bf[...] = wdn_ref[0, 0].astype(BF16)

    @pl.when(i < used_ref[0])
    def _():
        rt = wdn_bf.shape[1] // LANES
        bm = x_ref.shape[0] // rt
        xb = _load_token_tiles(x_ref, bm, rt).astype(BF16)
        acc = jnp.zeros((bm, rt * LANES), F32)
        for c in range(ff // fc):
            lo = c * fc
            gate = jnp.dot(xb, wgu_bf[:, lo:lo + fc], preferred_element_type=F32) + bgu_ref[0, 0, :, lo:lo + fc]
            up = jnp.dot(xb, wgu_bf[:, ff + lo:ff + lo + fc], preferred_element_type=F32) \
                + bgu_ref[0, 0, :, ff + lo:ff + lo + fc]
            gate = jnp.minimum(gate, SWIGLU_LIMIT)
            up = jnp.clip(up, -SWIGLU_LIMIT, SWIGLU_LIMIT)
            act = (up + 1.0) * gate * jax.nn.sigmoid(SWIGLU_ALPHA * gate)
            acc = acc + jnp.dot(act.astype(BF16), wdn_bf[lo:lo + fc, :], preferred_element_type=F32)
        _store_token_tiles(o_ref, acc + bdn_ref[0, 0])

    @pl.when(i >= used_ref[0])
    def _():
        o_ref[...] = jnp.zeros_like(o_ref)


def _expert_ffn(xs, blk_expert, blk_first, n_used, layer, w_gu, b_gu, w_down, b_down, *, bm):
    depth, n_exp, d, ff2 = w_gu.shape
    ff = ff2 // 2
    rt = d // LANES
    wmap = lambda i, be, fi, nu: (layer, be[i], 0, 0)
    grid_spec = pltpu.PrefetchScalarGridSpec(
        num_scalar_prefetch=3,
        grid=(xs.shape[0] // (bm * rt),),
        in_specs=[pl.BlockSpec((bm * rt, LANES), lambda i, be, fi, nu: (jnp.minimum(i, nu[0] - 1), 0)),
                  pl.BlockSpec((1, 1, d, ff2), wmap), pl.BlockSpec((1, 1, 1, ff2), wmap),
                  pl.BlockSpec((1, 1, ff, d), wmap), pl.BlockSpec((1, 1, 1, d), wmap)],
        out_specs=pl.BlockSpec((bm * rt, LANES), lambda i, be, fi, nu: (i, 0)),
        scratch_shapes=[pltpu.VMEM((d, ff2), BF16), pltpu.VMEM((ff, d), BF16)],
    )
    return pl.pallas_call(
        functools.partial(_expert_kernel, ff=ff, fc=min(ff, 512)),
        grid_spec=grid_spec,
        out_shape=jax.ShapeDtypeStruct(xs.shape, F32),
        compiler_params=_cparams(("arbitrary",)),
        name="expert_ffn",
    )(blk_expert, blk_first, n_used, xs, w_gu, b_gu.reshape(depth, n_exp, 1, ff2), w_down,
      b_down.reshape(depth, n_exp, 1, d))


def _route(idx, rank, cnt_tiles, tile_rows, bm):
    t = idx.shape[0]
    a = t * TOP_K
    n_exp = cnt_tiles.shape[1]
    counts = jnp.sum(cnt_tiles, axis=0)
    padded = (counts + bm - 1) // bm * bm
    pends = jnp.cumsum(padded)
    pstarts = pends - padded
    tile_off = pstarts[None, :] + jnp.cumsum(cnt_tiles, axis=0) - cnt_tiles
    off_tok, lo = [], 0
    for rows, n_tiles in tile_rows:
        part = tile_off[lo:lo + n_tiles]
        off_tok.append(jnp.broadcast_to(part[:, None, :], (n_tiles, rows, n_exp)).reshape(n_tiles * rows, n_exp))
        lo += n_tiles
    off_tok = jnp.concatenate(off_tok, axis=0)
    chosen = idx[:, :, None] == jnp.arange(n_exp, dtype=jnp.int32)[None, None, :]
    dest = jnp.sum(jnp.where(chosen, off_tok[:, None, :], 0), axis=-1) + rank
    nblk = -(-a // bm) + n_exp
    blk_start = jnp.arange(nblk, dtype=jnp.int32) * bm
    blk_expert = jnp.minimum(jnp.sum((blk_start[:, None] >= pends[None, :]).astype(jnp.int32), axis=1),
                             n_exp - 1)
    blk_first = jnp.concatenate([jnp.ones((1,), jnp.int32),
                                 (blk_expert[1:] != blk_expert[:-1]).astype(jnp.int32)])
    n_used = (pends[-1] // bm).astype(jnp.int32).reshape(1)
    last_blk = jnp.maximum(pends // bm - 1, 0).astype(jnp.int32)
    return dest, nblk, blk_expert, blk_first, n_used, last_blk


def _dispatch_kernel(*refs, rt, n_exp, nblk, tiles):
    n_s = len(tiles)
    lb_ref, nu_ref = refs[:2]
    dest_refs = refs[2:2 + n_s]
    h_refs = refs[2 + n_s:2 + 2 * n_s]
    xs_ref, zero_buf, sem, zero_sem = refs[2 + 2 * n_s:]
    i = pl.program_id(0)
    blk_rows = zero_buf.shape[0]

    def zero_copy(blk):
        return pltpu.make_async_copy(
            zero_buf, xs_ref.at[pl.ds(pl.multiple_of(blk * blk_rows, blk_rows), blk_rows), :], zero_sem)

    @pl.when(i == 0)
    def _():
        zero_buf[...] = jnp.zeros_like(zero_buf)
        todo = [(lb_ref[0], None)]
        todo += [(lb_ref[e], lb_ref[e] != lb_ref[e - 1]) for e in range(1, n_exp)]
        todo += [(nu_ref[0] + e, nu_ref[0] + e < nblk) for e in range(n_exp)]
        for blk, cond in todo:
            if cond is None:
                zero_copy(blk).start()
            else:
                pl.when(cond)(lambda blk=blk: zero_copy(blk).start())
        for _, cond in todo:
            if cond is None:
                zero_copy(0).wait()
            else:
                pl.when(cond)(lambda: zero_copy(0).wait())

    first = 0
    for dest_ref, h_ref, (tm, nt) in zip(dest_refs, h_refs, tiles):
        @pl.when((i >= first) & (i < first + nt))
        def _(dest_ref=dest_ref, h_ref=h_ref, tm=tm):
            def issue(j, carry):
                base = j * SUBLANES
                for s in range(SUBLANES):
                    src = h_ref.at[pl.ds(pl.multiple_of((base + s) * rt, rt), rt), :]
                    for kk in range(TOP_K):
                        dst = pl.multiple_of(dest_ref[0, 0, (base + s) * TOP_K + kk], rt)
                        pltpu.make_async_copy(src, xs_ref.at[pl.ds(dst, rt), :], sem).start(priority=kk % 2)
                return carry

            lax.fori_loop(0, tm // SUBLANES, issue, 0)
            for kk in range(TOP_K):
                pltpu.make_async_copy(h_ref, xs_ref.at[pl.ds(0, tm * rt), :], sem).wait()

        first += nt


def _dispatch(streams, last_blk, n_used, nblk, *, bm, rt):
    n_exp = last_blk.shape[0]
    tiles = [(tm, h.shape[0] // (tm * rt)) for h, _, tm in streams]
    dest_specs, h_specs, dests, hs = [], [], [], []
    first = 0
    for (h, dest_rows, tm), (_, nt) in zip(streams, tiles):
        tile_of = lambda i, lb, nu, first=first, nt=nt: jnp.clip(i - first, 0, nt - 1)
        dest_specs.append(pl.BlockSpec((1, 1, tm * TOP_K), lambda i, lb, nu, f=tile_of: (f(i, lb, nu), 0, 0),
                                       memory_space=pltpu.SMEM))
        h_specs.append(pl.BlockSpec((tm * rt, LANES), lambda i, lb, nu, f=tile_of: (f(i, lb, nu), 0)))
        dests.append(dest_rows.reshape(nt, 1, tm * TOP_K))
        hs.append(h)
        first += nt
    return pl.pallas_call(
        functools.partial(_dispatch_kernel, rt=rt, n_exp=n_exp, nblk=nblk, tiles=tiles),
        grid_spec=pltpu.PrefetchScalarGridSpec(
            num_scalar_prefetch=2, grid=(first,), in_specs=dest_specs + h_specs,
            out_specs=pl.BlockSpec(memory_space=pl.ANY),
            scratch_shapes=[pltpu.VMEM((bm * rt, LANES), F32), pltpu.SemaphoreType.DMA(()),
                            pltpu.SemaphoreType.DMA(())]),
        out_shape=jax.ShapeDtypeStruct((nblk * bm * rt, LANES), F32),
        compiler_params=_cparams(("arbitrary",)),
        name="dispatch_rows",
    )(last_blk, n_used, *dests, *hs)


def _combine_kernel(dest_ref, next_dest_ref, out_ref, gate_ref, x_ref, mod_ref, lng_ref, lnb_ref, o_ref,
                    y_buf, sem, *, alpha, n_tiles):
    n = pl.program_id(0)
    tm, d = x_ref.shape[1:]
    rt = d // LANES

    def request(table_ref, slot):
        def issue(j, carry):
            base = j * SUBLANES
            for s in range(SUBLANES):
                row = pl.multiple_of((base + s) * rt, rt)
                for kk in range(TOP_K):
                    src = pl.multiple_of(table_ref[0, 0, (base + s) * TOP_K + kk], rt)
                    pltpu.make_async_copy(out_ref.at[pl.ds(src, rt), :], y_buf.at[slot, kk, pl.ds(row, rt), :],
                                          sem.at[slot]).start(priority=kk % 2)
            return carry

        lax.fori_loop(0, tm // SUBLANES, issue, 0)

    slot = n % 2

    @pl.when(n == 0)
    def _():
        request(dest_ref, 0)

    for kk in range(TOP_K):
        pltpu.make_async_copy(out_ref.at[pl.ds(0, tm * rt), :], y_buf.at[slot, kk], sem.at[slot]).wait()
    for r in range(tm):
        for kk in range(TOP_K):
            src = pl.multiple_of(next_dest_ref[0, 0, r * TOP_K + kk], rt)
            pltpu.make_async_copy(out_ref.at[pl.ds(src, rt), :], y_buf.at[1 - slot, kk, pl.ds(r * rt, rt), :],
                                  sem.at[1 - slot]).start(priority=kk % 2)
    g = gate_ref[0]
    fx = g[:, 0:1] * _load_token_tiles(y_buf.at[slot, 0], tm, rt)
    for kk in range(1, TOP_K):
        fx = fx + g[:, kk:kk + 1] * _load_token_tiles(y_buf.at[slot, kk], tm, rt)
    m = mod_ref[0]
    o_ref[0] = _layer_norm(alpha * x_ref[0] + m[5:6] * fx, lng_ref[...], lnb_ref[...])

    @pl.when(n == n_tiles - 1)
    def _():
        for kk in range(TOP_K):
            pltpu.make_async_copy(out_ref.at[pl.ds(0, tm * rt), :], y_buf.at[1 - slot, kk], sem.at[1 - slot]).wait()


def _combine(out, dest_rows, gates, x, mod, ln_g, ln_b, *, alpha, tm):
    bsz, seq, d = x.shape
    nt = seq // tm
    n_tiles = bsz * nt
    dest = dest_rows.reshape(n_tiles, 1, tm * TOP_K)
    tile = pl.BlockSpec((1, tm, d), lambda n: (n // nt, n % nt, 0))
    row = pl.BlockSpec((1, d), lambda n: (0, 0))
    return pl.pallas_call(
        functools.partial(_combine_kernel, alpha=alpha, n_tiles=n_tiles),
        grid=(n_tiles,),
        in_specs=[pl.BlockSpec((1, 1, tm * TOP_K), lambda n: (n, 0, 0), memory_space=pltpu.SMEM),
                  pl.BlockSpec((1, 1, tm * TOP_K), lambda n: (jnp.minimum(n + 1, n_tiles - 1), 0, 0),
                               memory_space=pltpu.SMEM),
                  pl.BlockSpec(memory_space=pl.ANY),
                  pl.BlockSpec((1, tm, TOP_K), lambda n: (n // nt, n % nt, 0)),
                  tile, pl.BlockSpec((1, MOD_ROWS, d), lambda n: (n // nt, 0, 0)), row, row],
        out_specs=tile,
        out_shape=jax.ShapeDtypeStruct((bsz, seq, d), F32),
        scratch_shapes=[pltpu.VMEM((2, TOP_K, tm * (d // LANES), LANES), F32), pltpu.SemaphoreType.DMA((2,))],
        compiler_params=_cparams(("arbitrary",)),
        name="combine_norm",
    )(dest, dest, out, gates, x, mod, ln_g[None, :], ln_b[None, :])


def _rope_tables(seq):
    t = jnp.arange(seq, dtype=jnp.int32)
    row = (t // GRID_W).astype(F32)
    col = (t % GRID_W).astype(F32)
    n_freq = HEAD_DIM // 4
    inv_freq = ROPE_THETA ** (-jnp.arange(n_freq, dtype=F32) / n_freq)
    ar, ac = row[:, None] * inv_freq, col[:, None] * inv_freq
    cos = jnp.concatenate([jnp.cos(ar), jnp.cos(ar), jnp.cos(ac), jnp.cos(ac)], axis=1)
    sin = jnp.concatenate([-jnp.sin(ar), jnp.sin(ar), -jnp.sin(ac), jnp.sin(ac)], axis=1)
    return jnp.tile(cos, (1, 2)), jnp.tile(sin, (1, 2))


def _tile(n, want):
    t = min(n, want)
    while n % t:
        t //= 2
    return t


def kernel(x, c, ctx, c_ctx, mod_w, mod_b, ln1_g, ln1_b, ln2_g, ln2_b, win_wqkv, win_bqkv, win_sink, win_wo,
           conv_win, conv_w, conv_wout, full_wqkv, full_qnorm, full_knorm, full_wo,
           router_w, router_b, expert_wgu, expert_bgu, expert_wdown, expert_bdown):
    bsz, seq, d = x.shape
    ctx_len = ctx.shape[1]
    depth = mod_w.shape[0]
    alpha = (2.0 * depth) ** 0.25
    tables = _rope_tables(seq)
    tm_x, tm_z = _tile(seq, 512), _tile(ctx_len, 512)
    tq_win = _tile(seq, 256)
    tq_full = _tile(seq, 512)
    bm = 768

    c_rows = jnp.zeros((2 * MOD_ROWS, d), F32).at[:bsz].set(c).at[bsz].set(c_ctx)
    mods = _modulation(c_rows, mod_w, mod_b)
    z = ctx
    for i in range(depth):
        kind, j = i % N_MIXERS, i // N_MIXERS
        need_ctx = i < depth - 1
        mod_x = jnp.pad(mods[i, :bsz].reshape(bsz, 6, d), ((0, 0), (0, MOD_ROWS - 6), (0, 0)))
        mod_z = jnp.broadcast_to(jnp.pad(mods[i, bsz].reshape(1, 6, d), ((0, 0), (0, MOD_ROWS - 6), (0, 0))),
                                 (bsz, MOD_ROWS, d))
        route_args = (router_w[i], router_b[i])
        ln1 = (ln1_g[i], ln1_b[i])
        oz = None
        if kind == 0:
            qx, kx, vx = _qkv_project(x, mod_x, win_wqkv[j], win_bqkv[j], tables=tables, tm=tm_x)
            qz, kz, vz = _qkv_project(z, mod_z, win_wqkv[j], win_bqkv[j], tm=tm_z)
            ox = _window_attention(qx, kx, vx, kz, vz, win_sink[j], tq=tq_win)
            if need_ctx:
                oz = _dense_attention(qz, kz, vz, win_sink[j], tq=ctx_len)
            w_out, conv_taps = win_wo[j], None
        elif kind == 1:
            ox = _conv_in_project(x, mod_x, conv_win[j], tm=tm_x)
            if need_ctx:
                oz = _conv_in_project(z, mod_z, conv_win[j], tm=tm_z)
            w_out, conv_taps = conv_wout[j], conv_w[j]
        else:
            zero_b = jnp.zeros((full_wqkv.shape[2],), F32)
            norms = (full_qnorm[j], full_knorm[j])
            qx, kx, vx = _qkv_project(x, mod_x, full_wqkv[j], zero_b, tables=tables, norms=norms, tm=tm_x)
            qz, kz, vz = _qkv_project(z, mod_z, full_wqkv[j], zero_b, norms=norms, tm=tm_z)
            k_all = jnp.concatenate([kx, kz], axis=1)
            v_all = jnp.concatenate([vx, vz], axis=1)
            ox = _dense_attention(qx, k_all, v_all, tq=tq_full)
            if need_ctx:
                oz = _dense_attention(qz, kz, vz, tq=ctx_len)
            w_out, conv_taps = full_wo[j], None

        x, hx, gx, ix, rx, cx = _mixer_out(ox, w_out, x, mod_x, *ln1, *route_args, alpha=alpha, tm=tm_x,
                                           conv_w=conv_taps)
        n_x = bsz * seq
        idx, rank, cnt = ix.reshape(n_x, TOP_K), rx.reshape(n_x, TOP_K), cx[:, 0]
        tile_rows = [(tm_x, n_x // tm_x)]
        if need_ctx:
            z, hz, gz, iz, rz, cz = _mixer_out(oz, w_out, z, mod_z, *ln1, *route_args, alpha=alpha, tm=tm_z,
                                               conv_w=conv_taps)
            n_z = bsz * ctx_len
            idx = jnp.concatenate([idx, iz.reshape(n_z, TOP_K)], axis=0)
            rank = jnp.concatenate([rank, rz.reshape(n_z, TOP_K)], axis=0)
            cnt = jnp.concatenate([cnt, cz[:, 0]], axis=0)
            tile_rows.append((tm_z, n_z // tm_z))

        dest, nblk, blk_expert, blk_first, n_used, last_blk = _route(idx, rank, cnt, tile_rows, bm)
        rt = d // LANES
        dest_rows = dest * rt
        streams = [(hx, dest_rows[:n_x], tm_x)]
        if need_ctx:
            streams.append((hz, dest_rows[n_x:], tm_z))
        xs = _dispatch(streams, last_blk, n_used, nblk, bm=bm, rt=rt)
        out = _expert_ffn(xs, blk_expert, blk_first, n_used, i,
                          expert_wgu, expert_bgu, expert_wdown, expert_bdown, bm=bm)
        x = _combine(out, dest_rows[:n_x], gx, x, mod_x, ln2_g[i], ln2_b[i], alpha=alpha, tm=tm_x)
        if need_ctx:
            z = _combine(out, dest_rows[n_x:], gz, z, mod_z, ln2_g[i], ln2_b[i], alpha=alpha, tm=tm_z)
    return x
```
